```python
import math
import jax, jax.numpy as jnp
from jax import lax
import numpy as np

D_MODEL = 2048
BATCH = 2
SEQ = 4096
DEPTH = 2
DEC_BATCH = 8
DEC_SEQ = 1
PAST_LEN = 16384
PAGE_SIZE = 128

N_MIXERS = 2
N_FOX = (DEPTH + 1) // 2
N_RET = DEPTH // 2
FOX_HEADS = 16
FOX_HEAD_DIM = D_MODEL // FOX_HEADS
RET_HEADS = 8
RET_KEY_DIM = D_MODEL // RET_HEADS
RET_VAL_DIM = 2 * D_MODEL // RET_HEADS
D_FF = 4 * D_MODEL
Q_BLOCK = 128
RET_CHUNK = 128
ROPE_BASE = 10000.0
EPS = 1e-6
FORGET_BIAS_MIN = 2.0
FORGET_BIAS_MAX = 9.0
POOL_NUM = 5
POOL_DEN = 4

kernel_name = "fox_retention_hybrid_step"

F32 = jnp.float32


def _rmsnorm(x, g):
    xf = x.astype(F32)
    y = xf * lax.rsqrt(jnp.mean(xf * xf, axis=-1, keepdims=True) + EPS)
    return (y * g.astype(F32)).astype(x.dtype)


def _ada(c, w, b):
    m = jax.nn.silu(c) @ w + b
    return [t[:, None, :] for t in jnp.split(m, 6, axis=-1)]


def _modulate(x, g, shift, scale):
    return _rmsnorm(x, g) * (1 + scale) + shift


def _mlp(h, w_up, w_down):
    return jnp.square(jax.nn.relu(h @ w_up)) @ w_down


def _fox_project(h, w_in, b_f):
    B, T, _ = h.shape
    z = h @ w_in
    q, k, v, fl = jnp.split(z, [D_MODEL, 2 * D_MODEL, 3 * D_MODEL], axis=-1)
    shp = (B, T, FOX_HEADS, FOX_HEAD_DIM)
    logf = jax.nn.log_sigmoid((fl + b_f).astype(F32))
    return q.reshape(shp), k.reshape(shp), v.reshape(shp), logf


def _fox_prompt(q, k, v, logf):
    B, T, H, Dh = q.shape
    nb = T // Q_BLOCK
    scale = Dh ** -0.5
    c = jnp.cumsum(logf, axis=1)
    cT = jnp.transpose(c, (0, 2, 1))
    kpos = jnp.arange(T)

    def block(i):
        start = i * Q_BLOCK
        qb = lax.dynamic_slice_in_dim(q, start, Q_BLOCK, axis=1)
        cb = lax.dynamic_slice_in_dim(cT, start, Q_BLOCK, axis=2)
        s = jnp.einsum('bqhd,bkhd->bhqk', qb, k).astype(F32) * scale
        s = s + cb[..., :, None] - cT[:, :, None, :]
        qpos = start + jnp.arange(Q_BLOCK)
        s = jnp.where(kpos[None, :] <= qpos[:, None], s, -jnp.inf)
        p = jax.nn.softmax(s, axis=-1).astype(v.dtype)
        return jnp.einsum('bhqk,bkhd->bqhd', p, v)

    o = lax.map(block, jnp.arange(nb))
    return jnp.moveaxis(o, 0, 1).reshape(B, T, H, Dh)


def _fox_sample(q, k_new, v_new, logf_new, ck, cv, clf, page_table):
    DB, T, H, Dh = q.shape
    n_pages = page_table.shape[1]
    P = n_pages * ck.shape[1]
    scale = Dh ** -0.5
    kp = ck[page_table].reshape(DB, P, H, Dh)
    vp = cv[page_table].reshape(DB, P, H, Dh)
    lfp = clf[page_table].reshape(DB, P, H).astype(F32)
    r = lax.cumsum(lfp, axis=1, reverse=True) - lfp
    cn = jnp.cumsum(logf_new, axis=1)
    cnT = jnp.transpose(cn, (0, 2, 1))
    rT = jnp.transpose(r, (0, 2, 1))
    s_past = jnp.einsum('bqhd,bkhd->bhqk', q, kp).astype(F32) * scale + cnT[..., :, None] + rT[:, :, None, :]
    s_new = jnp.einsum('bqhd,bkhd->bhqk', q, k_new).astype(F32) * scale + cnT[..., :, None] - cnT[:, :, None, :]
    tpos = jnp.arange(T)
    s_new = jnp.where(tpos[None, :] <= tpos[:, None], s_new, -jnp.inf)
    p = jax.nn.softmax(jnp.concatenate([s_past, s_new], axis=-1), axis=-1).astype(v_new.dtype)
    return (jnp.einsum('bhqk,bkhd->bqhd', p[..., :P], vp)
            + jnp.einsum('bhqk,bkhd->bqhd', p[..., P:], v_new))


def _ret_log_gamma():
    return jnp.log1p(-jnp.exp2(-5.0 - jnp.arange(RET_HEADS, dtype=F32)))


def _rotate(x, pos):
    half = x.shape[-1] // 2
    inv = 1.0 / (ROPE_BASE ** jnp.linspace(0.0, 1.0, half, dtype=F32))
    ang = pos.astype(F32)[:, None] * inv[None, :]
    cos = jnp.cos(ang)[None, :, None, :]
    sin = jnp.sin(ang)[None, :, None, :]
    x1, x2 = x[..., :half], x[..., half:]
    return jnp.concatenate([x1 * cos - x2 * sin, x1 * sin + x2 * cos], axis=-1)


def _ret_project(h, w_in, pos):
    B, T, _ = h.shape
    z = h @ w_in
    q, k, v, g = jnp.split(z, [D_MODEL, 2 * D_MODEL, 4 * D_MODEL], axis=-1)
    q = _rotate(q.reshape(B, T, RET_HEADS, RET_KEY_DIM).astype(F32), pos)
    k = _rotate(k.reshape(B, T, RET_HEADS, RET_KEY_DIM).astype(F32), pos) * (RET_KEY_DIM ** -0.5)
    v = v.reshape(B, T, RET_HEADS, RET_VAL_DIM).astype(F32)
    return q, k, v, g


def _ret_chunk(q, k, v, S0, lg):
    L = q.shape[1]
    n = jnp.arange(L, dtype=F32)
    diff = n[:, None] - n[None, :]
    dec = jnp.where(diff >= 0, jnp.exp(lg[:, None, None] * jnp.maximum(diff, 0.0)), 0.0)
    a = jnp.einsum('blhk,bmhk->bhlm', q, k) * dec[None]
    xi = jnp.exp(lg[None, :] * (n[:, None] + 1.0))
    o = jnp.einsum('bhlm,bmhv->blhv', a, v) + jnp.einsum('blhk,bhkv->blhv', q, S0) * xi[None, :, :, None]
    zeta = jnp.exp(lg[None, :] * (L - 1.0 - n[:, None]))
    S = jnp.exp(lg * L)[None, :, None, None] * S0 + jnp.einsum('bmhk,bmhv->bhkv', k * zeta[None, :, :, None], v)
    return o, S


def _ret_prompt(q, k, v, lg):
    B, T, H, Dk = q.shape
    Dv = v.shape[-1]
    nc = T // RET_CHUNK

    def to_chunks(a):
        return jnp.moveaxis(a.reshape(B, nc, RET_CHUNK, *a.shape[2:]), 1, 0)

    def step(S, qkv):
        o, S = _ret_chunk(qkv[0], qkv[1], qkv[2], S, lg)
        return S, o

    S, o = lax.scan(step, jnp.zeros((B, H, Dk, Dv), F32), (to_chunks(q), to_chunks(k), to_chunks(v)))
    return jnp.moveaxis(o, 0, 1).reshape(B, T, H, Dv), S


def _ret_output(o, g, w_out):
    mu = jnp.mean(o, axis=-1, keepdims=True)
    var = jnp.mean(jnp.square(o - mu), axis=-1, keepdims=True)
    o = (o - mu) * lax.rsqrt(var + EPS)
    B, T = o.shape[:2]
    y = jax.nn.silu(g.astype(F32)) * o.reshape(B, T, -1)
    return y.astype(g.dtype) @ w_out


def setup_inputs(seed: int = 0) -> dict:
    key = jax.random.key(seed)
    ks = jax.random.split(key, 24)
    n_pages = PAST_LEN // PAGE_SIZE
    n_used = DEC_BATCH * n_pages
    n_pool = (n_used * POOL_NUM + POOL_DEN - 1) // POOL_DEN

    def nrm(k, shape, scale=1.0):
        return jax.random.normal(k, shape, F32) * scale

    perm = jax.random.permutation(ks[0], n_pool)
    page_table = perm[:n_used].reshape(DEC_BATCH, n_pages).astype(jnp.int32)
    D = D_MODEL
    head_bias = jnp.linspace(FORGET_BIAS_MIN, FORGET_BIAS_MAX, FOX_HEADS, dtype=F32)
    return {
        "x_prompt": nrm(ks[1], (BATCH, SEQ, D)),
        "x_sample": nrm(ks[2], (DEC_BATCH, DEC_SEQ, D)),
        "c_prompt": nrm(ks[3], (BATCH, D)),
        "c_sample": nrm(ks[4], (DEC_BATCH, D)),
        "cache_fox_k": nrm(ks[5], (N_FOX, n_pool, PAGE_SIZE, FOX_HEADS, FOX_HEAD_DIM)),
        "cache_fox_v": nrm(ks[6], (N_FOX, n_pool, PAGE_SIZE, FOX_HEADS, FOX_HEAD_DIM)),
        "cache_fox_logf": jax.nn.log_sigmoid(head_bias + nrm(ks[7], (N_FOX, n_pool, PAGE_SIZE, FOX_HEADS), 0.5)),
        "state_ret": nrm(ks[8], (N_RET, DEC_BATCH, RET_HEADS, RET_KEY_DIM, RET_VAL_DIM), 0.5),
        "page_table": page_table,
        "norm1_g": 1.0 + nrm(ks[9], (DEPTH, D), 0.05),
        "norm2_g": 1.0 + nrm(ks[10], (DEPTH, D), 0.05),
        "ada_w": nrm(ks[11], (DEPTH, D, 6 * D), 0.5 * D ** -0.5),
        "ada_b": nrm(ks[12], (DEPTH, 6 * D), 0.01),
        "fox_w_in": nrm(ks[13], (N_FOX, D, 3 * D + FOX_HEADS), D ** -0.5),
        "fox_b_f": head_bias[None, :] + nrm(ks[14], (N_FOX, FOX_HEADS), 0.1),
        "fox_w_out": nrm(ks[15], (N_FOX, D, D), D ** -0.5),
        "ret_w_in": nrm(ks[16], (N_RET, D, 6 * D), D ** -0.5),
        "ret_w_out": nrm(ks[17], (N_RET, 2 * D, D), (2 * D) ** -0.5),
        "mlp_w_up": nrm(ks[18], (DEPTH, D, D_FF), D ** -0.5),
        "mlp_w_down": nrm(ks[19], (DEPTH, D_FF, D), D_FF ** -0.5),
        "final_g": 1.0 + nrm(ks[20], (D,), 0.05),
    }


def reference(x_prompt, x_sample, c_prompt, c_sample, cache_fox_k, cache_fox_v, cache_fox_logf,
              state_ret, page_table, norm1_g, norm2_g, ada_w, ada_b, fox_w_in, fox_b_f, fox_w_out,
              ret_w_in, ret_w_out, mlp_w_up, mlp_w_down, final_g):
    Bp, Tp, _ = x_prompt.shape
    Bs, Ts, _ = x_sample.shape
    past_len = page_table.shape[1] * cache_fox_k.shape[2]
    pos_p = jnp.arange(Tp)
    pos_s = past_len + jnp.arange(Ts)
    lg = _ret_log_gamma()

    yp, ys = x_prompt, x_sample
    kp_l, vp_l, lfp_l, ks_l, vs_l, lfs_l, sp_l, ss_l = [], [], [], [], [], [], [], []
    for i in range(DEPTH):
        mp = _ada(c_prompt, ada_w[i], ada_b[i])
        ms = _ada(c_sample, ada_w[i], ada_b[i])
        hp = _modulate(yp, norm1_g[i], mp[0], mp[1])
        hs = _modulate(ys, norm1_g[i], ms[0], ms[1])
        j = i // N_MIXERS
        if i % N_MIXERS == 0:
            q, k, v, lf = _fox_project(hp, fox_w_in[j], fox_b_f[j])
            op = _fox_prompt(q, k, v, lf).reshape(Bp, Tp, D_MODEL) @ fox_w_out[j]
            kp_l.append(k); vp_l.append(v); lfp_l.append(lf)
            q, k, v, lf = _fox_project(hs, fox_w_in[j], fox_b_f[j])
            os_ = _fox_sample(q, k, v, lf, cache_fox_k[j], cache_fox_v[j], cache_fox_logf[j],
                              page_table).reshape(Bs, Ts, D_MODEL) @ fox_w_out[j]
            ks_l.append(k); vs_l.append(v); lfs_l.append(lf)
        else:
            q, k, v, g = _ret_project(hp, ret_w_in[j], pos_p)
            o, S = _ret_prompt(q, k, v, lg)
            op = _ret_output(o, g, ret_w_out[j])
            sp_l.append(S)
            q, k, v, g = _ret_project(hs, ret_w_in[j], pos_s)
            o, S = _ret_chunk(q, k, v, state_ret[j].astype(F32), lg)
            os_ = _ret_output(o, g, ret_w_out[j])
            ss_l.append(S)
        yp = yp + mp[2] * op
        ys = ys + ms[2] * os_
        yp = yp + mp[5] * _mlp(_modulate(yp, norm2_g[i], mp[3], mp[4]), mlp_w_up[i], mlp_w_down[i])
        ys = ys + ms[5] * _mlp(_modulate(ys, norm2_g[i], ms[3], ms[4]), mlp_w_up[i], mlp_w_down[i])

    y_prompt = _rmsnorm(yp, final_g)
    y_sample = _rmsnorm(ys, final_g)
    return (y_prompt, y_sample,
            jnp.stack(kp_l), jnp.stack(vp_l), jnp.stack(lfp_l),
            jnp.stack(ks_l), jnp.stack(vs_l), jnp.stack(lfs_l),
            jnp.stack(sp_l), jnp.stack(ss_l))
```

```python
import functools

import jax
import jax.numpy as jnp
from jax import lax
from jax.experimental import pallas as pl
from jax.experimental.pallas import tpu as pltpu

F32 = jnp.float32
BF16 = jnp.bfloat16

EPS = 1e-6
ROPE_BASE = 10000.0
NEG_BIG = -1e30

V7X_VMEM_LIMIT_BYTES = 60000 * 1024
BF16_SUBLANES = 16

ROW_TILE = 1024
COL_TILE = 512
ATTN_Q_TILE = 256
ATTN_K_TILE = 512
CUMSUM_TILE = 512
RET_CHUNK = 256
PAGES_PER_STEP = 2


def _params(semantics, vmem_bytes):
    return pltpu.CompilerParams(dimension_semantics=semantics,
                                vmem_limit_bytes=int(min(vmem_bytes, V7X_VMEM_LIMIT_BYTES)))


def _nbytes(shape, dtype):
    n = 1
    for s in shape:
        n *= s
    return n * jnp.dtype(dtype).itemsize


def _norm_mod(x, g, shift, scale):
    y = x * lax.rsqrt(jnp.mean(x * x, axis=-1, keepdims=True) + EPS)
    return (y * g) * (1.0 + scale) + shift


def _silu(x):
    return x * jax.nn.sigmoid(x)


def _log_sigmoid(x):
    return jnp.minimum(x, 0.0) - jnp.log1p(jnp.exp(-jnp.abs(x)))


def _split3(x):
    hi = x.astype(BF16)
    r1 = x - hi.astype(F32)
    mid = r1.astype(BF16)
    lo = (r1 - mid.astype(F32)).astype(BF16)
    return hi, mid, lo


def _mod_spec(rows, width, chunk, tiles_per_group):
    return pl.BlockSpec((None, rows, width), lambda m, n: (m // tiles_per_group, 0, chunk))


def _ada_kernel(c_ref, w_ref, b_ref, o_ref):
    o_ref[...] = jnp.dot(_silu(c_ref[...]), w_ref[...], preferred_element_type=F32) + b_ref[...]


def _ada(c_all, ada_w, ada_b):
    depth, d, n6 = ada_w.shape
    rows = c_all.shape[0]
    tn = min(1024, n6)
    vmem = 2 * (_nbytes((d, tn), F32) + _nbytes((rows, tn), F32)) + _nbytes((rows, d), F32) * 2 + (4 << 20)
    return pl.pallas_call(
        _ada_kernel,
        out_shape=jax.ShapeDtypeStruct((depth, rows, n6), F32),
        grid=(depth, n6 // tn),
        in_specs=[pl.BlockSpec((rows, d), lambda i, n: (0, 0)),
                  pl.BlockSpec((None, d, tn), lambda i, n: (i, 0, n)),
                  pl.BlockSpec((None, 1, tn), lambda i, n: (i, 0, n))],
        out_specs=pl.BlockSpec((None, rows, tn), lambda i, n: (i, 0, n)),
        compiler_params=_params(("arbitrary", "arbitrary"), vmem),
        name="ada_modulation",
    )(c_all, ada_w, ada_b.reshape(depth, 1, n6))


def _fox_proj_kernel(x_ref, g_ref, sh_ref, sc_ref, w_ref, wf_ref, bf_ref,
                     q_ref, kf_ref, vf_ref, kb_ref, vb_ref, lf_ref, h_ref, *, nq, q_scale):
    n = pl.program_id(1)

    @pl.when(n == 0)
    def _():
        h = _norm_mod(x_ref[...], g_ref[...], sh_ref[...], sc_ref[...]).astype(BF16)
        h_ref[...] = h
        fl = jnp.dot(h, wf_ref[...], preferred_element_type=F32) + bf_ref[...]
        lf_ref[...] = _log_sigmoid(fl)

    z = jnp.dot(h_ref[...], w_ref[...], preferred_element_type=F32)

    @pl.when(n < nq)
    def _():
        q_ref[...] = (z * q_scale).astype(BF16)

    @pl.when(jnp.logical_and(n >= nq, n < 2 * nq))
    def _():
        kf_ref[...] = z
        kb_ref[...] = z.astype(BF16)

    @pl.when(n >= 2 * nq)
    def _():
        vf_ref[...] = z
        vb_ref[...] = z.astype(BF16)


def _fox_proj(x, g, mods, mod_rows, tiles_per_group, w_qkv, w_f, b_f, tm, head_dim):
    m_rows, d = x.shape
    heads = w_f.shape[1]
    tn = min(COL_TILE, d)
    nq = d // tn
    kernel = functools.partial(_fox_proj_kernel, nq=nq, q_scale=head_dim ** -0.5)

    def col(lo):
        return lambda m, n: (m, jnp.clip(n - lo, 0, nq - 1))

    vmem = (2 * _nbytes((tm, d), F32) + _nbytes((tm, d), BF16) + 2 * _nbytes((d, tn), BF16)
            + 2 * (3 * _nbytes((tm, tn), BF16) + 2 * _nbytes((tm, tn), F32))
            + 3 * _nbytes((tm, tn), F32) + (4 << 20))
    out_f = jax.ShapeDtypeStruct((m_rows, d), F32)
    out_b = jax.ShapeDtypeStruct((m_rows, d), BF16)
    return pl.pallas_call(
        kernel,
        out_shape=(out_b, out_f, out_f, out_b, out_b, jax.ShapeDtypeStruct((m_rows, heads), F32)),
        grid=(m_rows // tm, 3 * nq),
        in_specs=[pl.BlockSpec((tm, d), lambda m, n: (m, 0)),
                  pl.BlockSpec((1, d), lambda m, n: (0, 0)),
                  _mod_spec(mod_rows, d, 0, tiles_per_group),
                  _mod_spec(mod_rows, d, 1, tiles_per_group),
                  pl.BlockSpec((d, tn), lambda m, n: (0, n)),
                  pl.BlockSpec((d, heads), lambda m, n: (0, 0)),
                  pl.BlockSpec((1, heads), lambda m, n: (0, 0))],
        out_specs=(pl.BlockSpec((tm, tn), col(0)),
                   pl.BlockSpec((tm, tn), col(nq)),
                   pl.BlockSpec((tm, tn), col(2 * nq)),
                   pl.BlockSpec((tm, tn), col(nq)),
                   pl.BlockSpec((tm, tn), col(2 * nq)),
                   pl.BlockSpec((tm, heads), lambda m, n: (m, 0))),
        scratch_shapes=[pltpu.VMEM((tm, d), BF16)],
        compiler_params=_params(("arbitrary", "arbitrary"), vmem),
        name="fox_in_proj",
    )(x, g, mods, mods, w_qkv, w_f, b_f)


def _cumsum_kernel(lf_ref, c_ref, carry_ref):
    @pl.when(pl.program_id(1) == 0)
    def _():
        carry_ref[...] = jnp.zeros_like(carry_ref)

    tc = lf_ref.shape[0]
    row = lax.broadcasted_iota(jnp.int32, (tc, tc), 0)
    col = lax.broadcasted_iota(jnp.int32, (tc, tc), 1)
    tri = jnp.where(col <= row, 1.0, 0.0).astype(BF16)
    c = carry_ref[...]
    for piece in _split3(lf_ref[...]):
        c = c + jnp.dot(tri, piece, preferred_element_type=F32)
    c_ref[...] = c
    carry_ref[...] = c[tc - 1:tc, :]


def _cumsum(logf, batch, seq):
    rows, heads = logf.shape
    tc = min(CUMSUM_TILE, seq)
    per = seq // tc
    return pl.pallas_call(
        _cumsum_kernel,
        out_shape=jax.ShapeDtypeStruct((rows, heads), F32),
        grid=(batch, per),
        in_specs=[pl.BlockSpec((tc, heads), lambda b, t: (b * per + t, 0))],
        out_specs=pl.BlockSpec((tc, heads), lambda b, t: (b * per + t, 0)),
        scratch_shapes=[pltpu.VMEM((1, heads), F32)],
        compiler_params=_params(("arbitrary", "arbitrary"), 16 << 20),
        name="fox_logf_cumsum",
    )(logf)


def _fox_attn_kernel(q_ref, k_ref, v_ref, c_ref, o_ref, *, tq, tk):
    i = pl.program_id(2)
    q0 = pl.multiple_of(i * tq, tq)
    q = q_ref[...]
    hd = q.shape[1]
    c_first = c_ref[:, pl.ds(q0, tq)][:, 0:1]

    def step(j, carry, masked):
        m, l, acc = carry
        k0 = pl.multiple_of(j * tk, tk)
        kb = k_ref[pl.ds(k0, tk), :]
        vb = v_ref[pl.ds(k0, tk), :]
        s = lax.dot_general(q, kb, (((1,), (1,)), ((), ())), preferred_element_type=F32)
        s = s + (c_first - c_ref[:, pl.ds(k0, tk)])
        if masked:
            rows = q0 + lax.broadcasted_iota(jnp.int32, (tq, tk), 0)
            cols = k0 + lax.broadcasted_iota(jnp.int32, (tq, tk), 1)
            s = jnp.where(cols <= rows, s, NEG_BIG)
        m_new = jnp.maximum(m, jnp.max(s, axis=-1, keepdims=True))
        alpha = jnp.exp(m - m_new)
        p = jnp.exp(s - m_new)
        l = alpha * l + jnp.sum(p, axis=-1, keepdims=True)
        acc = alpha * acc + jnp.dot(p.astype(BF16), vb, preferred_element_type=F32)
        return m_new, l, acc

    init = (jnp.full((tq, 1), NEG_BIG, F32), jnp.zeros((tq, 1), F32), jnp.zeros((tq, hd), F32))
    n_full = q0 // tk
    carry = lax.fori_loop(0, n_full, lambda j, c: step(j, c, False), init)
    _, l, acc = step(n_full, carry, True)
    o_ref[...] = (acc / l).astype(o_ref.dtype)


def _fox_attn(q, k, v, c_t, batch, seq, heads, head_dim):
    tq = min(ATTN_Q_TILE, seq)
    tk = min(ATTN_K_TILE, seq)
    nq = seq // tq
    kernel = functools.partial(_fox_attn_kernel, tq=tq, tk=tk)
    vmem = 4 * _nbytes((seq, head_dim), BF16) + 8 * _nbytes((tq, tk), F32) + (8 << 20)
    return pl.pallas_call(
        kernel,
        out_shape=jax.ShapeDtypeStruct(q.shape, BF16),
        grid=(batch, heads, nq),
        in_specs=[pl.BlockSpec((tq, head_dim), lambda b, h, i: (b * nq + i, h)),
                  pl.BlockSpec((seq, head_dim), lambda b, h, i: (b, h)),
                  pl.BlockSpec((seq, head_dim), lambda b, h, i: (b, h)),
                  pl.BlockSpec((None, None, 1, seq), lambda b, h, i: (b, h, 0, 0))],
        out_specs=pl.BlockSpec((tq, head_dim), lambda b, h, i: (b * nq + i, h)),
        compiler_params=_params(("arbitrary", "arbitrary", "arbitrary"), vmem),
        name="fox_prompt_attention",
    )(q, k, v, c_t)


def _fox_decode_kernel(pt_ref, q_ref, kn_ref, vn_ref, cn_ref, *refs, pages, heads, head_dim):
    k_refs = refs[:pages]
    v_refs = refs[pages:2 * pages]
    lf_refs = refs[2 * pages:3 * pages]
    o_ref, qbd_ref, m_ref, l_ref, acc_ref, run_ref = refs[3 * pages:]
    s_idx = pl.program_id(1)
    d = heads * head_dim
    page = k_refs[0].shape[0]
    head_mask = (lax.broadcasted_iota(jnp.int32, (heads, d), 1) // head_dim
                 == lax.broadcasted_iota(jnp.int32, (heads, d), 0))

    @pl.when(s_idx == 0)
    def _():
        qbd_ref[...] = jnp.where(head_mask, jnp.broadcast_to(q_ref[...], (heads, d)), 0.0).astype(BF16)
        m_ref[...] = jnp.full_like(m_ref, NEG_BIG)
        l_ref[...] = jnp.zeros_like(l_ref)
        acc_ref[...] = jnp.zeros_like(acc_ref)
        run_ref[...] = jnp.zeros_like(run_ref)

    later = (lax.broadcasted_iota(jnp.int32, (page, page), 0)
             > lax.broadcasted_iota(jnp.int32, (page, page), 1))
    later = jnp.where(later, 1.0, 0.0).astype(BF16)
    qbd = qbd_ref[...]
    cn = cn_ref[...]
    for g in range(pages):
        lf_t = lf_refs[g][...]
        suffix = jnp.zeros((heads, page), F32)
        for piece in _split3(lf_t):
            suffix = suffix + jnp.dot(piece, later, preferred_element_type=F32)
        s = lax.dot_general(qbd, k_refs[g][...].astype(BF16), (((1,), (1,)), ((), ())),
                            preferred_element_type=F32)
        s = s + (suffix + (run_ref[...] + cn))
        m = m_ref[...]
        m_new = jnp.maximum(m, jnp.max(s, axis=-1, keepdims=True))
        alpha = jnp.exp(m - m_new)
        p = jnp.exp(s - m_new)
        l_ref[...] = alpha * l_ref[...] + jnp.sum(p, axis=-1, keepdims=True)
        acc_ref[...] = alpha * acc_ref[...] + jnp.dot(p.astype(BF16), v_refs[g][...].astype(BF16),
                                                      preferred_element_type=F32)
        m_ref[...] = m_new
        run_ref[...] = run_ref[...] + jnp.sum(lf_t, axis=-1, keepdims=True)

    @pl.when(s_idx == pl.num_programs(1) - 1)
    def _():
        s_new = jnp.sum(qbd_ref[...].astype(F32) * kn_ref[...], axis=-1, keepdims=True)
        m = m_ref[...]
        m_new = jnp.maximum(m, s_new)
        alpha = jnp.exp(m - m_new)
        p_new = jnp.exp(s_new - m_new)
        l = alpha * l_ref[...] + p_new
        acc = alpha * acc_ref[...] + p_new.astype(BF16).astype(F32) * vn_ref[...]
        o_ref[...] = jnp.sum(jnp.where(head_mask, acc / l, 0.0), axis=0, keepdims=True)


def _fox_decode(page_table, q, k_new, v_new, logf_new, cache_k, cache_v, cache_lf_t, heads, head_dim):
    bs, n_pages = page_table.shape
    n_pool, page, d = cache_k.shape
    g_pages = PAGES_PER_STEP if n_pages % PAGES_PER_STEP == 0 else 1
    steps = n_pages // g_pages
    kernel = functools.partial(_fox_decode_kernel, pages=g_pages, heads=heads, head_dim=head_dim)

    def page_idx(g):
        return lambda b, s, pt: (pt[b * n_pages + (n_pages - 1 - (s * g_pages + g))], 0, 0)

    row = pl.BlockSpec((None, 1, d), lambda b, s, pt: (b, 0, 0))
    in_specs = [row, row, row, pl.BlockSpec((None, heads, 1), lambda b, s, pt: (b, 0, 0))]
    in_specs += [pl.BlockSpec((None, page, d), page_idx(g)) for g in range(g_pages)]
    in_specs += [pl.BlockSpec((None, page, d), page_idx(g)) for g in range(g_pages)]
    in_specs += [pl.BlockSpec((None, heads, page), page_idx(g)) for g in range(g_pages)]
    vmem = 4 * g_pages * _nbytes((page, d), F32) + 3 * g_pages * _nbytes((page, d), BF16) + (8 << 20)
    grid_spec = pltpu.PrefetchScalarGridSpec(
        num_scalar_prefetch=1,
        grid=(bs, steps),
        in_specs=in_specs,
        out_specs=pl.BlockSpec((None, 1, d), lambda b, s, pt: (b, 0, 0)),
        scratch_shapes=[pltpu.VMEM((heads, d), BF16), pltpu.VMEM((heads, 1), F32),
                        pltpu.VMEM((heads, 1), F32), pltpu.VMEM((heads, d), F32),
                        pltpu.VMEM((heads, 1), F32)])
    return pl.pallas_call(
        kernel,
        out_shape=jax.ShapeDtypeStruct((bs, 1, d), F32),
        grid_spec=grid_spec,
        compiler_params=_params(("arbitrary", "arbitrary"), vmem),
        name="fox_decode_attention",
    )(page_table.reshape(-1), q, k_new, v_new, logf_new,
      *([cache_k] * g_pages), *([cache_v] * g_pages), *([cache_lf_t] * g_pages))


def _proj_residual_kernel(a_ref, w_ref, x_ref, gate_ref, o_ref):
    z = jnp.dot(a_ref[...], w_ref[...], preferred_element_type=F32)
    o_ref[...] = x_ref[...] + gate_ref[...] * z


def _proj_residual(a, w, x, mods, mod_rows, tiles_per_group, gate_chunk, tm):
    m_rows, k = a.shape
    d = w.shape[1]
    tn = min(COL_TILE, d)
    vmem = (2 * _nbytes((tm, k), BF16) + 2 * _nbytes((k, tn), BF16) + 5 * _nbytes((tm, tn), F32)
            + (4 << 20))
    return pl.pallas_call(
        _proj_residual_kernel,
        out_shape=jax.ShapeDtypeStruct((m_rows, d), F32),
        grid=(m_rows // tm, d // tn),
        in_specs=[pl.BlockSpec((tm, k), lambda m, n: (m, 0)),
                  pl.BlockSpec((k, tn), lambda m, n: (0, n)),
                  pl.BlockSpec((tm, tn), lambda m, n: (m, n)),
                  pl.BlockSpec((None, mod_rows, tn),
                               lambda m, n: (m // tiles_per_group, 0, gate_chunk * (d // tn) + n))],
        out_specs=pl.BlockSpec((tm, tn), lambda m, n: (m, n)),
        compiler_params=_params(("arbitrary", "arbitrary"), vmem),
        name="mixer_out_proj",
    )(a, w, x, mods)


def _mlp_kernel(x_ref, g_ref, sh_ref, sc_ref, gate_ref, wu_ref, wd_ref, fg_ref, o_ref, h_ref, *,
                final_norm):
    f = pl.program_id(1)

    @pl.when(f == 0)
    def _():
        h_ref[...] = _norm_mod(x_ref[...], g_ref[...], sh_ref[...], sc_ref[...]).astype(BF16)
        o_ref[...] = jnp.zeros_like(o_ref)

    u = jnp.dot(h_ref[...], wu_ref[...], preferred_element_type=F32)
    u = jnp.square(jnp.maximum(u, 0.0)).astype(BF16)
    o_ref[...] += jnp.dot(u, wd_ref[...], preferred_element_type=F32)

    @pl.when(f == pl.num_programs(1) - 1)
    def _():
        y = x_ref[...] + gate_ref[...] * o_ref[...]
        if final_norm:
            y = y * lax.rsqrt(jnp.mean(y * y, axis=-1, keepdims=True) + EPS) * fg_ref[...]
        o_ref[...] = y


def _mlp(x, g, mods, mod_rows, tiles_per_group, w_up, w_down, final_g, final_norm, tm):
    m_rows, d = x.shape
    d_ff = w_up.shape[1]
    tf = min(COL_TILE, d_ff)
    kernel = functools.partial(_mlp_kernel, final_norm=final_norm)
    vmem = (4 * _nbytes((tm, d), F32) + _nbytes((tm, d), BF16) + 4 * _nbytes((d, tf), BF16)
            + 4 * _nbytes((tm, tf), F32) + (4 << 20))
    vec = pl.BlockSpec((1, d), lambda m, f: (0, 0))
    return pl.pallas_call(
        kernel,
        out_shape=jax.ShapeDtypeStruct((m_rows, d), F32),
        grid=(m_rows // tm, d_ff // tf),
        in_specs=[pl.BlockSpec((tm, d), lambda m, f: (m, 0)),
                  vec,
                  _mod_spec(mod_rows, d, 3, tiles_per_group),
                  _mod_spec(mod_rows, d, 4, tiles_per_group),
                  _mod_spec(mod_rows, d, 5, tiles_per_group),
                  pl.BlockSpec((d, tf), lambda m, f: (0, f)),
                  pl.BlockSpec((tf, d), lambda m, f: (f, 0)),
                  vec],
        out_specs=pl.BlockSpec((tm, d), lambda m, f: (m, 0)),
        scratch_shapes=[pltpu.VMEM((tm, d), BF16)],
        compiler_params=_params(("arbitrary", "arbitrary"), vmem),
        name="relu2_mlp",
    )(x, g, mods, mods, mods, w_up, w_down, final_g)


def _ret_proj_kernel(x_ref, g_ref, sh_ref, sc_ref, w_ref, cos_ref, sin_ref,
                     q_ref, k_ref, v_ref, gt_ref, h_ref, *, nq, key_dim, k_scale):
    n = pl.program_id(1)

    @pl.when(n == 0)
    def _():
        h_ref[...] = _norm_mod(x_ref[...], g_ref[...], sh_ref[...], sc_ref[...]).astype(BF16)

    z = jnp.dot(h_ref[...], w_ref[...], preferred_element_type=F32)
    half = key_dim // 2

    def rotate(zz):
        cos = cos_ref[...]
        sin = sin_ref[...]
        parts = []
        for hh in range(zz.shape[1] // key_dim):
            x1 = zz[:, hh * key_dim:hh * key_dim + half]
            x2 = zz[:, hh * key_dim + half:(hh + 1) * key_dim]
            parts += [x1 * cos - x2 * sin, x1 * sin + x2 * cos]
        return jnp.concatenate(parts, axis=1)

    @pl.when(n < nq)
    def _():
        q_ref[...] = rotate(z).astype(BF16)

    @pl.when(jnp.logical_and(n >= nq, n < 2 * nq))
    def _():
        k_ref[...] = rotate(z) * k_scale

    @pl.when(jnp.logical_and(n >= 2 * nq, n < 4 * nq))
    def _():
        v_ref[...] = z.astype(BF16)

    @pl.when(n >= 4 * nq)
    def _():
        gt_ref[...] = z


def _ret_proj(x, g, mods, mod_rows, tiles_per_group, w, cos, sin, pos_tiles, key_dim, tm):
    m_rows, d = x.shape
    tn = min(COL_TILE, d)
    nq = d // tn
    half = key_dim // 2
    kernel = functools.partial(_ret_proj_kernel, nq=nq, key_dim=key_dim, k_scale=key_dim ** -0.5)

    def col(lo, width):
        return lambda m, n: (m, jnp.clip(n - lo, 0, width - 1))

    vmem = (2 * _nbytes((tm, d), F32) + _nbytes((tm, d), BF16) + 2 * _nbytes((d, tn), BF16)
            + 2 * (2 * _nbytes((tm, tn), BF16) + 2 * _nbytes((tm, tn), F32))
            + 4 * _nbytes((tm, half), F32) + 4 * _nbytes((tm, tn), F32) + (4 << 20))
    return pl.pallas_call(
        kernel,
        out_shape=(jax.ShapeDtypeStruct((m_rows, d), BF16), jax.ShapeDtypeStruct((m_rows, d), F32),
                   jax.ShapeDtypeStruct((m_rows, 2 * d), BF16), jax.ShapeDtypeStruct((m_rows, 2 * d), F32)),
        grid=(m_rows // tm, 6 * nq),
        in_specs=[pl.BlockSpec((tm, d), lambda m, n: (m, 0)),
                  pl.BlockSpec((1, d), lambda m, n: (0, 0)),
                  _mod_spec(mod_rows, d, 0, tiles_per_group),
                  _mod_spec(mod_rows, d, 1, tiles_per_group),
                  pl.BlockSpec((d, tn), lambda m, n: (0, n)),
                  pl.BlockSpec((tm, half), lambda m, n: (m % pos_tiles, 0)),
                  pl.BlockSpec((tm, half), lambda m, n: (m % pos_tiles, 0))],
        out_specs=(pl.BlockSpec((tm, tn), col(0, nq)),
                   pl.BlockSpec((tm, tn), col(nq, nq)),
                   pl.BlockSpec((tm, tn), col(2 * nq, 2 * nq)),
                   pl.BlockSpec((tm, tn), col(4 * nq, 2 * nq))),
        scratch_shapes=[pltpu.VMEM((tm, d), BF16)],
        compiler_params=_params(("arbitrary", "arbitrary"), vmem),
        name="ret_in_proj",
    )(x, g, mods, mods, w, cos, sin)


def _group_norm_gate(o, gate):
    mu = jnp.mean(o, axis=-1, keepdims=True)
    cen = o - mu
    var = jnp.mean(cen * cen, axis=-1, keepdims=True)
    return _silu(gate) * (cen * lax.rsqrt(var + EPS))


def _ret_chunk_kernel(lg_ref, q_ref, k_ref, v_ref, gt_ref, y_ref, s_ref):
    lg = jnp.full((1, 1), lg_ref[pl.program_id(1)], F32)

    @pl.when(pl.program_id(2) == 0)
    def _():
        s_ref[...] = jnp.zeros_like(s_ref)

    q = q_ref[...]
    k = k_ref[...]
    v = v_ref[...]
    ln = q.shape[0]
    diff = (lax.broadcasted_iota(jnp.int32, (ln, ln), 0)
            - lax.broadcasted_iota(jnp.int32, (ln, ln), 1)).astype(F32)
    dec = jnp.where(diff >= 0, jnp.exp(lg * jnp.maximum(diff, 0.0)), 0.0)
    n = lax.broadcasted_iota(jnp.int32, (ln, 1), 0).astype(F32)
    xi = jnp.exp(lg * (n + 1.0))
    zeta = jnp.exp(lg * (ln - 1.0 - n))

    s0 = s_ref[...]
    qk = lax.dot_general(q, k.astype(BF16), (((1,), (1,)), ((), ())), preferred_element_type=F32)
    a = (qk * dec).astype(BF16)
    o = (jnp.dot(a, v, preferred_element_type=F32)
         + jnp.dot(q, s0.astype(BF16), preferred_element_type=F32) * xi)
    kz = (k * zeta).astype(BF16)
    s_ref[...] = jnp.exp(lg * ln) * s0 + lax.dot_general(
        kz, v, (((0,), (0,)), ((), ())), preferred_element_type=F32)
    y_ref[...] = _group_norm_gate(o, gt_ref[...]).astype(y_ref.dtype)


def _ret_prompt(lg, q, k, v, gate, batch, seq, heads, key_dim, val_dim):
    ln = min(RET_CHUNK, seq)
    nc = seq // ln
    vmem = (2 * (_nbytes((ln, key_dim), BF16) + _nbytes((ln, key_dim), F32) + _nbytes((ln, val_dim), BF16)
                 + _nbytes((ln, val_dim), F32) + _nbytes((ln, val_dim), BF16))
            + 4 * _nbytes((key_dim, val_dim), F32) + 8 * _nbytes((ln, val_dim), F32) + (8 << 20))
    grid_spec = pltpu.PrefetchScalarGridSpec(
        num_scalar_prefetch=1,
        grid=(batch, heads, nc),
        in_specs=[pl.BlockSpec((ln, key_dim), lambda b, h, c, lg: (b * nc + c, h)),
                  pl.BlockSpec((ln, key_dim), lambda b, h, c, lg: (b * nc + c, h)),
                  pl.BlockSpec((ln, val_dim), lambda b, h, c, lg: (b * nc + c, h)),
                  pl.BlockSpec((ln, val_dim), lambda b, h, c, lg: (b * nc + c, h))],
        out_specs=(pl.BlockSpec((ln, val_dim), lambda b, h, c, lg: (b * nc + c, h)),
                   pl.BlockSpec((None, None, key_dim, val_dim), lambda b, h, c, lg: (b, h, 0, 0))))
    return pl.pallas_call(
        _ret_chunk_kernel,
        out_shape=(jax.ShapeDtypeStruct((batch * seq, heads * val_dim), BF16),
                   jax.ShapeDtypeStruct((batch, heads, key_dim, val_dim), F32)),
        grid_spec=grid_spec,
        compiler_params=_params(("arbitrary", "arbitrary", "arbitrary"), vmem),
        name="ret_prompt_chunks",
    )(lg, q, k, v, gate)


def _ret_decode_kernel(lg_ref, qc_ref, kc_ref, qr_ref, kr_ref, v_ref, gt_ref, s0_ref, y_ref, s_ref):
    gamma = jnp.exp(jnp.full((1, 1), lg_ref[pl.program_id(1)], F32))
    s0 = s0_ref[...]
    v = v_ref[...]
    a = jnp.sum(qr_ref[...] * kr_ref[...], axis=-1, keepdims=True)
    qs = jnp.sum(qc_ref[...] * s0, axis=0, keepdims=True)
    o = a * v + qs * gamma
    s_ref[...] = gamma * s0 + kc_ref[...] * v
    y_ref[...] = _group_norm_gate(o, gt_ref[...])


def _ret_decode(lg, q, k, v, gate, state):
    bs, heads, key_dim, val_dim = state.shape
    q4 = q.reshape(bs, heads, key_dim, 1)
    k4 = k.reshape(bs, heads, key_dim, 1)
    qr = q.reshape(bs, heads, 1, key_dim)
    kr = k.reshape(bs, heads, 1, key_dim)
    v4 = v.reshape(bs, heads, 1, val_dim)
    g4 = gate.reshape(bs, heads, 1, val_dim)

    def spec(r, c):
        return pl.BlockSpec((None, None, r, c), lambda b, h, lg: (b, h, 0, 0))

    grid_spec = pltpu.PrefetchScalarGridSpec(
        num_scalar_prefetch=1,
        grid=(bs, heads),
        in_specs=[spec(key_dim, 1), spec(key_dim, 1), spec(1, key_dim), spec(1, key_dim),
                  spec(1, val_dim), spec(1, val_dim), spec(key_dim, val_dim)],
        out_specs=(spec(1, val_dim), spec(key_dim, val_dim)))
    y, s = pl.pallas_call(
        _ret_decode_kernel,
        out_shape=(jax.ShapeDtypeStruct((bs, heads, 1, val_dim), F32),
                   jax.ShapeDtypeStruct(state.shape, F32)),
        grid_spec=grid_spec,
        compiler_params=_params(("arbitrary", "arbitrary"), 24 << 20),
        name="ret_decode_step",
    )(lg, q4, k4, qr, kr, v4, g4, state)
    return y.reshape(bs, heads * val_dim), s


def _rotary_tables(pos, key_dim):
    half = key_dim // 2
    inv = 1.0 / (ROPE_BASE ** jnp.linspace(0.0, 1.0, half, dtype=F32))
    ang = pos.astype(F32)[:, None] * inv[None, :]
    return jnp.cos(ang), jnp.sin(ang)


def _pad_rows(a, rows):
    return jnp.pad(a, ((0, rows - a.shape[0]), (0, 0)))


def kernel(x_prompt, x_sample, c_prompt, c_sample, cache_fox_k, cache_fox_v, cache_fox_logf, state_ret, page_table, norm1_g, norm2_g, ada_w, ada_b, fox_w_in, fox_b_f, fox_w_out, ret_w_in, ret_w_out, mlp_w_up, mlp_w_down, final_g):
    bp, tp, d = x_prompt.shape
    bs, ts, _ = x_sample.shape
    assert ts == 1, "the decode kernels handle one new token per sequence"
    depth = ada_w.shape[0]
    _, n_pool, page, fox_heads, fox_hd = cache_fox_k.shape
    _, _, ret_heads, key_dim, val_dim = state_ret.shape
    past_len = page_table.shape[1] * page
    srows = -(-bs // BF16_SUBLANES) * BF16_SUBLANES
    tm = min(ROW_TILE, tp)
    tiles_p = tp // tm

    mod_rows = -(-(bs + bp) // 8) * 8
    c_all = _pad_rows(jnp.concatenate([c_sample, c_prompt], axis=0), max(mod_rows, srows))
    mods = _ada(c_all, ada_w, ada_b)

    lg = jnp.log1p(-jnp.exp2(-5.0 - jnp.arange(ret_heads, dtype=F32)))
    cos_p, sin_p = _rotary_tables(jnp.arange(tp), key_dim)
    cos_s, sin_s = _rotary_tables(jnp.full((srows,), past_len), key_dim)

    yp = x_prompt.reshape(bp * tp, d)
    ys = _pad_rows(x_sample.reshape(bs, d), srows)
    fg = final_g.reshape(1, d)
    outs = {name: [] for name in ("kp", "vp", "lfp", "ks", "vs", "lfs", "sp", "ss")}

    for i in range(depth):
        mods_p = mods[i, bs:bs + bp].reshape(bp, 1, 6 * d)
        mods_s = mods[i, :srows].reshape(1, srows, 6 * d)
        g1 = norm1_g[i].reshape(1, d)
        g2 = norm2_g[i].reshape(1, d)
        j = i // 2
        if i % 2 == 0:
            w_qkv = fox_w_in[j, :, :3 * d].astype(BF16)
            w_f = fox_w_in[j, :, 3 * d:].astype(BF16)
            b_f = fox_b_f[j].reshape(1, fox_heads)
            w_out = fox_w_out[j].astype(BF16)
            q, kf, vf, kb, vb, lf = _fox_proj(yp, g1, mods_p, 1, tiles_p, w_qkv, w_f, b_f, tm, fox_hd)
            c = _cumsum(lf, bp, tp)
            c_t = c.reshape(bp, tp, fox_heads).transpose(0, 2, 1).reshape(bp, fox_heads, 1, tp)
            o = _fox_attn(q, kb, vb, c_t, bp, tp, fox_heads, fox_hd)
            yp = _proj_residual(o, w_out, yp, mods_p, 1, tiles_p, 2, tm)
            outs["kp"].append(kf.reshape(bp, tp, fox_heads, fox_hd))
            outs["vp"].append(vf.reshape(bp, tp, fox_heads, fox_hd))
            outs["lfp"].append(lf.reshape(bp, tp, fox_heads))
            q, kf, vf, kb, vb, lf = _fox_proj(ys, g1, mods_s, srows, 1, w_qkv, w_f, b_f, srows, fox_hd)
            o = _fox_decode(
                page_table,
                q[:bs].astype(F32).reshape(bs, 1, d), kb[:bs].astype(F32).reshape(bs, 1, d),
                vb[:bs].astype(F32).reshape(bs, 1, d), lf[:bs].reshape(bs, fox_heads, 1),
                cache_fox_k[j].reshape(n_pool, page, d), cache_fox_v[j].reshape(n_pool, page, d),
                cache_fox_logf[j].transpose(0, 2, 1), fox_heads, fox_hd)
            o = _pad_rows(o.reshape(bs, d), srows).astype(BF16)
            ys = _proj_residual(o, w_out, ys, mods_s, srows, 1, 2, srows)
            outs["ks"].append(kf[:bs].reshape(bs, ts, fox_heads, fox_hd))
            outs["vs"].append(vf[:bs].reshape(bs, ts, fox_heads, fox_hd))
            outs["lfs"].append(lf[:bs].reshape(bs, ts, fox_heads))
        else:
            w_in = ret_w_in[j].astype(BF16)
            w_out = ret_w_out[j].astype(BF16)
            q, k, v, gate = _ret_proj(yp, g1, mods_p, 1, tiles_p, w_in, cos_p, sin_p, tiles_p, key_dim, tm)
            y, s = _ret_prompt(lg, q, k, v, gate, bp, tp, ret_heads, key_dim, val_dim)
            yp = _proj_residual(y, w_out, yp, mods_p, 1, tiles_p, 2, tm)
            outs["sp"].append(s)
            q, k, v, gate = _ret_proj(ys, g1, mods_s, srows, 1, w_in, cos_s, sin_s, 1, key_dim, srows)
            y, s = _ret_decode(lg, q[:bs].astype(F32), k[:bs], v[:bs].astype(F32), gate[:bs],
                               state_ret[j].astype(F32))
            ys = _proj_residual(_pad_rows(y, srows).astype(BF16), w_out, ys, mods_s, srows, 1, 2, srows)
            outs["ss"].append(s)
        last = i == depth - 1
        w_up = mlp_w_up[i].astype(BF16)
        w_down = mlp_w_down[i].astype(BF16)
        yp = _mlp(yp, g2, mods_p, 1, tiles_p, w_up, w_down, fg, last, tm)
        ys = _mlp(ys, g2, mods_s, srows, 1, w_up, w_down, fg, last, srows)

    return (yp.reshape(bp, tp, d), ys[:bs].reshape(bs, ts, d),
            jnp.stack(outs["kp"]), jnp.stack(outs["vp"]), jnp.stack(outs["lfp"]),
            jnp.stack(outs["ks"]), jnp.stack(outs["vs"]), jnp.stack(outs["lfs"]),
            jnp.stack(outs["sp"]), jnp.stack(outs["ss"]))
```

```python
import functools

import jax
import jax.numpy as jnp
from jax import lax
from jax.experimental import pallas as pl
from jax.experimental.pallas import tpu as pltpu

F32 = jnp.float32
BF16 = jnp.bfloat16

EPS = 1e-6
ROPE_BASE = 10000.0
LOG2E = 1.4426950408889634
NEG_BIG = -1e30

V7X_VMEM_LIMIT_BYTES = 60000 * 1024
BF16_SUBLANES = 16

ROW_TILE = 1024
COL_TILE = 512
ATTN_Q_TILE = 1024
ATTN_K_TILE = 1024
ATTN_K_SUB = 1024
ATTN_HEADS_PER_STEP = 2
CUMSUM_TILE = 512
RET_CHUNK = 256
PAGES_PER_STEP = 4


def _params(semantics, vmem_bytes):
    return pltpu.CompilerParams(dimension_semantics=semantics,
                                vmem_limit_bytes=int(min(vmem_bytes, V7X_VMEM_LIMIT_BYTES)))


def _nbytes(shape, dtype):
    n = 1
    for s in shape:
        n *= s
    return n * jnp.dtype(dtype).itemsize


def _norm_mod(x, g, shift, scale):
    y = x * lax.rsqrt(jnp.mean(x * x, axis=-1, keepdims=True) + EPS)
    return (y * g) * (1.0 + scale) + shift


def _silu(x):
    return x * jax.nn.sigmoid(x)


def _log_sigmoid(x):
    return jnp.minimum(x, 0.0) - jnp.log1p(jnp.exp(-jnp.abs(x)))


def _split3(x):
    hi = x.astype(BF16)
    r1 = x - hi.astype(F32)
    mid = r1.astype(BF16)
    lo = (r1 - mid.astype(F32)).astype(BF16)
    return hi, mid, lo


def _mod_spec(rows, width, chunk, tiles_per_group):
    return pl.BlockSpec((None, rows, width), lambda m, n: (m // tiles_per_group, 0, chunk))


def _ada_kernel(c_ref, w_ref, b_ref, o_ref):
    o_ref[...] = jnp.dot(_silu(c_ref[...]), w_ref[...], preferred_element_type=F32) + b_ref[...]


def _ada(c_all, ada_w, ada_b):
    depth, d, n6 = ada_w.shape
    rows = c_all.shape[0]
    tn = min(1024, n6)
    vmem = 2 * (_nbytes((d, tn), F32) + _nbytes((rows, tn), F32)) + _nbytes((rows, d), F32) * 2 + (4 << 20)
    return pl.pallas_call(
        _ada_kernel,
        out_shape=jax.ShapeDtypeStruct((depth, rows, n6), F32),
        grid=(depth, n6 // tn),
        in_specs=[pl.BlockSpec((rows, d), lambda i, n: (0, 0)),
                  pl.BlockSpec((None, d, tn), lambda i, n: (i, 0, n)),
                  pl.BlockSpec((None, 1, tn), lambda i, n: (i, 0, n))],
        out_specs=pl.BlockSpec((None, rows, tn), lambda i, n: (i, 0, n)),
        compiler_params=_params(("arbitrary", "arbitrary"), vmem),
        name="ada_modulation",
    )(c_all, ada_w, ada_b.reshape(depth, 1, n6))


def _fox_proj_kernel(x_ref, g_ref, sh_ref, sc_ref, w_ref, wf_ref, bf_ref,
                     q_ref, kf_ref, vf_ref, kb_ref, vb_ref, lf_ref, h_ref, *, nq, q_scale):
    n = pl.program_id(1)

    @pl.when(n == 0)
    def _():
        h = _norm_mod(x_ref[...], g_ref[...], sh_ref[...], sc_ref[...]).astype(BF16)
        h_ref[...] = h
        fl = jnp.dot(h, wf_ref[...], preferred_element_type=F32) + bf_ref[...]
        lf_ref[...] = _log_sigmoid(fl)

    z = jnp.dot(h_ref[...], w_ref[...], preferred_element_type=F32)

    @pl.when(n < nq)
    def _():
        q_ref[...] = (z * q_scale).astype(BF16)

    @pl.when(jnp.logical_and(n >= nq, n < 2 * nq))
    def _():
        kf_ref[...] = z
        kb_ref[...] = z.astype(BF16)

    @pl.when(n >= 2 * nq)
    def _():
        vf_ref[...] = z
        vb_ref[...] = z.astype(BF16)


def _fox_proj(x, g, mods, mod_rows, tiles_per_group, w_qkv, w_f, b_f, tm, q_scale):
    m_rows, d = x.shape
    heads = w_f.shape[1]
    tn = min(COL_TILE, d)
    nq = d // tn
    kernel = functools.partial(_fox_proj_kernel, nq=nq, q_scale=q_scale)

    def col(lo):
        return lambda m, n: (m, jnp.clip(n - lo, 0, nq - 1))

    vmem = (2 * _nbytes((tm, d), F32) + _nbytes((tm, d), BF16) + 2 * _nbytes((d, tn), BF16)
            + 2 * (3 * _nbytes((tm, tn), BF16) + 2 * _nbytes((tm, tn), F32))
            + 3 * _nbytes((tm, tn), F32) + (4 << 20))
    out_f = jax.ShapeDtypeStruct((m_rows, d), F32)
    out_b = jax.ShapeDtypeStruct((m_rows, d), BF16)
    return pl.pallas_call(
        kernel,
        out_shape=(out_b, out_f, out_f, out_b, out_b, jax.ShapeDtypeStruct((m_rows, heads), F32)),
        grid=(m_rows // tm, 3 * nq),
        in_specs=[pl.BlockSpec((tm, d), lambda m, n: (m, 0)),
                  pl.BlockSpec((1, d), lambda m, n: (0, 0)),
                  _mod_spec(mod_rows, d, 0, tiles_per_group),
                  _mod_spec(mod_rows, d, 1, tiles_per_group),
                  pl.BlockSpec((d, tn), lambda m, n: (0, n)),
                  pl.BlockSpec((d, heads), lambda m, n: (0, 0)),
                  pl.BlockSpec((1, heads), lambda m, n: (0, 0))],
        out_specs=(pl.BlockSpec((tm, tn), col(0)),
                   pl.BlockSpec((tm, tn), col(nq)),
                   pl.BlockSpec((tm, tn), col(2 * nq)),
                   pl.BlockSpec((tm, tn), col(nq)),
                   pl.BlockSpec((tm, tn), col(2 * nq)),
                   pl.BlockSpec((tm, heads), lambda m, n: (m, 0))),
        scratch_shapes=[pltpu.VMEM((tm, d), BF16)],
        compiler_params=_params(("arbitrary", "arbitrary"), vmem),
        name="fox_in_proj",
    )(x, g, mods, mods, w_qkv, w_f, b_f)


def _cumsum_kernel(lf_ref, c_ref, carry_ref):
    @pl.when(pl.program_id(1) == 0)
    def _():
        carry_ref[...] = jnp.zeros_like(carry_ref)

    tc = lf_ref.shape[0]
    row = lax.broadcasted_iota(jnp.int32, (tc, tc), 0)
    col = lax.broadcasted_iota(jnp.int32, (tc, tc), 1)
    tri = jnp.where(col <= row, 1.0, 0.0).astype(BF16)
    c = carry_ref[...]
    for piece in _split3(lf_ref[...]):
        c = c + jnp.dot(tri, piece, preferred_element_type=F32)
    c_ref[...] = c
    carry_ref[...] = c[tc - 1:tc, :]


def _cumsum(logf, batch, seq):
    rows, heads = logf.shape
    tc = min(CUMSUM_TILE, seq)
    per = seq // tc
    return pl.pallas_call(
        _cumsum_kernel,
        out_shape=jax.ShapeDtypeStruct((rows, heads), F32),
        grid=(batch, per),
        in_specs=[pl.BlockSpec((tc, heads), lambda b, t: (b * per + t, 0))],
        out_specs=pl.BlockSpec((tc, heads), lambda b, t: (b * per + t, 0)),
        scratch_shapes=[pltpu.VMEM((1, heads), F32)],
        compiler_params=_params(("arbitrary", "arbitrary"), 16 << 20),
        name="fox_logf_cumsum",
    )(logf)


def _fox_attn_kernel(q_ref, k_ref, v_ref, c_ref, o_ref, vt_ref, *, tq, tk, ts, hd, heads_per_step):
    i = pl.program_id(2)
    seq = k_ref.shape[0]
    q0 = pl.multiple_of(i * tq, tq)

    @pl.when(i == 0)
    def _():
        for t in range(seq // tk):
            vt_ref[:, t * tk:(t + 1) * tk] = v_ref[t * tk:(t + 1) * tk, :].astype(F32).T.astype(BF16)

    def step(j, carry, masked):
        carry = list(carry)
        for sub in range(tk // ts):
            k0 = pl.multiple_of(j * tk + sub * ts, ts)
            if masked:
                keep = (k0 + lax.broadcasted_iota(jnp.int32, (ts, tq), 0)
                        <= q0 + lax.broadcasted_iota(jnp.int32, (ts, tq), 1))
            for hh in range(heads_per_step):
                m, l, acc = carry[3 * hh:3 * hh + 3]
                lanes = slice(hh * hd, (hh + 1) * hd)
                kb = k_ref[pl.ds(k0, ts), lanes]
                s = lax.dot_general(kb, q_ref[:, lanes], (((1,), (1,)), ((), ())),
                                    preferred_element_type=F32)
                cb = c_ref[hh, pl.ds(k0, ts), :]
                s = s - jnp.concatenate([cb] * (tq // cb.shape[1]), axis=1)
                if masked:
                    s = jnp.where(keep, s, NEG_BIG)
                m_new = jnp.maximum(m, jnp.max(s, axis=0, keepdims=True))
                alpha = jnp.exp2(m - m_new)
                p = jnp.exp2(s - m_new)
                l = alpha * l + jnp.sum(p, axis=0, keepdims=True)
                vt = vt_ref[lanes, pl.ds(k0, ts)]
                acc = alpha * acc + jnp.dot(vt, p.astype(BF16), preferred_element_type=F32)
                carry[3 * hh:3 * hh + 3] = [m_new, l, acc]
        return tuple(carry)

    init = (jnp.full((1, tq), NEG_BIG, F32), jnp.zeros((1, tq), F32),
            jnp.zeros((hd, tq), F32)) * heads_per_step
    n_full = q0 // tk
    carry = lax.fori_loop(0, n_full, lambda j, c: step(j, c, False), init)
    carry = step(n_full, carry, True)
    for hh in range(heads_per_step):
        _, l, acc = carry[3 * hh:3 * hh + 3]
        o_ref[:, hh * hd:(hh + 1) * hd] = (acc / l).T.astype(o_ref.dtype)


def _fox_attn(q, k, v, c_rep, batch, seq, heads, head_dim):
    tq = min(ATTN_Q_TILE, seq)
    tk = min(ATTN_K_TILE, seq)
    nq = seq // tq
    hps = ATTN_HEADS_PER_STEP if heads % ATTN_HEADS_PER_STEP == 0 else 1
    width = hps * head_dim
    kernel = functools.partial(_fox_attn_kernel, tq=tq, tk=tk, ts=min(ATTN_K_SUB, tk), hd=head_dim,
                               heads_per_step=hps)
    vmem = (5 * _nbytes((seq, width), BF16) + 2 * _nbytes((hps, seq, 128), F32)
            + 6 * hps * _nbytes((tk, tq), F32) + (8 << 20))
    return pl.pallas_call(
        kernel,
        out_shape=jax.ShapeDtypeStruct(q.shape, BF16),
        grid=(batch, heads // hps, nq),
        in_specs=[pl.BlockSpec((tq, width), lambda b, h, i: (b * nq + i, h)),
                  pl.BlockSpec((seq, width), lambda b, h, i: (b, h)),
                  pl.BlockSpec((seq, width), lambda b, h, i: (b, h)),
                  pl.BlockSpec((None, hps, seq, 128), lambda b, h, i: (b, h, 0, 0))],
        out_specs=pl.BlockSpec((tq, width), lambda b, h, i: (b * nq + i, h)),
        scratch_shapes=[pltpu.VMEM((width, seq), BF16)],
        compiler_params=_params(("arbitrary", "arbitrary", "arbitrary"), vmem),
        name="fox_prompt_attention",
    )(q, k, v, c_rep)


def _fox_decode_kernel(pt_ref, q_ref, kn_ref, vn_ref, cn_ref, *refs, pages):
    k_refs = refs[:pages]
    v_refs = refs[pages:2 * pages]
    lf_refs = refs[2 * pages:3 * pages]
    o_ref, m_ref, l_ref, acc_ref, run_ref = refs[3 * pages:]
    s_idx = pl.program_id(1)
    heads = q_ref.shape[0]
    cols = k_refs[0].shape[0]
    head_mask = (lax.broadcasted_iota(jnp.int32, (heads, cols), 1) % heads
                 == lax.broadcasted_iota(jnp.int32, (heads, cols), 0))
    col = lax.broadcasted_iota(jnp.int32, (1, cols), 1)

    @pl.when(s_idx == 0)
    def _():
        m_ref[...] = jnp.full_like(m_ref, NEG_BIG)
        l_ref[...] = jnp.zeros_like(l_ref)
        acc_ref[...] = jnp.zeros_like(acc_ref)
        run_ref[...] = jnp.zeros_like(run_ref)

    q = q_ref[...]
    cn = cn_ref[...]
    for g in range(pages):
        x = lf_refs[g][...]
        incl = x
        tot = x
        sh = heads
        while sh < cols:
            incl = incl + jnp.where(col + sh < cols, pltpu.roll(incl, cols - sh, 1), 0.0)
            tot = tot + pltpu.roll(tot, sh, 1)
            sh *= 2
        bias = (incl - x) + (run_ref[...] + cn)
        k2 = k_refs[g][...].astype(BF16)
        v2 = v_refs[g][...].astype(BF16)
        s = lax.dot_general(q, k2, (((1,), (1,)), ((), ())), preferred_element_type=F32)
        s = jnp.where(head_mask, s + bias, NEG_BIG)
        m = m_ref[...]
        m_new = jnp.maximum(m, jnp.max(s, axis=-1, keepdims=True))
        alpha = jnp.exp(m - m_new)
        p = jnp.exp(s - m_new)
        l_ref[...] = alpha * l_ref[...] + jnp.sum(p, axis=-1, keepdims=True)
        acc_ref[...] = alpha * acc_ref[...] + jnp.dot(p.astype(BF16), v2, preferred_element_type=F32)
        m_ref[...] = m_new
        run_ref[...] = run_ref[...] + tot

    @pl.when(s_idx == pl.num_programs(1) - 1)
    def _():
        s_new = jnp.sum(q.astype(F32) * kn_ref[...], axis=-1, keepdims=True)
        m = m_ref[...]
        m_new = jnp.maximum(m, s_new)
        alpha = jnp.exp(m - m_new)
        p_new = jnp.exp(s_new - m_new)
        l = alpha * l_ref[...] + p_new
        acc = alpha * acc_ref[...] + p_new.astype(BF16).astype(F32) * vn_ref[...]
        o_ref[...] = acc / l


def _fox_decode(page_table, pool_base, q, k_new, v_new, logf_new, cache_k, cache_v, cache_lf):
    bs, n_pages = page_table.shape
    _, heads, hd = q.shape
    _, cols, _ = cache_k.shape
    g_pages = PAGES_PER_STEP if n_pages % PAGES_PER_STEP == 0 else 1
    steps = n_pages // g_pages
    kernel = functools.partial(_fox_decode_kernel, pages=g_pages)

    def page_idx(g):
        return lambda b, s, pt: (pool_base + pt[b * n_pages + (n_pages - 1 - (s * g_pages + g))], 0, 0)

    tok = pl.BlockSpec((None, heads, hd), lambda b, s, pt: (b, 0, 0))
    in_specs = [tok, tok, tok, pl.BlockSpec((None, 1, cols), lambda b, s, pt: (b, 0, 0))]
    in_specs += [pl.BlockSpec((None, cols, hd), page_idx(g)) for g in range(g_pages)]
    in_specs += [pl.BlockSpec((None, cols, hd), page_idx(g)) for g in range(g_pages)]
    in_specs += [pl.BlockSpec((None, 1, cols), page_idx(g)) for g in range(g_pages)]
    vmem = (4 * g_pages * _nbytes((cols, hd), F32) + 2 * g_pages * _nbytes((cols, hd), BF16)
            + 8 * _nbytes((heads, cols), F32) + (4 << 20))
    grid_spec = pltpu.PrefetchScalarGridSpec(
        num_scalar_prefetch=1,
        grid=(bs, steps),
        in_specs=in_specs,
        out_specs=pl.BlockSpec((None, heads, hd), lambda b, s, pt: (b, 0, 0)),
        scratch_shapes=[pltpu.VMEM((heads, 1), F32), pltpu.VMEM((heads, 1), F32),
                        pltpu.VMEM((heads, hd), F32), pltpu.VMEM((1, cols), F32)])
    return pl.pallas_call(
        kernel,
        out_shape=jax.ShapeDtypeStruct((bs, heads, hd), F32),
        grid_spec=grid_spec,
        compiler_params=_params(("arbitrary", "arbitrary"), vmem),
        name="fox_decode_attention",
    )(page_table.reshape(-1), q, k_new, v_new, logf_new,
      *([cache_k] * g_pages), *([cache_v] * g_pages), *([cache_lf] * g_pages))


def _proj_residual_kernel(a_ref, w_ref, x_ref, gate_ref, o_ref):
    z = jnp.dot(a_ref[...], w_ref[...], preferred_element_type=F32)
    o_ref[...] = x_ref[...] + gate_ref[...] * z


def _proj_residual(a, w, x, mods, mod_rows, tiles_per_group, gate_chunk, tm):
    m_rows, k = a.shape
    d = w.shape[1]
    tn = min(COL_TILE, d)
    vmem = (2 * _nbytes((tm, k), BF16) + 2 * _nbytes((k, tn), BF16) + 5 * _nbytes((tm, tn), F32)
            + (4 << 20))
    return pl.pallas_call(
        _proj_residual_kernel,
        out_shape=jax.ShapeDtypeStruct((m_rows, d), F32),
        grid=(m_rows // tm, d // tn),
        in_specs=[pl.BlockSpec((tm, k), lambda m, n: (m, 0)),
                  pl.BlockSpec((k, tn), lambda m, n: (0, n)),
                  pl.BlockSpec((tm, tn), lambda m, n: (m, n)),
                  pl.BlockSpec((None, mod_rows, tn),
                               lambda m, n: (m // tiles_per_group, 0, gate_chunk * (d // tn) + n))],
        out_specs=pl.BlockSpec((tm, tn), lambda m, n: (m, n)),
        compiler_params=_params(("arbitrary", "arbitrary"), vmem),
        name="mixer_out_proj",
    )(a, w, x, mods)


def _mlp_kernel(x_ref, g_ref, sh_ref, sc_ref, gate_ref, wu_ref, wd_ref, fg_ref, o_ref, h_ref, *,
                final_norm):
    f = pl.program_id(1)

    @pl.when(f == 0)
    def _():
        h_ref[...] = _norm_mod(x_ref[...], g_ref[...], sh_ref[...], sc_ref[...]).astype(BF16)
        o_ref[...] = jnp.zeros_like(o_ref)

    u = jnp.dot(h_ref[...], wu_ref[...], preferred_element_type=F32)
    u = jnp.square(jnp.maximum(u, 0.0)).astype(BF16)
    o_ref[...] += jnp.dot(u, wd_ref[...], preferred_element_type=F32)

    @pl.when(f == pl.num_programs(1) - 1)
    def _():
        y = x_ref[...] + gate_ref[...] * o_ref[...]
        if final_norm:
            y = y * lax.rsqrt(jnp.mean(y * y, axis=-1, keepdims=True) + EPS) * fg_ref[...]
        o_ref[...] = y


def _mlp(x, g, mods, mod_rows, tiles_per_group, w_up, w_down, final_g, final_norm, tm):
    m_rows, d = x.shape
    d_ff = w_up.shape[1]
    tf = min(COL_TILE, d_ff)
    kernel = functools.partial(_mlp_kernel, final_norm=final_norm)
    vmem = (4 * _nbytes((tm, d), F32) + _nbytes((tm, d), BF16) + 4 * _nbytes((d, tf), BF16)
            + 4 * _nbytes((tm, tf), F32) + (4 << 20))
    vec = pl.BlockSpec((1, d), lambda m, f: (0, 0))
    return pl.pallas_call(
        kernel,
        out_shape=jax.ShapeDtypeStruct((m_rows, d), F32),
        grid=(m_rows // tm, d_ff // tf),
        in_specs=[pl.BlockSpec((tm, d), lambda m, f: (m, 0)),
                  vec,
                  _mod_spec(mod_rows, d, 3, tiles_per_group),
                  _mod_spec(mod_rows, d, 4, tiles_per_group),
                  _mod_spec(mod_rows, d, 5, tiles_per_group),
                  pl.BlockSpec((d, tf), lambda m, f: (0, f)),
                  pl.BlockSpec((tf, d), lambda m, f: (f, 0)),
                  vec],
        out_specs=pl.BlockSpec((tm, d), lambda m, f: (m, 0)),
        scratch_shapes=[pltpu.VMEM((tm, d), BF16)],
        compiler_params=_params(("arbitrary", "arbitrary"), vmem),
        name="relu2_mlp",
    )(x, g, mods, mods, mods, w_up, w_down, final_g)


def _ret_proj_kernel(x_ref, g_ref, sh_ref, sc_ref, w_ref, cos_ref, sin_ref,
                     q_ref, k_ref, v_ref, gt_ref, h_ref, *, nq, key_dim, k_scale):
    n = pl.program_id(1)

    @pl.when(n == 0)
    def _():
        h_ref[...] = _norm_mod(x_ref[...], g_ref[...], sh_ref[...], sc_ref[...]).astype(BF16)

    z = jnp.dot(h_ref[...], w_ref[...], preferred_element_type=F32)
    half = key_dim // 2

    def rotate(zz):
        cos = cos_ref[...]
        sin = sin_ref[...]
        parts = []
        for hh in range(zz.shape[1] // key_dim):
            x1 = zz[:, hh * key_dim:hh * key_dim + half]
            x2 = zz[:, hh * key_dim + half:(hh + 1) * key_dim]
            parts += [x1 * cos - x2 * sin, x1 * sin + x2 * cos]
        return jnp.concatenate(parts, axis=1)

    @pl.when(n < nq)
    def _():
        q_ref[...] = rotate(z).astype(BF16)

    @pl.when(jnp.logical_and(n >= nq, n < 2 * nq))
    def _():
        k_ref[...] = rotate(z) * k_scale

    @pl.when(jnp.logical_and(n >= 2 * nq, n < 4 * nq))
    def _():
        v_ref[...] = z.astype(BF16)

    @pl.when(n >= 4 * nq)
    def _():
        gt_ref[...] = z


def _ret_proj(x, g, mods, mod_rows, tiles_per_group, w, cos, sin, pos_tiles, key_dim, tm):
    m_rows, d = x.shape
    tn = min(COL_TILE, d)
    nq = d // tn
    half = key_dim // 2
    kernel = functools.partial(_ret_proj_kernel, nq=nq, key_dim=key_dim, k_scale=key_dim ** -0.5)

    def col(lo, width):
        return lambda m, n: (m, jnp.clip(n - lo, 0, width - 1))

    vmem = (2 * _nbytes((tm, d), F32) + _nbytes((tm, d), BF16) + 2 * _nbytes((d, tn), BF16)
            + 2 * (2 * _nbytes((tm, tn), BF16) + 2 * _nbytes((tm, tn), F32))
            + 4 * _nbytes((tm, half), F32) + 4 * _nbytes((tm, tn), F32) + (4 << 20))
    return pl.pallas_call(
        kernel,
        out_shape=(jax.ShapeDtypeStruct((m_rows, d), BF16), jax.ShapeDtypeStruct((m_rows, d), F32),
                   jax.ShapeDtypeStruct((m_rows, 2 * d), BF16), jax.ShapeDtypeStruct((m_rows, 2 * d), F32)),
        grid=(m_rows // tm, 6 * nq),
        in_specs=[pl.BlockSpec((tm, d), lambda m, n: (m, 0)),
                  pl.BlockSpec((1, d), lambda m, n: (0, 0)),
                  _mod_spec(mod_rows, d, 0, tiles_per_group),
                  _mod_spec(mod_rows, d, 1, tiles_per_group),
                  pl.BlockSpec((d, tn), lambda m, n: (0, n)),
                  pl.BlockSpec((tm, half), lambda m, n: (m % pos_tiles, 0)),
                  pl.BlockSpec((tm, half), lambda m, n: (m % pos_tiles, 0))],
        out_specs=(pl.BlockSpec((tm, tn), col(0, nq)),
                   pl.BlockSpec((tm, tn), col(nq, nq)),
                   pl.BlockSpec((tm, tn), col(2 * nq, 2 * nq)),
                   pl.BlockSpec((tm, tn), col(4 * nq, 2 * nq))),
        scratch_shapes=[pltpu.VMEM((tm, d), BF16)],
        compiler_params=_params(("arbitrary", "arbitrary"), vmem),
        name="ret_in_proj",
    )(x, g, mods, mods, w, cos, sin)


def _group_norm_gate(o, gate):
    mu = jnp.mean(o, axis=-1, keepdims=True)
    cen = o - mu
    var = jnp.mean(cen * cen, axis=-1, keepdims=True)
    return _silu(gate) * (cen * lax.rsqrt(var + EPS))


def _ret_chunk_kernel(lg_ref, q_ref, k_ref, v_ref, gt_ref, y_ref, s_ref):
    lg = jnp.full((1, 1), lg_ref[pl.program_id(1)], F32)

    @pl.when(pl.program_id(2) == 0)
    def _():
        s_ref[...] = jnp.zeros_like(s_ref)

    q = q_ref[...]
    k = k_ref[...]
    v = v_ref[...]
    ln = q.shape[0]
    diff = (lax.broadcasted_iota(jnp.int32, (ln, ln), 0)
            - lax.broadcasted_iota(jnp.int32, (ln, ln), 1)).astype(F32)
    dec = jnp.where(diff >= 0, jnp.exp(lg * jnp.maximum(diff, 0.0)), 0.0)
    n = lax.broadcasted_iota(jnp.int32, (ln, 1), 0).astype(F32)
    xi = jnp.exp(lg * (n + 1.0))
    zeta = jnp.exp(lg * (ln - 1.0 - n))

    s0 = s_ref[...]
    qk = lax.dot_general(q, k.astype(BF16), (((1,), (1,)), ((), ())), preferred_element_type=F32)
    a = (qk * dec).astype(BF16)
    o = (jnp.dot(a, v, preferred_element_type=F32)
         + jnp.dot(q, s0.astype(BF16), preferred_element_type=F32) * xi)
    kz = (k * zeta).astype(BF16)
    s_ref[...] = jnp.exp(lg * ln) * s0 + lax.dot_general(
        kz, v, (((0,), (0,)), ((), ())), preferred_element_type=F32)
    y_ref[...] = _group_norm_gate(o, gt_ref[...]).astype(y_ref.dtype)


def _ret_prompt(lg, q, k, v, gate, batch, seq, heads, key_dim, val_dim):
    ln = min(RET_CHUNK, seq)
    nc = seq // ln
    vmem = (2 * (_nbytes((ln, key_dim), BF16) + _nbytes((ln, key_dim), F32) + _nbytes((ln, val_dim), BF16)
                 + _nbytes((ln, val_dim), F32) + _nbytes((ln, val_dim), BF16))
            + 4 * _nbytes((key_dim, val_dim), F32) + 8 * _nbytes((ln, val_dim), F32) + (8 << 20))
    grid_spec = pltpu.PrefetchScalarGridSpec(
        num_scalar_prefetch=1,
        grid=(batch, heads, nc),
        in_specs=[pl.BlockSpec((ln, key_dim), lambda b, h, c, lg: (b * nc + c, h)),
                  pl.BlockSpec((ln, key_dim), lambda b, h, c, lg: (b * nc + c, h)),
                  pl.BlockSpec((ln, val_dim), lambda b, h, c, lg: (b * nc + c, h)),
                  pl.BlockSpec((ln, val_dim), lambda b, h, c, lg: (b * nc + c, h))],
        out_specs=(pl.BlockSpec((ln, val_dim), lambda b, h, c, lg: (b * nc + c, h)),
                   pl.BlockSpec((None, None, key_dim, val_dim), lambda b, h, c, lg: (b, h, 0, 0))))
    return pl.pallas_call(
        _ret_chunk_kernel,
        out_shape=(jax.ShapeDtypeStruct((batch * seq, heads * val_dim), BF16),
                   jax.ShapeDtypeStruct((batch, heads, key_dim, val_dim), F32)),
        grid_spec=grid_spec,
        compiler_params=_params(("arbitrary", "arbitrary", "arbitrary"), vmem),
        name="ret_prompt_chunks",
    )(lg, q, k, v, gate)


def _ret_decode_kernel(lg_ref, qc_ref, kc_ref, qr_ref, kr_ref, v_ref, gt_ref, s0_ref, y_ref, s_ref):
    gamma = jnp.exp(jnp.full((1, 1), lg_ref[pl.program_id(1)], F32))
    s0 = s0_ref[...]
    v = v_ref[...]
    a = jnp.sum(qr_ref[...] * kr_ref[...], axis=-1, keepdims=True)
    qs = jnp.sum(qc_ref[...] * s0, axis=0, keepdims=True)
    o = a * v + qs * gamma
    s_ref[...] = gamma * s0 + kc_ref[...] * v
    y_ref[...] = _group_norm_gate(o, gt_ref[...])


def _ret_decode(lg, q, k, v, gate, state):
    bs, heads, key_dim, val_dim = state.shape
    q4 = q.reshape(bs, heads, key_dim, 1)
    k4 = k.reshape(bs, heads, key_dim, 1)
    qr = q.reshape(bs, heads, 1, key_dim)
    kr = k.reshape(bs, heads, 1, key_dim)
    v4 = v.reshape(bs, heads, 1, val_dim)
    g4 = gate.reshape(bs, heads, 1, val_dim)

    def spec(r, c):
        return pl.BlockSpec((None, None, r, c), lambda b, h, lg: (b, h, 0, 0))

    grid_spec = pltpu.PrefetchScalarGridSpec(
        num_scalar_prefetch=1,
        grid=(bs, heads),
        in_specs=[spec(key_dim, 1), spec(key_dim, 1), spec(1, key_dim), spec(1, key_dim),
                  spec(1, val_dim), spec(1, val_dim), spec(key_dim, val_dim)],
        out_specs=(spec(1, val_dim), spec(key_dim, val_dim)))
    y, s = pl.pallas_call(
        _ret_decode_kernel,
        out_shape=(jax.ShapeDtypeStruct((bs, heads, 1, val_dim), F32),
                   jax.ShapeDtypeStruct(state.shape, F32)),
        grid_spec=grid_spec,
        compiler_params=_params(("arbitrary", "arbitrary"), 24 << 20),
        name="ret_decode_step",
    )(lg, q4, k4, qr, kr, v4, g4, state)
    return y.reshape(bs, heads * val_dim), s


def _rotary_tables(pos, key_dim):
    half = key_dim // 2
    inv = 1.0 / (ROPE_BASE ** jnp.linspace(0.0, 1.0, half, dtype=F32))
    ang = pos.astype(F32)[:, None] * inv[None, :]
    return jnp.cos(ang), jnp.sin(ang)


def _pad_rows(a, rows):
    return jnp.pad(a, ((0, rows - a.shape[0]), (0, 0)))


def kernel(x_prompt, x_sample, c_prompt, c_sample, cache_fox_k, cache_fox_v, cache_fox_logf, state_ret, page_table, norm1_g, norm2_g, ada_w, ada_b, fox_w_in, fox_b_f, fox_w_out, ret_w_in, ret_w_out, mlp_w_up, mlp_w_down, final_g):
    bp, tp, d = x_prompt.shape
    bs, ts, _ = x_sample.shape
    assert ts == 1, "the decode kernels handle one new token per sequence"
    depth = ada_w.shape[0]
    _, n_pool, page, fox_heads, fox_hd = cache_fox_k.shape
    _, _, ret_heads, key_dim, val_dim = state_ret.shape
    past_len = page_table.shape[1] * page
    srows = -(-bs // BF16_SUBLANES) * BF16_SUBLANES
    tm = min(ROW_TILE, tp)
    tiles_p = tp // tm

    mod_rows = -(-(bs + bp) // 8) * 8
    c_all = _pad_rows(jnp.concatenate([c_sample, c_prompt], axis=0), max(mod_rows, srows))
    mods = _ada(c_all, ada_w, ada_b)

    lg = jnp.log1p(-jnp.exp2(-5.0 - jnp.arange(ret_heads, dtype=F32)))
    cos_p, sin_p = _rotary_tables(jnp.arange(tp), key_dim)
    cos_s, sin_s = _rotary_tables(jnp.full((srows,), past_len), key_dim)

    yp = x_prompt.reshape(bp * tp, d)
    ys = _pad_rows(x_sample.reshape(bs, d), srows)
    fg = final_g.reshape(1, d)
    outs = {name: [] for name in ("kp", "vp", "lfp", "ks", "vs", "lfs", "sp", "ss")}

    for i in range(depth):
        mods_p = mods[i, bs:bs + bp].reshape(bp, 1, 6 * d)
        mods_s = mods[i, :srows].reshape(1, srows, 6 * d)
        g1 = norm1_g[i].reshape(1, d)
        g2 = norm2_g[i].reshape(1, d)
        j = i // 2
        if i % 2 == 0:
            w_qkv = fox_w_in[j, :, :3 * d].astype(BF16)
            w_f = fox_w_in[j, :, 3 * d:].astype(BF16)
            b_f = fox_b_f[j].reshape(1, fox_heads)
            w_out = fox_w_out[j].astype(BF16)
            q, kf, vf, kb, vb, lf = _fox_proj(yp, g1, mods_p, 1, tiles_p, w_qkv, w_f, b_f, tm,
                                              fox_hd ** -0.5 * LOG2E)
            c = _cumsum(lf, bp, tp) * LOG2E
            c_rep = jnp.broadcast_to(c.reshape(bp, tp, fox_heads).transpose(0, 2, 1)[..., None],
                                     (bp, fox_heads, tp, 128))
            o = _fox_attn(q, kb, vb, c_rep, bp, tp, fox_heads, fox_hd)
            yp = _proj_residual(o, w_out, yp, mods_p, 1, tiles_p, 2, tm)
            outs["kp"].append(kf.reshape(bp, tp, fox_heads, fox_hd))
            outs["vp"].append(vf.reshape(bp, tp, fox_heads, fox_hd))
            outs["lfp"].append(lf.reshape(bp, tp, fox_heads))
            q, kf, vf, kb, vb, lf = _fox_proj(ys, g1, mods_s, srows, 1, w_qkv, w_f, b_f, srows,
                                              fox_hd ** -0.5)
            o = _fox_decode(
                page_table, j * n_pool,
                q[:bs].reshape(bs, fox_heads, fox_hd), kb[:bs].astype(F32).reshape(bs, fox_heads, fox_hd),
                vb[:bs].astype(F32).reshape(bs, fox_heads, fox_hd),
                jnp.tile(lf[:bs], (1, page)).reshape(bs, 1, page * fox_heads),
                cache_fox_k.reshape(-1, page * fox_heads, fox_hd),
                cache_fox_v.reshape(-1, page * fox_heads, fox_hd),
                cache_fox_logf.reshape(-1, 1, page * fox_heads))
            o = _pad_rows(o.reshape(bs, d), srows).astype(BF16)
            ys = _proj_residual(o, w_out, ys, mods_s, srows, 1, 2, srows)
            outs["ks"].append(kf[:bs].reshape(bs, ts, fox_heads, fox_hd))
            outs["vs"].append(vf[:bs].reshape(bs, ts, fox_heads, fox_hd))
            outs["lfs"].append(lf[:bs].reshape(bs, ts, fox_heads))
        else:
            w_in = ret_w_in[j].astype(BF16)
            w_out = ret_w_out[j].astype(BF16)
            q, k, v, gate = _ret_proj(yp, g1, mods_p, 1, tiles_p, w_in, cos_p, sin_p, tiles_p, key_dim, tm)
            y, s = _ret_prompt(lg, q, k, v, gate, bp, tp, ret_heads, key_dim, val_dim)
            yp = _proj_residual(y, w_out, yp, mods_p, 1, tiles_p, 2, tm)
            outs["sp"].append(s)
            q, k, v, gate = _ret_proj(ys, g1, mods_s, srows, 1, w_in, cos_s, sin_s, 1, key_dim, srows)
            y, s = _ret_decode(lg, q[:bs].astype(F32), k[:bs], v[:bs].astype(F32), gate[:bs],
                               state_ret[j].astype(F32))
            ys = _proj_residual(_pad_rows(y, srows).astype(BF16), w_out, ys, mods_s, srows, 1, 2, srows)
            outs["ss"].append(s)
        last = i == depth - 1
        w_up = mlp_w_up[i].astype(BF16)
        w_down = mlp_w_down[i].astype(BF16)
        yp = _mlp(yp, g2, mods_p, 1, tiles_p, w_up, w_down, fg, last, tm)
        ys = _mlp(ys, g2, mods_s, srows, 1, w_up, w_down, fg, last, srows)

    return (yp.reshape(bp, tp, d), ys[:bs].reshape(bs, ts, d),
            jnp.stack(outs["kp"]), jnp.stack(outs["vp"]), jnp.stack(outs["lfp"]),
            jnp.stack(outs["ks"]), jnp.stack(outs["vs"]), jnp.stack(outs["lfs"]),
            jnp.stack(outs["sp"]), jnp.stack(outs["ss"]))
```

```python
import functools

import jax
import jax.numpy as jnp
from jax import lax
from jax.experimental import pallas as pl
from jax.experimental.pallas import tpu as pltpu

F32 = jnp.float32
BF16 = jnp.bfloat16

EPS = 1e-6
ROPE_BASE = 10000.0
LOG2E = 1.4426950408889634
NEG_BIG = -1e30

V7X_VMEM_LIMIT_BYTES = 60000 * 1024
BF16_SUBLANES = 16

ROW_TILE = 1024
COL_TILE = 512
ATTN_Q_TILE = 1024
ATTN_K_TILE = 1024
ATTN_K_SUB = 512
ATTN_HEADS_PER_STEP = 2
CUMSUM_TILE = 512
RET_CHUNK = 512
PAGES_PER_STEP = 8


def _params(semantics, vmem_bytes):
    return pltpu.CompilerParams(dimension_semantics=semantics,
                                vmem_limit_bytes=int(min(vmem_bytes, V7X_VMEM_LIMIT_BYTES)))


def _nbytes(shape, dtype):
    n = 1
    for s in shape:
        n *= s
    return n * jnp.dtype(dtype).itemsize


def _norm_mod(x, g, shift, scale):
    y = x * lax.rsqrt(jnp.mean(x * x, axis=-1, keepdims=True) + EPS)
    return (y * g) * (1.0 + scale) + shift


def _silu(x):
    return x * jax.nn.sigmoid(x)


def _log_sigmoid(x):
    return jnp.minimum(x, 0.0) - jnp.log1p(jnp.exp(-jnp.abs(x)))


def _split3(x):
    hi = x.astype(BF16)
    r1 = x - hi.astype(F32)
    mid = r1.astype(BF16)
    lo = (r1 - mid.astype(F32)).astype(BF16)
    return hi, mid, lo


def _mod_spec(rows, width, chunk, tiles_per_group):
    return pl.BlockSpec((None, rows, width), lambda m, n: (m // tiles_per_group, 0, chunk))


def _ada_kernel(c_ref, w_ref, b_ref, o_ref):
    o_ref[...] = jnp.dot(_silu(c_ref[...]), w_ref[...], preferred_element_type=F32) + b_ref[...]


def _ada(c_all, ada_w, ada_b):
    depth, d, n6 = ada_w.shape
    rows = c_all.shape[0]
    tn = min(1024, n6)
    vmem = 2 * (_nbytes((d, tn), F32) + _nbytes((rows, tn), F32)) + _nbytes((rows, d), F32) * 2 + (4 << 20)
    return pl.pallas_call(
        _ada_kernel,
        out_shape=jax.ShapeDtypeStruct((depth, rows, n6), F32),
        grid=(depth, n6 // tn),
        in_specs=[pl.BlockSpec((rows, d), lambda i, n: (0, 0)),
                  pl.BlockSpec((None, d, tn), lambda i, n: (i, 0, n)),
                  pl.BlockSpec((None, 1, tn), lambda i, n: (i, 0, n))],
        out_specs=pl.BlockSpec((None, rows, tn), lambda i, n: (i, 0, n)),
        compiler_params=_params(("arbitrary", "arbitrary"), vmem),
        name="ada_modulation",
    )(c_all, ada_w, ada_b.reshape(depth, 1, n6))


def _fox_proj_kernel(x_ref, g_ref, sh_ref, sc_ref, w_ref, wf_ref, bf_ref,
                     q_ref, kf_ref, vf_ref, kb_ref, vb_ref, lf_ref, h_ref, *, nq, q_scale):
    n = pl.program_id(1)

    @pl.when(n == 0)
    def _():
        h = _norm_mod(x_ref[...], g_ref[...], sh_ref[...], sc_ref[...]).astype(BF16)
        h_ref[...] = h
        fl = jnp.dot(h, wf_ref[...], preferred_element_type=F32) + bf_ref[...]
        lf_ref[...] = _log_sigmoid(fl)

    z = jnp.dot(h_ref[...], w_ref[...], preferred_element_type=F32)

    @pl.when(n < nq)
    def _():
        q_ref[...] = (z * q_scale).astype(BF16)

    @pl.when(jnp.logical_and(n >= nq, n < 2 * nq))
    def _():
        kf_ref[...] = z
        kb_ref[...] = z.astype(BF16)

    @pl.when(n >= 2 * nq)
    def _():
        vf_ref[...] = z
        vb_ref[...] = z.astype(BF16)


def _fox_proj(x, g, mods, mod_rows, tiles_per_group, w_qkv, w_f, b_f, tm, q_scale):
    m_rows, d = x.shape
    heads = w_f.shape[1]
    tn = min(COL_TILE, d)
    nq = d // tn
    kernel = functools.partial(_fox_proj_kernel, nq=nq, q_scale=q_scale)

    def col(lo):
        return lambda m, n: (m, jnp.clip(n - lo, 0, nq - 1))

    vmem = (2 * _nbytes((tm, d), F32) + _nbytes((tm, d), BF16) + 2 * _nbytes((d, tn), BF16)
            + 2 * (3 * _nbytes((tm, tn), BF16) + 2 * _nbytes((tm, tn), F32))
            + 3 * _nbytes((tm, tn), F32) + (4 << 20))
    out_f = jax.ShapeDtypeStruct((m_rows, d), F32)
    out_b = jax.ShapeDtypeStruct((m_rows, d), BF16)
    return pl.pallas_call(
        kernel,
        out_shape=(out_b, out_f, out_f, out_b, out_b, jax.ShapeDtypeStruct((m_rows, heads), F32)),
        grid=(m_rows // tm, 3 * nq),
        in_specs=[pl.BlockSpec((tm, d), lambda m, n: (m, 0)),
                  pl.BlockSpec((1, d), lambda m, n: (0, 0)),
                  _mod_spec(mod_rows, d, 0, tiles_per_group),
                  _mod_spec(mod_rows, d, 1, tiles_per_group),
                  pl.BlockSpec((d, tn), lambda m, n: (0, n)),
                  pl.BlockSpec((d, heads), lambda m, n: (0, 0)),
                  pl.BlockSpec((1, heads), lambda m, n: (0, 0))],
        out_specs=(pl.BlockSpec((tm, tn), col(0)),
                   pl.BlockSpec((tm, tn), col(nq)),
                   pl.BlockSpec((tm, tn), col(2 * nq)),
                   pl.BlockSpec((tm, tn), col(nq)),
                   pl.BlockSpec((tm, tn), col(2 * nq)),
                   pl.BlockSpec((tm, heads), lambda m, n: (m, 0))),
        scratch_shapes=[pltpu.VMEM((tm, d), BF16)],
        compiler_params=_params(("arbitrary", "arbitrary"), vmem),
        name="fox_in_proj",
    )(x, g, mods, mods, w_qkv, w_f, b_f)


def _cumsum_kernel(lf_ref, c_ref, carry_ref):
    @pl.when(pl.program_id(1) == 0)
    def _():
        carry_ref[...] = jnp.zeros_like(carry_ref)

    tc = lf_ref.shape[0]
    row = lax.broadcasted_iota(jnp.int32, (tc, tc), 0)
    col = lax.broadcasted_iota(jnp.int32, (tc, tc), 1)
    tri = jnp.where(col <= row, 1.0, 0.0).astype(BF16)
    c = carry_ref[...]
    for piece in _split3(lf_ref[...]):
        c = c + jnp.dot(tri, piece, preferred_element_type=F32)
    c_ref[...] = c
    carry_ref[...] = c[tc - 1:tc, :]


def _cumsum(logf, batch, seq):
    rows, heads = logf.shape
    tc = min(CUMSUM_TILE, seq)
    per = seq // tc
    return pl.pallas_call(
        _cumsum_kernel,
        out_shape=jax.ShapeDtypeStruct((rows, heads), F32),
        grid=(batch, per),
        in_specs=[pl.BlockSpec((tc, heads), lambda b, t: (b * per + t, 0))],
        out_specs=pl.BlockSpec((tc, heads), lambda b, t: (b * per + t, 0)),
        scratch_shapes=[pltpu.VMEM((1, heads), F32)],
        compiler_params=_params(("arbitrary", "arbitrary"), 16 << 20),
        name="fox_logf_cumsum",
    )(logf)


def _fox_attn_kernel(q_ref, k_ref, v_ref, c_ref, o_ref, vt_ref, *, tq, tk, ts, hd, heads_per_step):
    i = pl.program_id(2)
    seq = k_ref.shape[0]
    q0 = pl.multiple_of(i * tq, tq)

    @pl.when(i == 0)
    def _():
        for t in range(seq // tk):
            vt_ref[:, t * tk:(t + 1) * tk] = v_ref[t * tk:(t + 1) * tk, :].astype(F32).T.astype(BF16)

    def step(j, carry, masked):
        carry = list(carry)
        items = [(sub, hh) for sub in range(tk // ts) for hh in range(heads_per_step)]

        def scores(sub, hh):
            k0 = pl.multiple_of(j * tk + sub * ts, ts)
            lanes = slice(hh * hd, (hh + 1) * hd)
            s = lax.dot_general(k_ref[pl.ds(k0, ts), lanes], q_ref[:, lanes], (((1,), (1,)), ((), ())),
                                preferred_element_type=F32)
            cb = c_ref[hh, pl.ds(k0, ts), :]
            s = s - jnp.concatenate([cb] * (tq // cb.shape[1]), axis=1)
            if masked:
                keep = (k0 + lax.broadcasted_iota(jnp.int32, (ts, tq), 0)
                        <= q0 + lax.broadcasted_iota(jnp.int32, (ts, tq), 1))
                s = jnp.where(keep, s, NEG_BIG)
            return s

        s_next = scores(*items[0])
        for n, (sub, hh) in enumerate(items):
            s = s_next
            if n + 1 < len(items):
                s_next = scores(*items[n + 1])
            m, l, acc = carry[3 * hh:3 * hh + 3]
            m_new = jnp.maximum(m, jnp.max(s, axis=0, keepdims=True))
            alpha = jnp.exp2(m - m_new)
            p = jnp.exp2(s - m_new)
            l = alpha * l + jnp.sum(p, axis=0, keepdims=True)
            k0 = pl.multiple_of(j * tk + sub * ts, ts)
            vt = vt_ref[hh * hd:(hh + 1) * hd, pl.ds(k0, ts)]
            acc = alpha * acc + jnp.dot(vt, p.astype(BF16), preferred_element_type=F32)
            carry[3 * hh:3 * hh + 3] = [m_new, l, acc]
        return tuple(carry)

    init = (jnp.full((1, tq), NEG_BIG, F32), jnp.zeros((1, tq), F32),
            jnp.zeros((hd, tq), F32)) * heads_per_step
    n_full = q0 // tk
    carry = lax.fori_loop(0, n_full, lambda j, c: step(j, c, False), init)
    carry = step(n_full, carry, True)
    for hh in range(heads_per_step):
        _, l, acc = carry[3 * hh:3 * hh + 3]
        o_ref[:, hh * hd:(hh + 1) * hd] = (acc / l).T.astype(o_ref.dtype)


def _fox_attn(q, k, v, c_rep, batch, seq, heads, head_dim):
    tq = min(ATTN_Q_TILE, seq)
    tk = min(ATTN_K_TILE, seq)
    nq = seq // tq
    hps = ATTN_HEADS_PER_STEP if heads % ATTN_HEADS_PER_STEP == 0 else 1
    width = hps * head_dim
    kernel = functools.partial(_fox_attn_kernel, tq=tq, tk=tk, ts=min(ATTN_K_SUB, tk), hd=head_dim,
                               heads_per_step=hps)
    vmem = (5 * _nbytes((seq, width), BF16) + 2 * _nbytes((hps, seq, 128), F32)
            + 6 * hps * _nbytes((tk, tq), F32) + (8 << 20))
    return pl.pallas_call(
        kernel,
        out_shape=jax.ShapeDtypeStruct(q.shape, BF16),
        grid=(batch, heads // hps, nq),
        in_specs=[pl.BlockSpec((tq, width), lambda b, h, i: (b * nq + i, h)),
                  pl.BlockSpec((seq, width), lambda b, h, i: (b, h)),
                  pl.BlockSpec((seq, width), lambda b, h, i: (b, h)),
                  pl.BlockSpec((None, hps, seq, 128), lambda b, h, i: (b, h, 0, 0))],
        out_specs=pl.BlockSpec((tq, width), lambda b, h, i: (b * nq + i, h)),
        scratch_shapes=[pltpu.VMEM((width, seq), BF16)],
        compiler_params=_params(("arbitrary", "arbitrary", "arbitrary"), vmem),
        name="fox_prompt_attention",
    )(q, k, v, c_rep)


def _page_suffix_kernel(lf_ref, suf_ref, tot_ref, *, heads):
    x = lf_ref[...]
    cols = x.shape[1]
    col = lax.broadcasted_iota(jnp.int32, x.shape, 1)
    incl = x
    tot = x
    sh = heads
    while sh < cols:
        incl = incl + jnp.where(col + sh < cols, pltpu.roll(incl, cols - sh, 1), 0.0)
        tot = tot + pltpu.roll(tot, sh, 1)
        sh *= 2
    suf_ref[...] = incl - x
    tot_ref[...] = tot


def _page_suffix(cache_lf, heads):
    n_rows, cols = cache_lf.shape
    tr = 256 if n_rows % 256 == 0 else n_rows
    shape = jax.ShapeDtypeStruct((n_rows, cols), F32)
    spec = pl.BlockSpec((tr, cols), lambda r: (r, 0))
    return pl.pallas_call(
        functools.partial(_page_suffix_kernel, heads=heads),
        out_shape=(shape, shape),
        grid=(n_rows // tr,),
        in_specs=[spec],
        out_specs=(spec, spec),
        compiler_params=_params(("arbitrary",), 12 * _nbytes((tr, cols), F32) + (4 << 20)),
        name="fox_page_logf_suffix",
    )(cache_lf)


def _fox_decode_kernel(pt_ref, q_ref, kn_ref, vn_ref, cn_ref, *refs, pages):
    k_refs = refs[:pages]
    v_refs = refs[pages:2 * pages]
    suf_refs = refs[2 * pages:3 * pages]
    tot_refs = refs[3 * pages:4 * pages]
    o_ref, m_ref, l_ref, acc_ref, run_ref = refs[4 * pages:]
    s_idx = pl.program_id(1)
    heads = q_ref.shape[0]
    cols = k_refs[0].shape[0]
    head_mask = (lax.broadcasted_iota(jnp.int32, (heads, cols), 1) % heads
                 == lax.broadcasted_iota(jnp.int32, (heads, cols), 0))

    @pl.when(s_idx == 0)
    def _():
        m_ref[...] = jnp.full_like(m_ref, NEG_BIG)
        l_ref[...] = jnp.zeros_like(l_ref)
        acc_ref[...] = jnp.zeros_like(acc_ref)
        run_ref[...] = jnp.zeros_like(run_ref)

    q = q_ref[...]
    cn = cn_ref[...]
    run = run_ref[...]
    scores = []
    for g in range(pages):
        s = lax.dot_general(q, k_refs[g][...].astype(BF16), (((1,), (1,)), ((), ())),
                            preferred_element_type=F32)
        scores.append(jnp.where(head_mask, s + (suf_refs[g][...] + (run + cn)), NEG_BIG))
        run = run + tot_refs[g][...]
    run_ref[...] = run
    m = m_ref[...]
    m_new = m
    for s in scores:
        m_new = jnp.maximum(m_new, jnp.max(s, axis=-1, keepdims=True))
    alpha = jnp.exp(m - m_new)
    l = alpha * l_ref[...]
    acc = alpha * acc_ref[...]
    for g in range(pages):
        p = jnp.exp(scores[g] - m_new)
        l = l + jnp.sum(p, axis=-1, keepdims=True)
        acc = acc + jnp.dot(p.astype(BF16), v_refs[g][...].astype(BF16), preferred_element_type=F32)
    m_ref[...] = m_new
    l_ref[...] = l
    acc_ref[...] = acc

    @pl.when(s_idx == pl.num_programs(1) - 1)
    def _():
        s_new = jnp.sum(q.astype(F32) * kn_ref[...], axis=-1, keepdims=True)
        m = m_ref[...]
        m_new = jnp.maximum(m, s_new)
        alpha = jnp.exp(m - m_new)
        p_new = jnp.exp(s_new - m_new)
        l = alpha * l_ref[...] + p_new
        acc = alpha * acc_ref[...] + p_new.astype(BF16).astype(F32) * vn_ref[...]
        o_ref[...] = acc / l


def _fox_decode(page_table, pool_base, q, k_new, v_new, logf_new, cache_k, cache_v, cache_lf):
    bs, n_pages = page_table.shape
    _, heads, hd = q.shape
    n_pool, cols, _ = cache_k.shape
    suffix, total = _page_suffix(cache_lf, heads)
    suffix = suffix.reshape(n_pool, 1, cols)
    total = total.reshape(n_pool, 1, cols)
    g_pages = PAGES_PER_STEP if n_pages % PAGES_PER_STEP == 0 else 1
    steps = n_pages // g_pages
    kernel = functools.partial(_fox_decode_kernel, pages=g_pages)

    def page_idx(g):
        return lambda b, s, pt: (pool_base + pt[b * n_pages + (n_pages - 1 - (s * g_pages + g))], 0, 0)

    tok = pl.BlockSpec((None, heads, hd), lambda b, s, pt: (b, 0, 0))
    in_specs = [tok, tok, tok, pl.BlockSpec((None, 1, cols), lambda b, s, pt: (b, 0, 0))]
    in_specs += [pl.BlockSpec((None, cols, hd), page_idx(g)) for g in range(g_pages)]
    in_specs += [pl.BlockSpec((None, cols, hd), page_idx(g)) for g in range(g_pages)]
    in_specs += [pl.BlockSpec((None, 1, cols), page_idx(g)) for g in range(g_pages)]
    in_specs += [pl.BlockSpec((None, 1, cols), page_idx(g)) for g in range(g_pages)]
    vmem = (4 * g_pages * _nbytes((cols, hd), F32) + 2 * g_pages * _nbytes((cols, hd), BF16)
            + 4 * g_pages * _nbytes((heads, cols), F32) + (4 << 20))
    grid_spec = pltpu.PrefetchScalarGridSpec(
        num_scalar_prefetch=1,
        grid=(bs, steps),
        in_specs=in_specs,
        out_specs=pl.BlockSpec((None, heads, hd), lambda b, s, pt: (b, 0, 0)),
        scratch_shapes=[pltpu.VMEM((heads, 1), F32), pltpu.VMEM((heads, 1), F32),
                        pltpu.VMEM((heads, hd), F32), pltpu.VMEM((1, cols), F32)])
    return pl.pallas_call(
        kernel,
        out_shape=jax.ShapeDtypeStruct((bs, heads, hd), F32),
        grid_spec=grid_spec,
        compiler_params=_params(("arbitrary", "arbitrary"), vmem),
        name="fox_decode_attention",
    )(page_table.reshape(-1), q, k_new, v_new, logf_new,
      *([cache_k] * g_pages), *([cache_v] * g_pages), *([suffix] * g_pages), *([total] * g_pages))


def _proj_residual_kernel(a_ref, w_ref, x_ref, gate_ref, o_ref):
    z = jnp.dot(a_ref[...], w_ref[...], preferred_element_type=F32)
    o_ref[...] = x_ref[...] + gate_ref[...] * z


def _proj_residual(a, w, x, mods, mod_rows, tiles_per_group, gate_chunk, tm):
    m_rows, k = a.shape
    d = w.shape[1]
    tn = min(COL_TILE, d)
    vmem = (2 * _nbytes((tm, k), BF16) + 2 * _nbytes((k, tn), BF16) + 5 * _nbytes((tm, tn), F32)
            + (4 << 20))
    return pl.pallas_call(
        _proj_residual_kernel,
        out_shape=jax.ShapeDtypeStruct((m_rows, d), F32),
        grid=(m_rows // tm, d // tn),
        in_specs=[pl.BlockSpec((tm, k), lambda m, n: (m, 0)),
                  pl.BlockSpec((k, tn), lambda m, n: (0, n)),
                  pl.BlockSpec((tm, tn), lambda m, n: (m, n)),
                  pl.BlockSpec((None, mod_rows, tn),
                               lambda m, n: (m // tiles_per_group, 0, gate_chunk * (d // tn) + n))],
        out_specs=pl.BlockSpec((tm, tn), lambda m, n: (m, n)),
        compiler_params=_params(("arbitrary", "arbitrary"), vmem),
        name="mixer_out_proj",
    )(a, w, x, mods)


def _mlp_kernel(x_ref, g_ref, sh_ref, sc_ref, gate_ref, wu_ref, wd_ref, fg_ref, o_ref, h_ref, *,
                final_norm):
    f = pl.program_id(1)

    @pl.when(f == 0)
    def _():
        h_ref[...] = _norm_mod(x_ref[...], g_ref[...], sh_ref[...], sc_ref[...]).astype(BF16)
        o_ref[...] = jnp.zeros_like(o_ref)

    u = jnp.dot(h_ref[...], wu_ref[...], preferred_element_type=F32)
    u = jnp.square(jnp.maximum(u, 0.0)).astype(BF16)
    o_ref[...] += jnp.dot(u, wd_ref[...], preferred_element_type=F32)

    @pl.when(f == pl.num_programs(1) - 1)
    def _():
        y = x_ref[...] + gate_ref[...] * o_ref[...]
        if final_norm:
            y = y * lax.rsqrt(jnp.mean(y * y, axis=-1, keepdims=True) + EPS) * fg_ref[...]
        o_ref[...] = y


def _mlp(x, g, mods, mod_rows, tiles_per_group, w_up, w_down, final_g, final_norm, tm):
    m_rows, d = x.shape
    d_ff = w_up.shape[1]
    tf = min(COL_TILE, d_ff)
    kernel = functools.partial(_mlp_kernel, final_norm=final_norm)
    vmem = (4 * _nbytes((tm, d), F32) + _nbytes((tm, d), BF16) + 4 * _nbytes((d, tf), BF16)
            + 4 * _nbytes((tm, tf), F32) + (4 << 20))
    vec = pl.BlockSpec((1, d), lambda m, f: (0, 0))
    return pl.pallas_call(
        kernel,
        out_shape=jax.ShapeDtypeStruct((m_rows, d), F32),
        grid=(m_rows // tm, d_ff // tf),
        in_specs=[pl.BlockSpec((tm, d), lambda m, f: (m, 0)),
                  vec,
                  _mod_spec(mod_rows, d, 3, tiles_per_group),
                  _mod_spec(mod_rows, d, 4, tiles_per_group),
                  _mod_spec(mod_rows, d, 5, tiles_per_group),
                  pl.BlockSpec((d, tf), lambda m, f: (0, f)),
                  pl.BlockSpec((tf, d), lambda m, f: (f, 0)),
                  vec],
        out_specs=pl.BlockSpec((tm, d), lambda m, f: (m, 0)),
        scratch_shapes=[pltpu.VMEM((tm, d), BF16)],
        compiler_params=_params(("arbitrary", "arbitrary"), vmem),
        name="relu2_mlp",
    )(x, g, mods, mods, mods, w_up, w_down, final_g)


def _ret_proj_kernel(x_ref, g_ref, sh_ref, sc_ref, w_ref, cos_ref, sin_ref,
                     q_ref, k_ref, v_ref, gt_ref, h_ref, *, nq, key_dim, k_scale):
    n = pl.program_id(1)

    @pl.when(n == 0)
    def _():
        h_ref[...] = _norm_mod(x_ref[...], g_ref[...], sh_ref[...], sc_ref[...]).astype(BF16)

    z = jnp.dot(h_ref[...], w_ref[...], preferred_element_type=F32)
    half = key_dim // 2

    def rotate(zz):
        cos = cos_ref[...]
        sin = sin_ref[...]
        parts = []
        for hh in range(zz.shape[1] // key_dim):
            x1 = zz[:, hh * key_dim:hh * key_dim + half]
            x2 = zz[:, hh * key_dim + half:(hh + 1) * key_dim]
            parts += [x1 * cos - x2 * sin, x1 * sin + x2 * cos]
        return jnp.concatenate(parts, axis=1)

    @pl.when(n < nq)
    def _():
        q_ref[...] = rotate(z).astype(BF16)

    @pl.when(jnp.logical_and(n >= nq, n < 2 * nq))
    def _():
        k_ref[...] = rotate(z) * k_scale

    @pl.when(jnp.logical_and(n >= 2 * nq, n < 4 * nq))
    def _():
        v_ref[...] = z.astype(BF16)

    @pl.when(n >= 4 * nq)
    def _():
        gt_ref[...] = z


def _ret_proj(x, g, mods, mod_rows, tiles_per_group, w, cos, sin, pos_tiles, key_dim, tm):
    m_rows, d = x.shape
    tn = min(COL_TILE, d)
    nq = d // tn
    half = key_dim // 2
    kernel = functools.partial(_ret_proj_kernel, nq=nq, key_dim=key_dim, k_scale=key_dim ** -0.5)

    def col(lo, width):
        return lambda m, n: (m, jnp.clip(n - lo, 0, width - 1))

    vmem = (2 * _nbytes((tm, d), F32) + _nbytes((tm, d), BF16) + 2 * _nbytes((d, tn), BF16)
            + 2 * (2 * _nbytes((tm, tn), BF16) + 2 * _nbytes((tm, tn), F32))
            + 4 * _nbytes((tm, half), F32) + 4 * _nbytes((tm, tn), F32) + (4 << 20))
    return pl.pallas_call(
        kernel,
        out_shape=(jax.ShapeDtypeStruct((m_rows, d), BF16), jax.ShapeDtypeStruct((m_rows, d), F32),
                   jax.ShapeDtypeStruct((m_rows, 2 * d), BF16), jax.ShapeDtypeStruct((m_rows, 2 * d), F32)),
        grid=(m_rows // tm, 6 * nq),
        in_specs=[pl.BlockSpec((tm, d), lambda m, n: (m, 0)),
                  pl.BlockSpec((1, d), lambda m, n: (0, 0)),
                  _mod_spec(mod_rows, d, 0, tiles_per_group),
                  _mod_spec(mod_rows, d, 1, tiles_per_group),
                  pl.BlockSpec((d, tn), lambda m, n: (0, n)),
                  pl.BlockSpec((tm, half), lambda m, n: (m % pos_tiles, 0)),
                  pl.BlockSpec((tm, half), lambda m, n: (m % pos_tiles, 0))],
        out_specs=(pl.BlockSpec((tm, tn), col(0, nq)),
                   pl.BlockSpec((tm, tn), col(nq, nq)),
                   pl.BlockSpec((tm, tn), col(2 * nq, 2 * nq)),
                   pl.BlockSpec((tm, tn), col(4 * nq, 2 * nq))),
        scratch_shapes=[pltpu.VMEM((tm, d), BF16)],
        compiler_params=_params(("arbitrary", "arbitrary"), vmem),
        name="ret_in_proj",
    )(x, g, mods, mods, w, cos, sin)


def _group_norm_gate(o, gate):
    mu = jnp.mean(o, axis=-1, keepdims=True)
    cen = o - mu
    var = jnp.mean(cen * cen, axis=-1, keepdims=True)
    return _silu(gate) * (cen * lax.rsqrt(var + EPS))


def _ret_chunk_kernel(lg_ref, q_ref, k_ref, v_ref, gt_ref, y_ref, s_ref):
    lg = jnp.full((1, 1), lg_ref[pl.program_id(1)], F32)

    @pl.when(pl.program_id(2) == 0)
    def _():
        s_ref[...] = jnp.zeros_like(s_ref)

    q = q_ref[...]
    k = k_ref[...]
    v = v_ref[...]
    ln = q.shape[0]
    diff = (lax.broadcasted_iota(jnp.int32, (ln, ln), 0)
            - lax.broadcasted_iota(jnp.int32, (ln, ln), 1)).astype(F32)
    dec = jnp.where(diff >= 0, jnp.exp(lg * jnp.maximum(diff, 0.0)), 0.0)
    n = lax.broadcasted_iota(jnp.int32, (ln, 1), 0).astype(F32)
    xi = jnp.exp(lg * (n + 1.0))
    zeta = jnp.exp(lg * (ln - 1.0 - n))

    s0 = s_ref[...]
    qk = lax.dot_general(q, k.astype(BF16), (((1,), (1,)), ((), ())), preferred_element_type=F32)
    a = (qk * dec).astype(BF16)
    o = (jnp.dot(a, v, preferred_element_type=F32)
         + jnp.dot(q, s0.astype(BF16), preferred_element_type=F32) * xi)
    kz = (k * zeta).astype(BF16)
    s_ref[...] = jnp.exp(lg * ln) * s0 + lax.dot_general(
        kz, v, (((0,), (0,)), ((), ())), preferred_element_type=F32)
    y_ref[...] = _group_norm_gate(o, gt_ref[...]).astype(y_ref.dtype)


def _ret_prompt(lg, q, k, v, gate, batch, seq, heads, key_dim, val_dim):
    ln = min(RET_CHUNK, seq)
    nc = seq // ln
    vmem = (2 * (_nbytes((ln, key_dim), BF16) + _nbytes((ln, key_dim), F32) + _nbytes((ln, val_dim), BF16)
                 + _nbytes((ln, val_dim), F32) + _nbytes((ln, val_dim), BF16))
            + 4 * _nbytes((key_dim, val_dim), F32) + 8 * _nbytes((ln, val_dim), F32) + (8 << 20))
    grid_spec = pltpu.PrefetchScalarGridSpec(
        num_scalar_prefetch=1,
        grid=(batch, heads, nc),
        in_specs=[pl.BlockSpec((ln, key_dim), lambda b, h, c, lg: (b * nc + c, h)),
                  pl.BlockSpec((ln, key_dim), lambda b, h, c, lg: (b * nc + c, h)),
                  pl.BlockSpec((ln, val_dim), lambda b, h, c, lg: (b * nc + c, h)),
                  pl.BlockSpec((ln, val_dim), lambda b, h, c, lg: (b * nc + c, h))],
        out_specs=(pl.BlockSpec((ln, val_dim), lambda b, h, c, lg: (b * nc + c, h)),
                   pl.BlockSpec((None, None, key_dim, val_dim), lambda b, h, c, lg: (b, h, 0, 0))))
    return pl.pallas_call(
        _ret_chunk_kernel,
        out_shape=(jax.ShapeDtypeStruct((batch * seq, heads * val_dim), BF16),
                   jax.ShapeDtypeStruct((batch, heads, key_dim, val_dim), F32)),
        grid_spec=grid_spec,
        compiler_params=_params(("arbitrary", "arbitrary", "arbitrary"), vmem),
        name="ret_prompt_chunks",
    )(lg, q, k, v, gate)


def _ret_decode_kernel(lg_ref, qc_ref, kc_ref, qr_ref, kr_ref, v_ref, gt_ref, s0_ref, y_ref, s_ref):
    gamma = jnp.exp(jnp.full((1, 1), lg_ref[pl.program_id(1)], F32))
    s0 = s0_ref[...]
    v = v_ref[...]
    a = jnp.sum(qr_ref[...] * kr_ref[...], axis=-1, keepdims=True)
    qs = jnp.sum(qc_ref[...] * s0, axis=0, keepdims=True)
    o = a * v + qs * gamma
    s_ref[...] = gamma * s0 + kc_ref[...] * v
    y_ref[...] = _group_norm_gate(o, gt_ref[...])


def _ret_decode(lg, q, k, v, gate, state):
    bs, heads, key_dim, val_dim = state.shape
    q4 = q.reshape(bs, heads, key_dim, 1)
    k4 = k.reshape(bs, heads, key_dim, 1)
    qr = q.reshape(bs, heads, 1, key_dim)
    kr = k.reshape(bs, heads, 1, key_dim)
    v4 = v.reshape(bs, heads, 1, val_dim)
    g4 = gate.reshape(bs, heads, 1, val_dim)

    def spec(r, c):
        return pl.BlockSpec((None, None, r, c), lambda b, h, lg: (b, h, 0, 0))

    grid_spec = pltpu.PrefetchScalarGridSpec(
        num_scalar_prefetch=1,
        grid=(bs, heads),
        in_specs=[spec(key_dim, 1), spec(key_dim, 1), spec(1, key_dim), spec(1, key_dim),
                  spec(1, val_dim), spec(1, val_dim), spec(key_dim, val_dim)],
        out_specs=(spec(1, val_dim), spec(key_dim, val_dim)))
    y, s = pl.pallas_call(
        _ret_decode_kernel,
        out_shape=(jax.ShapeDtypeStruct((bs, heads, 1, val_dim), F32),
                   jax.ShapeDtypeStruct(state.shape, F32)),
        grid_spec=grid_spec,
        compiler_params=_params(("arbitrary", "arbitrary"), 24 << 20),
        name="ret_decode_step",
    )(lg, q4, k4, qr, kr, v4, g4, state)
    return y.reshape(bs, heads * val_dim), s


def _rotary_tables(pos, key_dim):
    half = key_dim // 2
    inv = 1.0 / (ROPE_BASE ** jnp.linspace(0.0, 1.0, half, dtype=F32))
    ang = pos.astype(F32)[:, None] * inv[None, :]
    return jnp.cos(ang), jnp.sin(ang)


def _pad_rows(a, rows):
    return jnp.pad(a, ((0, rows - a.shape[0]), (0, 0)))


def kernel(x_prompt, x_sample, c_prompt, c_sample, cache_fox_k, cache_fox_v, cache_fox_logf, state_ret, page_table, norm1_g, norm2_g, ada_w, ada_b, fox_w_in, fox_b_f, fox_w_out, ret_w_in, ret_w_out, mlp_w_up, mlp_w_down, final_g):
    bp, tp, d = x_prompt.shape
    bs, ts, _ = x_sample.shape
    assert ts == 1, "the decode kernels handle one new token per sequence"
    depth = ada_w.shape[0]
    _, n_pool, page, fox_heads, fox_hd = cache_fox_k.shape
    _, _, ret_heads, key_dim, val_dim = state_ret.shape
    past_len = page_table.shape[1] * page
    srows = -(-bs // BF16_SUBLANES) * BF16_SUBLANES
    tm = min(ROW_TILE, tp)
    tiles_p = tp // tm

    mod_rows = -(-(bs + bp) // 8) * 8
    c_all = _pad_rows(jnp.concatenate([c_sample, c_prompt], axis=0), max(mod_rows, srows))
    mods = _ada(c_all, ada_w, ada_b)

    lg = jnp.log1p(-jnp.exp2(-5.0 - jnp.arange(ret_heads, dtype=F32)))
    cos_p, sin_p = _rotary_tables(jnp.arange(tp), key_dim)
    cos_s, sin_s = _rotary_tables(jnp.full((srows,), past_len), key_dim)

    yp = x_prompt.reshape(bp * tp, d)
    ys = _pad_rows(x_sample.reshape(bs, d), srows)
    fg = final_g.reshape(1, d)
    outs = {name: [] for name in ("kp", "vp", "lfp", "ks", "vs", "lfs", "sp", "ss")}

    for i in range(depth):
        mods_p = mods[i, bs:bs + bp].reshape(bp, 1, 6 * d)
        mods_s = mods[i, :srows].reshape(1, srows, 6 * d)
        g1 = norm1_g[i].reshape(1, d)
        g2 = norm2_g[i].reshape(1, d)
        j = i // 2
        if i % 2 == 0:
            w_qkv = fox_w_in[j, :, :3 * d].astype(BF16)
            w_f = fox_w_in[j, :, 3 * d:].astype(BF16)
            b_f = fox_b_f[j].reshape(1, fox_heads)
            w_out = fox_w_out[j].astype(BF16)
            q, kf, vf, kb, vb, lf = _fox_proj(yp, g1, mods_p, 1, tiles_p, w_qkv, w_f, b_f, tm,
                                              fox_hd ** -0.5 * LOG2E)
            c = _cumsum(lf, bp, tp) * LOG2E
            c_rep = jnp.broadcast_to(c.reshape(bp, tp, fox_heads).transpose(0, 2, 1)[..., None],
                                     (bp, fox_heads, tp, 128))
            o = _fox_attn(q, kb, vb, c_rep, bp, tp, fox_heads, fox_hd)
            yp = _proj_residual(o, w_out, yp, mods_p, 1, tiles_p, 2, tm)
            outs["kp"].append(kf.reshape(bp, tp, fox_heads, fox_hd))
            outs["vp"].append(vf.reshape(bp, tp, fox_heads, fox_hd))
            outs["lfp"].append(lf.reshape(bp, tp, fox_heads))
            q, kf, vf, kb, vb, lf = _fox_proj(ys, g1, mods_s, srows, 1, w_qkv, w_f, b_f, srows,
                                              fox_hd ** -0.5)
            o = _fox_decode(
                page_table, j * n_pool,
                q[:bs].reshape(bs, fox_heads, fox_hd), kb[:bs].astype(F32).reshape(bs, fox_heads, fox_hd),
                vb[:bs].astype(F32).reshape(bs, fox_heads, fox_hd),
                jnp.tile(lf[:bs], (1, page)).reshape(bs, 1, page * fox_heads),
                cache_fox_k.reshape(-1, page * fox_heads, fox_hd),
                cache_fox_v.reshape(-1, page * fox_heads, fox_hd),
                cache_fox_logf.reshape(-1, page * fox_heads))
            o = _pad_rows(o.reshape(bs, d), srows).astype(BF16)
            ys = _proj_residual(o, w_out, ys, mods_s, srows, 1, 2, srows)
            outs["ks"].append(kf[:bs].reshape(bs, ts, fox_heads, fox_hd))
            outs["vs"].append(vf[:bs].reshape(bs, ts, fox_heads, fox_hd))
            outs["lfs"].append(lf[:bs].reshape(bs, ts, fox_heads))
        else:
            w_in = ret_w_in[j].astype(BF16)
            w_out = ret_w_out[j].astype(BF16)
            q, k, v, gate = _ret_proj(yp, g1, mods_p, 1, tiles_p, w_in, cos_p, sin_p, tiles_p, key_dim, tm)
            y, s = _ret_prompt(lg, q, k, v, gate, bp, tp, ret_heads, key_dim, val_dim)
            yp = _proj_residual(y, w_out, yp, mods_p, 1, tiles_p, 2, tm)
            outs["sp"].append(s)
            q, k, v, gate = _ret_proj(ys, g1, mods_s, srows, 1, w_in, cos_s, sin_s, 1, key_dim, srows)
            y, s = _ret_decode(lg, q[:bs].astype(F32), k[:bs], v[:bs].astype(F32), gate[:bs],
                               state_ret[j].astype(F32))
            ys = _proj_residual(_pad_rows(y, srows).astype(BF16), w_out, ys, mods_s, srows, 1, 2, srows)
            outs["ss"].append(s)
        last = i == depth - 1
        w_up = mlp_w_up[i].astype(BF16)
        w_down = mlp_w_down[i].astype(BF16)
        yp = _mlp(yp, g2, mods_p, 1, tiles_p, w_up, w_down, fg, last, tm)
        ys = _mlp(ys, g2, mods_s, srows, 1, w_up, w_down, fg, last, srows)

    return (yp.reshape(bp, tp, d), ys[:bs].reshape(bs, ts, d),
            jnp.stack(outs["kp"]), jnp.stack(outs["vp"]), jnp.stack(outs["lfp"]),
            jnp.stack(outs["ks"]), jnp.stack(outs["vs"]), jnp.stack(outs["lfs"]),
            jnp.stack(outs["sp"]), jnp.stack(outs["ss"]))
```

```python
import functools

import jax
import jax.numpy as jnp
from jax import lax
from jax.experimental import pallas as pl
from jax.experimental.pallas import tpu as pltpu

F32 = jnp.float32
BF16 = jnp.bfloat16

EPS = 1e-6
ROPE_BASE = 10000.0
LOG2E = 1.4426950408889634
NEG_BIG = -1e30

V7X_VMEM_LIMIT_BYTES = 60000 * 1024
BF16_SUBLANES = 16

ROW_TILE = 1024
COL_TILE = 512
ATTN_Q_TILE = 1024
ATTN_K_TILE = 1024
ATTN_K_SUB = 512
ATTN_HEADS_PER_STEP = 2
CUMSUM_TILE = 512
RET_CHUNK = 512
PAGES_PER_STEP = 8


def _params(semantics, vmem_bytes):
    return pltpu.CompilerParams(dimension_semantics=semantics,
                                vmem_limit_bytes=int(min(vmem_bytes, V7X_VMEM_LIMIT_BYTES)))


def _nbytes(shape, dtype):
    n = 1
    for s in shape:
        n *= s
    return n * jnp.dtype(dtype).itemsize


def _norm_mod(x, g, shift, scale):
    y = x * lax.rsqrt(jnp.mean(x * x, axis=-1, keepdims=True) + EPS)
    return (y * g) * (1.0 + scale) + shift


def _silu(x):
    return x * jax.nn.sigmoid(x)


def _log_sigmoid(x):
    return jnp.minimum(x, 0.0) - jnp.log1p(jnp.exp(-jnp.abs(x)))


def _split3(x):
    hi = x.astype(BF16)
    r1 = x - hi.astype(F32)
    mid = r1.astype(BF16)
    lo = (r1 - mid.astype(F32)).astype(BF16)
    return hi, mid, lo


def _mod_spec(rows, width, chunk, tiles_per_group):
    return pl.BlockSpec((None, rows, width), lambda m, n: (m // tiles_per_group, 0, chunk))


def _ada_kernel(c_ref, w_ref, b_ref, o_ref):
    o_ref[...] = jnp.dot(_silu(c_ref[...]), w_ref[...], preferred_element_type=F32) + b_ref[...]


def _ada(c_all, ada_w, ada_b):
    depth, d, n6 = ada_w.shape
    rows = c_all.shape[0]
    tn = min(1024, n6)
    vmem = 2 * (_nbytes((d, tn), F32) + _nbytes((rows, tn), F32)) + _nbytes((rows, d), F32) * 2 + (4 << 20)
    return pl.pallas_call(
        _ada_kernel,
        out_shape=jax.ShapeDtypeStruct((depth, rows, n6), F32),
        grid=(depth, n6 // tn),
        in_specs=[pl.BlockSpec((rows, d), lambda i, n: (0, 0)),
                  pl.BlockSpec((None, d, tn), lambda i, n: (i, 0, n)),
                  pl.BlockSpec((None, 1, tn), lambda i, n: (i, 0, n))],
        out_specs=pl.BlockSpec((None, rows, tn), lambda i, n: (i, 0, n)),
        compiler_params=_params(("arbitrary", "arbitrary"), vmem),
        name="ada_modulation",
    )(c_all, ada_w, ada_b.reshape(depth, 1, n6))


def _fox_proj_kernel(x_ref, xs_ref, g_ref, sh_ref, sc_ref, shs_ref, scs_ref, w_ref, wf_ref, bf_ref,
                     q_ref, kf_ref, vf_ref, kb_ref, vb_ref, lf_ref,
                     qs_ref, kfs_ref, vfs_ref, kbs_ref, vbs_ref, lfs_ref, h_ref, *,
                     nq, tm, q_scale, q_scale_s):
    n = pl.program_id(1)

    @pl.when(n == 0)
    def _():
        for x_r, sh_r, sc_r, lf_r, rows in ((x_ref, sh_ref, sc_ref, lf_ref, slice(0, tm)),
                                           (xs_ref, shs_ref, scs_ref, lfs_ref, slice(tm, None))):
            h = _norm_mod(x_r[...], g_ref[...], sh_r[...], sc_r[...]).astype(BF16)
            h_ref[rows, :] = h
            fl = jnp.dot(h, wf_ref[...], preferred_element_type=F32) + bf_ref[...]
            lf_r[...] = _log_sigmoid(fl)

    z = jnp.dot(h_ref[...], w_ref[...].astype(BF16), preferred_element_type=F32)
    zp = z[:tm]
    zs = z[tm:]
    last_tile = pl.program_id(0) == pl.num_programs(0) - 1
    in_q = n < nq
    in_k = jnp.logical_and(n >= nq, n < 2 * nq)
    in_v = n >= 2 * nq

    @pl.when(in_q)
    def _():
        q_ref[...] = (zp * q_scale).astype(BF16)

    @pl.when(in_k)
    def _():
        kf_ref[...] = zp
        kb_ref[...] = zp.astype(BF16)

    @pl.when(in_v)
    def _():
        vf_ref[...] = zp
        vb_ref[...] = zp.astype(BF16)

    @pl.when(jnp.logical_and(last_tile, in_q))
    def _():
        qs_ref[...] = (zs * q_scale_s).astype(BF16)

    @pl.when(jnp.logical_and(last_tile, in_k))
    def _():
        kfs_ref[...] = zs
        kbs_ref[...] = zs.astype(BF16)

    @pl.when(jnp.logical_and(last_tile, in_v))
    def _():
        vfs_ref[...] = zs
        vbs_ref[...] = zs.astype(BF16)


def _fox_proj(x, xs, g, mods_p, mods_s, tiles_per_batch, w_in, layer, w_f, b_f, tm, q_scale, q_scale_s):
    m_rows, d = x.shape
    srows = xs.shape[0]
    heads = w_f.shape[1]
    tn = min(COL_TILE, d)
    nq = d // tn
    kernel = functools.partial(_fox_proj_kernel, nq=nq, tm=tm, q_scale=q_scale, q_scale_s=q_scale_s)

    last_tile = m_rows // tm - 1

    def col(lo, by_tile):
        if by_tile:
            return lambda m, n: (m, jnp.clip(n - lo, 0, nq - 1))
        return lambda m, n: (0, jnp.where(m == last_tile, jnp.clip(n - lo, 0, nq - 1), 0))

    vmem = (2 * _nbytes((tm, d), F32) + _nbytes((tm + srows, d), BF16) + 2 * _nbytes((d, tn), F32)
            + _nbytes((d, tn), BF16) + 2 * (3 * _nbytes((tm, tn), BF16) + 2 * _nbytes((tm, tn), F32))
            + 3 * _nbytes((tm, tn), F32) + (4 << 20))

    def shapes(rows):
        f = jax.ShapeDtypeStruct((rows, d), F32)
        b = jax.ShapeDtypeStruct((rows, d), BF16)
        return (b, f, f, b, b, jax.ShapeDtypeStruct((rows, heads), F32))

    def out_specs(rows, by_tile):
        blk = lambda lo: pl.BlockSpec((rows, tn), col(lo, by_tile))
        last = (lambda m, n: (m, 0)) if by_tile else (lambda m, n: (0, 0))
        return (blk(0), blk(nq), blk(2 * nq), blk(nq), blk(2 * nq), pl.BlockSpec((rows, heads), last))

    outs = pl.pallas_call(
        kernel,
        out_shape=shapes(m_rows) + shapes(srows),
        grid=(m_rows // tm, 3 * nq),
        in_specs=[pl.BlockSpec((tm, d), lambda m, n: (m, 0)),
                  pl.BlockSpec((srows, d), lambda m, n: (0, 0)),
                  pl.BlockSpec((1, d), lambda m, n: (0, 0)),
                  _mod_spec(1, d, 0, tiles_per_batch),
                  _mod_spec(1, d, 1, tiles_per_batch),
                  _mod_spec(srows, d, 0, m_rows),
                  _mod_spec(srows, d, 1, m_rows),
                  pl.BlockSpec((None, d, tn), lambda m, n: (layer, 0, n)),
                  pl.BlockSpec((d, heads), lambda m, n: (0, 0)),
                  pl.BlockSpec((1, heads), lambda m, n: (0, 0))],
        out_specs=out_specs(tm, True) + out_specs(srows, False),
        scratch_shapes=[pltpu.VMEM((tm + srows, d), BF16)],
        compiler_params=_params(("arbitrary", "arbitrary"), vmem),
        name="fox_in_proj",
    )(x, xs, g, mods_p, mods_p, mods_s, mods_s, w_in, w_f, b_f)
    return outs[:6], outs[6:]


def _cumsum_kernel(lf_ref, aug_ref, carry_ref, *, hd):
    @pl.when(pl.program_id(1) == 0)
    def _():
        carry_ref[...] = jnp.zeros_like(carry_ref)

    tc, heads = lf_ref.shape
    row = lax.broadcasted_iota(jnp.int32, (tc, tc), 0)
    col = lax.broadcasted_iota(jnp.int32, (tc, tc), 1)
    tri = jnp.where(col <= row, 1.0, 0.0).astype(BF16)
    c = carry_ref[...]
    for piece in _split3(lf_ref[...]):
        c = c + jnp.dot(tri, piece, preferred_element_type=F32)
    carry_ref[...] = c[tc - 1:tc, :]

    head = lax.broadcasted_iota(jnp.int32, (heads, heads * hd), 0)
    lane = lax.broadcasted_iota(jnp.int32, (heads, heads * hd), 1)
    aug = jnp.zeros((tc, heads * hd), F32)
    for slot, piece in enumerate(_split3(c * (-LOG2E))):
        place = jnp.where(lane == head * hd + slot, 1.0, 0.0).astype(BF16)
        aug = aug + jnp.dot(piece, place, preferred_element_type=F32)
    aug_ref[...] = aug.astype(BF16)


def _cumsum(logf, batch, seq, hd):
    rows, heads = logf.shape
    tc = min(CUMSUM_TILE, seq)
    per = seq // tc
    return pl.pallas_call(
        functools.partial(_cumsum_kernel, hd=hd),
        out_shape=jax.ShapeDtypeStruct((rows, heads * hd), BF16),
        grid=(batch, per),
        in_specs=[pl.BlockSpec((tc, heads), lambda b, t: (b * per + t, 0))],
        out_specs=pl.BlockSpec((tc, heads * hd), lambda b, t: (b * per + t, 0)),
        scratch_shapes=[pltpu.VMEM((1, heads), F32)],
        compiler_params=_params(("arbitrary", "arbitrary"), 24 << 20),
        name="fox_logf_cumsum",
    )(logf)


def _fox_attn_kernel(q_ref, k_ref, v_ref, aug_ref, o_ref, vt_ref, *, tq, tk, ts, hd, heads_per_step):
    i = pl.program_id(2)
    seq = k_ref.shape[0]
    q0 = pl.multiple_of(i * tq, tq)
    lane_tile = 128

    @pl.when(i == 0)
    def _():
        for t in range(seq // tk):
            vt_ref[:, t * tk:(t + 1) * tk] = v_ref[t * tk:(t + 1) * tk, :].astype(F32).T.astype(BF16)

    ones = jnp.where(lax.broadcasted_iota(jnp.int32, (tq, hd), 1) < 3, 1.0, 0.0).astype(BF16)
    q_ext = [jnp.concatenate([q_ref[:, hh * hd:(hh + 1) * hd], ones], axis=1)
             for hh in range(heads_per_step)]

    def step(j, carry, masked):
        carry = list(carry)
        items = [(sub, hh) for sub in range(tk // ts) for hh in range(heads_per_step)]

        def scores(sub, hh):
            k0 = pl.multiple_of(j * tk + sub * ts, ts)
            lanes = slice(hh * hd, (hh + 1) * hd)
            k_ext = jnp.concatenate([k_ref[pl.ds(k0, ts), lanes], aug_ref[pl.ds(k0, ts), lanes]], axis=1)
            return lax.dot_general(k_ext, q_ext[hh], (((1,), (1,)), ((), ())),
                                   preferred_element_type=F32)

        s_next = scores(*items[0])
        for n, (sub, hh) in enumerate(items):
            s = s_next
            if n + 1 < len(items):
                s_next = scores(*items[n + 1])
            m, l, acc = carry[3 * hh:3 * hh + 3]
            k0 = pl.multiple_of(j * tk + sub * ts, ts)
            m_t, l_t, p_t = [], [], []
            for qt in range(tq // lane_tile):
                ql = slice(qt * lane_tile, (qt + 1) * lane_tile)
                sq = s[:, ql]
                if masked:
                    keep = (k0 + lax.broadcasted_iota(jnp.int32, (ts, lane_tile), 0)
                            <= q0 + qt * lane_tile + lax.broadcasted_iota(jnp.int32, (ts, lane_tile), 1))
                    sq = jnp.where(keep, sq, NEG_BIG)
                mq = jnp.maximum(m[:, ql], jnp.max(sq, axis=0, keepdims=True))
                pq = jnp.exp2(sq - mq)
                l_t.append(jnp.exp2(m[:, ql] - mq) * l[:, ql] + jnp.sum(pq, axis=0, keepdims=True))
                m_t.append(mq)
                p_t.append(pq.astype(BF16))
            m_new = jnp.concatenate(m_t, axis=1)
            alpha = jnp.exp2(m - m_new)
            vt = vt_ref[hh * hd:(hh + 1) * hd, pl.ds(k0, ts)]
            acc = alpha * acc + jnp.dot(vt, jnp.concatenate(p_t, axis=1), preferred_element_type=F32)
            carry[3 * hh:3 * hh + 3] = [m_new, jnp.concatenate(l_t, axis=1), acc]
        return tuple(carry)

    init = (jnp.full((1, tq), NEG_BIG, F32), jnp.zeros((1, tq), F32),
            jnp.zeros((hd, tq), F32)) * heads_per_step
    n_full = q0 // tk
    carry = lax.fori_loop(0, n_full, lambda j, c: step(j, c, False), init)
    carry = step(n_full, carry, True)
    for hh in range(heads_per_step):
        _, l, acc = carry[3 * hh:3 * hh + 3]
        o_ref[:, hh * hd:(hh + 1) * hd] = (acc / l).T.astype(o_ref.dtype)


def _fox_attn(q, k, v, aug, batch, seq, heads, head_dim):
    tq = min(ATTN_Q_TILE, seq)
    tk = min(ATTN_K_TILE, seq)
    nq = seq // tq
    hps = ATTN_HEADS_PER_STEP if heads % ATTN_HEADS_PER_STEP == 0 else 1
    width = hps * head_dim
    kernel = functools.partial(_fox_attn_kernel, tq=tq, tk=tk, ts=min(ATTN_K_SUB, tk), hd=head_dim,
                               heads_per_step=hps)
    vmem = 7 * _nbytes((seq, width), BF16) + 6 * hps * _nbytes((tk, tq), F32) + (8 << 20)
    return pl.pallas_call(
        kernel,
        out_shape=jax.ShapeDtypeStruct(q.shape, BF16),
        grid=(batch, heads // hps, nq),
        in_specs=[pl.BlockSpec((tq, width), lambda b, h, i: (b * nq + i, h)),
                  pl.BlockSpec((seq, width), lambda b, h, i: (b, h)),
                  pl.BlockSpec((seq, width), lambda b, h, i: (b, h)),
                  pl.BlockSpec((seq, width), lambda b, h, i: (b, h))],
        out_specs=pl.BlockSpec((tq, width), lambda b, h, i: (b * nq + i, h)),
        scratch_shapes=[pltpu.VMEM((width, seq), BF16)],
        compiler_params=_params(("arbitrary", "arbitrary", "arbitrary"), vmem),
        name="fox_prompt_attention",
    )(q, k, v, aug)


def _page_suffix_kernel(lf_ref, suf_ref, tot_ref, *, heads):
    x = lf_ref[...]
    cols = x.shape[1]
    col = lax.broadcasted_iota(jnp.int32, x.shape, 1)
    incl = x
    tot = x
    sh = heads
    while sh < cols:
        incl = incl + jnp.where(col + sh < cols, pltpu.roll(incl, cols - sh, 1), 0.0)
        tot = tot + pltpu.roll(tot, sh, 1)
        sh *= 2
    suf_ref[...] = incl - x
    tot_ref[...] = tot


def _page_suffix(cache_lf, heads):
    n_rows, cols = cache_lf.shape
    tr = 256 if n_rows % 256 == 0 else n_rows
    shape = jax.ShapeDtypeStruct((n_rows, cols), F32)
    spec = pl.BlockSpec((tr, cols), lambda r: (r, 0))
    return pl.pallas_call(
        functools.partial(_page_suffix_kernel, heads=heads),
        out_shape=(shape, shape),
        grid=(n_rows // tr,),
        in_specs=[spec],
        out_specs=(spec, spec),
        compiler_params=_params(("arbitrary",), 12 * _nbytes((tr, cols), F32) + (4 << 20)),
        name="fox_page_logf_suffix",
    )(cache_lf)


def _fox_decode_kernel(pt_ref, q_ref, kn_ref, vn_ref, cn_ref, *refs, pages):
    k_refs = refs[:pages]
    v_refs = refs[pages:2 * pages]
    suf_refs = refs[2 * pages:3 * pages]
    tot_refs = refs[3 * pages:4 * pages]
    o_ref, m_ref, l_ref, acc_ref, run_ref = refs[4 * pages:]
    s_idx = pl.program_id(1)
    heads = q_ref.shape[0]
    cols = k_refs[0].shape[0]
    head_mask = (lax.broadcasted_iota(jnp.int32, (heads, cols), 1) % heads
                 == lax.broadcasted_iota(jnp.int32, (heads, cols), 0))

    @pl.when(s_idx == 0)
    def _():
        m_ref[...] = jnp.full_like(m_ref, NEG_BIG)
        l_ref[...] = jnp.zeros_like(l_ref)
        acc_ref[...] = jnp.zeros_like(acc_ref)
        run_ref[...] = jnp.zeros_like(run_ref)

    q = q_ref[...]
    cn = cn_ref[...]
    run = run_ref[...]
    scores = []
    for g in range(pages):
        s = lax.dot_general(q, k_refs[g][...].astype(BF16), (((1,), (1,)), ((), ())),
                            preferred_element_type=F32)
        scores.append(jnp.where(head_mask, s + (suf_refs[g][...] + (run + cn)), NEG_BIG))
        run = run + tot_refs[g][...]
    run_ref[...] = run
    m = m_ref[...]
    m_new = m
    for s in scores:
        m_new = jnp.maximum(m_new, jnp.max(s, axis=-1, keepdims=True))
    alpha = jnp.exp(m - m_new)
    l = alpha * l_ref[...]
    acc = alpha * acc_ref[...]
    for g in range(pages):
        p = jnp.exp(scores[g] - m_new)
        l = l + jnp.sum(p, axis=-1, keepdims=True)
        acc = acc + jnp.dot(p.astype(BF16), v_refs[g][...].astype(BF16), preferred_element_type=F32)
    m_ref[...] = m_new
    l_ref[...] = l
    acc_ref[...] = acc

    @pl.when(s_idx == pl.num_programs(1) - 1)
    def _():
        s_new = jnp.sum(q.astype(F32) * kn_ref[...], axis=-1, keepdims=True)
        m = m_ref[...]
        m_new = jnp.maximum(m, s_new)
        alpha = jnp.exp(m - m_new)
        p_new = jnp.exp(s_new - m_new)
        l = alpha * l_ref[...] + p_new
        acc = alpha * acc_ref[...] + p_new.astype(BF16).astype(F32) * vn_ref[...]
        o_ref[...] = acc / l


def _fox_decode(page_table, pool_base, q, k_new, v_new, logf_new, cache_k, cache_v, cache_lf):
    bs, n_pages = page_table.shape
    _, heads, hd = q.shape
    n_pool, cols, _ = cache_k.shape
    suffix, total = _page_suffix(cache_lf, heads)
    suffix = suffix.reshape(n_pool, 1, cols)
    total = total.reshape(n_pool, 1, cols)
    g_pages = PAGES_PER_STEP if n_pages % PAGES_PER_STEP == 0 else 1
    steps = n_pages // g_pages
    kernel = functools.partial(_fox_decode_kernel, pages=g_pages)

    def page_idx(g):
        return lambda b, s, pt: (pool_base + pt[b * n_pages + (n_pages - 1 - (s * g_pages + g))], 0, 0)

    tok = pl.BlockSpec((None, heads, hd), lambda b, s, pt: (b, 0, 0))
    in_specs = [tok, tok, tok, pl.BlockSpec((None, 1, cols), lambda b, s, pt: (b, 0, 0))]
    in_specs += [pl.BlockSpec((None, cols, hd), page_idx(g)) for g in range(g_pages)]
    in_specs += [pl.BlockSpec((None, cols, hd), page_idx(g)) for g in range(g_pages)]
    in_specs += [pl.BlockSpec((None, 1, cols), page_idx(g)) for g in range(g_pages)]
    in_specs += [pl.BlockSpec((None, 1, cols), page_idx(g)) for g in range(g_pages)]
    vmem = (4 * g_pages * _nbytes((cols, hd), F32) + 2 * g_pages * _nbytes((cols, hd), BF16)
            + 4 * g_pages * _nbytes((heads, cols), F32) + (4 << 20))
    grid_spec = pltpu.PrefetchScalarGridSpec(
        num_scalar_prefetch=1,
        grid=(bs, steps),
        in_specs=in_specs,
        out_specs=pl.BlockSpec((None, heads, hd), lambda b, s, pt: (b, 0, 0)),
        scratch_shapes=[pltpu.VMEM((heads, 1), F32), pltpu.VMEM((heads, 1), F32),
                        pltpu.VMEM((heads, hd), F32), pltpu.VMEM((1, cols), F32)])
    return pl.pallas_call(
        kernel,
        out_shape=jax.ShapeDtypeStruct((bs, heads, hd), F32),
        grid_spec=grid_spec,
        compiler_params=_params(("arbitrary", "arbitrary"), vmem),
        name="fox_decode_attention",
    )(page_table.reshape(-1), q, k_new, v_new, logf_new,
      *([cache_k] * g_pages), *([cache_v] * g_pages), *([suffix] * g_pages), *([total] * g_pages))


def _proj_residual_kernel(a_ref, w_ref, x_ref, gate_ref, o_ref):
    z = jnp.dot(a_ref[...], w_ref[...].astype(BF16), preferred_element_type=F32)
    o_ref[...] = x_ref[...] + gate_ref[...] * z


def _proj_residual(a, w, layer, x, mods, mod_rows, tiles_per_group, gate_chunk, tm):
    m_rows, k = a.shape
    d = w.shape[2]
    tn = min(COL_TILE, d)
    vmem = (2 * _nbytes((tm, k), BF16) + 2 * _nbytes((k, tn), F32) + _nbytes((k, tn), BF16)
            + 5 * _nbytes((tm, tn), F32) + (4 << 20))
    return pl.pallas_call(
        _proj_residual_kernel,
        out_shape=jax.ShapeDtypeStruct((m_rows, d), F32),
        grid=(m_rows // tm, d // tn),
        in_specs=[pl.BlockSpec((tm, k), lambda m, n: (m, 0)),
                  pl.BlockSpec((None, k, tn), lambda m, n: (layer, 0, n)),
                  pl.BlockSpec((tm, tn), lambda m, n: (m, n)),
                  pl.BlockSpec((None, mod_rows, tn),
                               lambda m, n: (m // tiles_per_group, 0, gate_chunk * (d // tn) + n))],
        out_specs=pl.BlockSpec((tm, tn), lambda m, n: (m, n)),
        compiler_params=_params(("arbitrary", "arbitrary"), vmem),
        name="mixer_out_proj",
    )(a, w, x, mods)


def _mlp_kernel(x_ref, xs_ref, g_ref, sh_ref, sc_ref, gate_ref, shs_ref, scs_ref, gates_ref,
                wu_ref, wd_ref, fg_ref, o_ref, os_ref, h_ref, *, tm, final_norm):
    f = pl.program_id(1)

    @pl.when(f == 0)
    def _():
        h_ref[:tm, :] = _norm_mod(x_ref[...], g_ref[...], sh_ref[...], sc_ref[...]).astype(BF16)
        h_ref[tm:, :] = _norm_mod(xs_ref[...], g_ref[...], shs_ref[...], scs_ref[...]).astype(BF16)
        o_ref[...] = jnp.zeros_like(o_ref)
        os_ref[...] = jnp.zeros_like(os_ref)

    u = jnp.dot(h_ref[...], wu_ref[...].astype(BF16), preferred_element_type=F32)
    u = jnp.square(jnp.maximum(u, 0.0)).astype(BF16)
    z = jnp.dot(u, wd_ref[...].astype(BF16), preferred_element_type=F32)
    o_ref[...] += z[:tm]
    os_ref[...] += z[tm:]

    @pl.when(f == pl.num_programs(1) - 1)
    def _():
        for x_r, gate_r, o_r in ((x_ref, gate_ref, o_ref), (xs_ref, gates_ref, os_ref)):
            y = x_r[...] + gate_r[...] * o_r[...]
            if final_norm:
                y = y * lax.rsqrt(jnp.mean(y * y, axis=-1, keepdims=True) + EPS) * fg_ref[...]
            o_r[...] = y


def _mlp(x, xs, g, mods_p, mods_s, tiles_per_batch, w_up, w_down, layer, final_g, final_norm, tm):
    m_rows, d = x.shape
    srows = xs.shape[0]
    d_ff = w_up.shape[2]
    tf = min(COL_TILE, d_ff)
    kernel = functools.partial(_mlp_kernel, tm=tm, final_norm=final_norm)
    vmem = (3 * _nbytes((tm, d), F32) + _nbytes((tm + srows, d), BF16) + 4 * _nbytes((d, tf), F32)
            + 2 * _nbytes((d, tf), BF16) + 3 * _nbytes((tm, tf), F32) + (4 << 20))
    full = lambda m, f: (0, 0)
    vec = pl.BlockSpec((1, d), full)
    return pl.pallas_call(
        kernel,
        out_shape=(jax.ShapeDtypeStruct((m_rows, d), F32), jax.ShapeDtypeStruct((srows, d), F32)),
        grid=(m_rows // tm, d_ff // tf),
        in_specs=[pl.BlockSpec((tm, d), lambda m, f: (m, 0), pipeline_mode=pl.Buffered(1)),
                  pl.BlockSpec((srows, d), full),
                  vec,
                  _mod_spec(1, d, 3, tiles_per_batch),
                  _mod_spec(1, d, 4, tiles_per_batch),
                  _mod_spec(1, d, 5, tiles_per_batch),
                  _mod_spec(srows, d, 3, m_rows),
                  _mod_spec(srows, d, 4, m_rows),
                  _mod_spec(srows, d, 5, m_rows),
                  pl.BlockSpec((None, d, tf), lambda m, f: (layer, 0, f)),
                  pl.BlockSpec((None, tf, d), lambda m, f: (layer, f, 0)),
                  vec],
        out_specs=(pl.BlockSpec((tm, d), lambda m, f: (m, 0)), pl.BlockSpec((srows, d), full)),
        scratch_shapes=[pltpu.VMEM((tm + srows, d), BF16)],
        compiler_params=_params(("arbitrary", "arbitrary"), vmem),
        name="relu2_mlp",
    )(x, xs, g, mods_p, mods_p, mods_p, mods_s, mods_s, mods_s, w_up, w_down, final_g)


def _ret_proj_kernel(x_ref, xs_ref, g_ref, sh_ref, sc_ref, shs_ref, scs_ref, w_ref,
                     cos_ref, sin_ref, coss_ref, sins_ref,
                     q_ref, k_ref, v_ref, gt_ref, qs_ref, ks_ref, vs_ref, gts_ref, h_ref, *,
                     nq, tm, key_dim, k_scale):
    n = pl.program_id(1)

    @pl.when(n == 0)
    def _():
        h_ref[:tm, :] = _norm_mod(x_ref[...], g_ref[...], sh_ref[...], sc_ref[...]).astype(BF16)
        h_ref[tm:, :] = _norm_mod(xs_ref[...], g_ref[...], shs_ref[...], scs_ref[...]).astype(BF16)

    z = jnp.dot(h_ref[...], w_ref[...].astype(BF16), preferred_element_type=F32)
    zp = z[:tm]
    zs = z[tm:]
    half = key_dim // 2

    def rotate(zz, c_ref, s_ref):
        cos = c_ref[...]
        sin = s_ref[...]
        parts = []
        for hh in range(zz.shape[1] // key_dim):
            x1 = zz[:, hh * key_dim:hh * key_dim + half]
            x2 = zz[:, hh * key_dim + half:(hh + 1) * key_dim]
            parts += [x1 * cos - x2 * sin, x1 * sin + x2 * cos]
        return jnp.concatenate(parts, axis=1)

    last_tile = pl.program_id(0) == pl.num_programs(0) - 1
    in_q = n < nq
    in_k = jnp.logical_and(n >= nq, n < 2 * nq)
    in_v = jnp.logical_and(n >= 2 * nq, n < 4 * nq)
    in_g = n >= 4 * nq

    @pl.when(in_q)
    def _():
        q_ref[...] = rotate(zp, cos_ref, sin_ref).astype(BF16)

    @pl.when(in_k)
    def _():
        k_ref[...] = rotate(zp, cos_ref, sin_ref) * k_scale

    @pl.when(in_v)
    def _():
        v_ref[...] = zp.astype(BF16)

    @pl.when(in_g)
    def _():
        gt_ref[...] = zp

    @pl.when(jnp.logical_and(last_tile, in_q))
    def _():
        qs_ref[...] = rotate(zs, coss_ref, sins_ref).astype(BF16)

    @pl.when(jnp.logical_and(last_tile, in_k))
    def _():
        ks_ref[...] = rotate(zs, coss_ref, sins_ref) * k_scale

    @pl.when(jnp.logical_and(last_tile, in_v))
    def _():
        vs_ref[...] = zs.astype(BF16)

    @pl.when(jnp.logical_and(last_tile, in_g))
    def _():
        gts_ref[...] = zs


def _ret_proj(x, xs, g, mods_p, mods_s, tiles_per_batch, w_in, layer, cos, sin, cos_s, sin_s, key_dim, tm):
    m_rows, d = x.shape
    srows = xs.shape[0]
    tn = min(COL_TILE, d)
    nq = d // tn
    half = key_dim // 2
    kernel = functools.partial(_ret_proj_kernel, nq=nq, tm=tm, key_dim=key_dim, k_scale=key_dim ** -0.5)

    last_tile = m_rows // tm - 1

    def col(lo, width, by_tile):
        if by_tile:
            return lambda m, n: (m, jnp.clip(n - lo, 0, width - 1))
        return lambda m, n: (0, jnp.where(m == last_tile, jnp.clip(n - lo, 0, width - 1), 0))

    vmem = (2 * _nbytes((tm, d), F32) + _nbytes((tm + srows, d), BF16) + 2 * _nbytes((d, tn), F32)
            + _nbytes((d, tn), BF16) + 2 * (2 * _nbytes((tm, tn), BF16) + 2 * _nbytes((tm, tn), F32))
            + 4 * _nbytes((tm, half), F32) + 4 * _nbytes((tm, tn), F32) + (4 << 20))

    def shapes(rows):
        return (jax.ShapeDtypeStruct((rows, d), BF16), jax.ShapeDtypeStruct((rows, d), F32),
                jax.ShapeDtypeStruct((rows, 2 * d), BF16), jax.ShapeDtypeStruct((rows, 2 * d), F32))

    def out_specs(rows, by_tile):
        return (pl.BlockSpec((rows, tn), col(0, nq, by_tile)),
                pl.BlockSpec((rows, tn), col(nq, nq, by_tile)),
                pl.BlockSpec((rows, tn), col(2 * nq, 2 * nq, by_tile)),
                pl.BlockSpec((rows, tn), col(4 * nq, 2 * nq, by_tile)))

    full = lambda m, n: (0, 0)
    outs = pl.pallas_call(
        kernel,
        out_shape=shapes(m_rows) + shapes(srows),
        grid=(m_rows // tm, 6 * nq),
        in_specs=[pl.BlockSpec((tm, d), lambda m, n: (m, 0)),
                  pl.BlockSpec((srows, d), full),
                  pl.BlockSpec((1, d), full),
                  _mod_spec(1, d, 0, tiles_per_batch),
                  _mod_spec(1, d, 1, tiles_per_batch),
                  _mod_spec(srows, d, 0, m_rows),
                  _mod_spec(srows, d, 1, m_rows),
                  pl.BlockSpec((None, d, tn), lambda m, n: (layer, 0, n)),
                  pl.BlockSpec((tm, half), lambda m, n: (m % tiles_per_batch, 0)),
                  pl.BlockSpec((tm, half), lambda m, n: (m % tiles_per_batch, 0)),
                  pl.BlockSpec((srows, half), full),
                  pl.BlockSpec((srows, half), full)],
        out_specs=out_specs(tm, True) + out_specs(srows, False),
        scratch_shapes=[pltpu.VMEM((tm + srows, d), BF16)],
        compiler_params=_params(("arbitrary", "arbitrary"), vmem),
        name="ret_in_proj",
    )(x, xs, g, mods_p, mods_p, mods_s, mods_s, w_in, cos, sin, cos_s, sin_s)
    return outs[:4], outs[4:]


def _group_norm_gate(o, gate):
    mu = jnp.mean(o, axis=-1, keepdims=True)
    cen = o - mu
    var = jnp.mean(cen * cen, axis=-1, keepdims=True)
    return _silu(gate) * (cen * lax.rsqrt(var + EPS))


def _ret_chunk_kernel(lg_ref, q_ref, k_ref, v_ref, gt_ref, y_ref, s_ref):
    lg = jnp.full((1, 1), lg_ref[pl.program_id(1)], F32)

    @pl.when(pl.program_id(2) == 0)
    def _():
        s_ref[...] = jnp.zeros_like(s_ref)

    q = q_ref[...]
    k = k_ref[...]
    v = v_ref[...]
    ln = q.shape[0]
    diff = (lax.broadcasted_iota(jnp.int32, (ln, ln), 0)
            - lax.broadcasted_iota(jnp.int32, (ln, ln), 1)).astype(F32)
    dec = jnp.where(diff >= 0, jnp.exp(lg * jnp.maximum(diff, 0.0)), 0.0)
    n = lax.broadcasted_iota(jnp.int32, (ln, 1), 0).astype(F32)
    xi = jnp.exp(lg * (n + 1.0))
    zeta = jnp.exp(lg * (ln - 1.0 - n))

    s0 = s_ref[...]
    qk = lax.dot_general(q, k.astype(BF16), (((1,), (1,)), ((), ())), preferred_element_type=F32)
    a = (qk * dec).astype(BF16)
    o = (jnp.dot(a, v, preferred_element_type=F32)
         + jnp.dot(q, s0.astype(BF16), preferred_element_type=F32) * xi)
    kz = (k * zeta).astype(BF16)
    s_ref[...] = jnp.exp(lg * ln) * s0 + lax.dot_general(
        kz, v, (((0,), (0,)), ((), ())), preferred_element_type=F32)
    y_ref[...] = _group_norm_gate(o, gt_ref[...]).astype(y_ref.dtype)


def _ret_prompt(lg, q, k, v, gate, batch, seq, heads, key_dim, val_dim):
    ln = min(RET_CHUNK, seq)
    nc = seq // ln
    vmem = (2 * (_nbytes((ln, key_dim), BF16) + _nbytes((ln, key_dim), F32) + _nbytes((ln, val_dim), BF16)
                 + _nbytes((ln, val_dim), F32) + _nbytes((ln, val_dim), BF16))
            + 4 * _nbytes((key_dim, val_dim), F32) + 8 * _nbytes((ln, val_dim), F32) + (8 << 20))
    grid_spec = pltpu.PrefetchScalarGridSpec(
        num_scalar_prefetch=1,
        grid=(batch, heads, nc),
        in_specs=[pl.BlockSpec((ln, key_dim), lambda b, h, c, lg: (b * nc + c, h)),
                  pl.BlockSpec((ln, key_dim), lambda b, h, c, lg: (b * nc + c, h)),
                  pl.BlockSpec((ln, val_dim), lambda b, h, c, lg: (b * nc + c, h)),
                  pl.BlockSpec((ln, val_dim), lambda b, h, c, lg: (b * nc + c, h))],
        out_specs=(pl.BlockSpec((ln, val_dim), lambda b, h, c, lg: (b * nc + c, h)),
                   pl.BlockSpec((None, None, key_dim, val_dim), lambda b, h, c, lg: (b, h, 0, 0))))
    return pl.pallas_call(
        _ret_chunk_kernel,
        out_shape=(jax.ShapeDtypeStruct((batch * seq, heads * val_dim), BF16),
                   jax.ShapeDtypeStruct((batch, heads, key_dim, val_dim), F32)),
        grid_spec=grid_spec,
        compiler_params=_params(("arbitrary", "arbitrary", "arbitrary"), vmem),
        name="ret_prompt_chunks",
    )(lg, q, k, v, gate)


def _ret_decode_kernel(lg_ref, qc_ref, kc_ref, qr_ref, kr_ref, v_ref, gt_ref, s0_ref, y_ref, s_ref):
    for h in range(s0_ref.shape[0]):
        gamma = jnp.exp(jnp.full((1, 1), lg_ref[h], F32))
        s0 = s0_ref[h]
        v = v_ref[h]
        a = jnp.sum(qr_ref[h] * kr_ref[h], axis=-1, keepdims=True)
        qs = jnp.sum(qc_ref[h] * s0, axis=0, keepdims=True)
        o = a * v + qs * gamma
        s_ref[h] = gamma * s0 + kc_ref[h] * v
        y_ref[h] = _group_norm_gate(o, gt_ref[h])


def _ret_decode(lg, q, k, v, gate, state):
    bs, heads, key_dim, val_dim = state.shape
    q4 = q.reshape(bs, heads, key_dim, 1)
    k4 = k.reshape(bs, heads, key_dim, 1)
    qr = q.reshape(bs, heads, 1, key_dim)
    kr = k.reshape(bs, heads, 1, key_dim)
    v4 = v.reshape(bs, heads, 1, val_dim)
    g4 = gate.reshape(bs, heads, 1, val_dim)

    def spec(r, c):
        return pl.BlockSpec((None, heads, r, c), lambda b, lg: (b, 0, 0, 0))

    grid_spec = pltpu.PrefetchScalarGridSpec(
        num_scalar_prefetch=1,
        grid=(bs,),
        in_specs=[spec(key_dim, 1), spec(key_dim, 1), spec(1, key_dim), spec(1, key_dim),
                  spec(1, val_dim), spec(1, val_dim), spec(key_dim, val_dim)],
        out_specs=(spec(1, val_dim), spec(key_dim, val_dim)))
    y, s = pl.pallas_call(
        _ret_decode_kernel,
        out_shape=(jax.ShapeDtypeStruct((bs, heads, 1, val_dim), F32),
                   jax.ShapeDtypeStruct(state.shape, F32)),
        grid_spec=grid_spec,
        compiler_params=_params(("arbitrary",), 4 * _nbytes(state.shape[1:], F32) + (16 << 20)),
        name="ret_decode_step",
    )(lg, q4, k4, qr, kr, v4, g4, state)
    return y.reshape(bs, heads * val_dim), s


def _rotary_tables(pos, key_dim):
    half = key_dim // 2
    inv = 1.0 / (ROPE_BASE ** jnp.linspace(0.0, 1.0, half, dtype=F32))
    ang = pos.astype(F32)[:, None] * inv[None, :]
    return jnp.cos(ang), jnp.sin(ang)


def _pad_rows(a, rows):
    return jnp.pad(a, ((0, rows - a.shape[0]), (0, 0)))


def kernel(x_prompt, x_sample, c_prompt, c_sample, cache_fox_k, cache_fox_v, cache_fox_logf, state_ret, page_table, norm1_g, norm2_g, ada_w, ada_b, fox_w_in, fox_b_f, fox_w_out, ret_w_in, ret_w_out, mlp_w_up, mlp_w_down, final_g):
    bp, tp, d = x_prompt.shape
    bs, ts, _ = x_sample.shape
    assert ts == 1, "the decode kernels handle one new token per sequence"
    depth = ada_w.shape[0]
    _, n_pool, page, fox_heads, fox_hd = cache_fox_k.shape
    _, _, ret_heads, key_dim, val_dim = state_ret.shape
    past_len = page_table.shape[1] * page
    srows = -(-bs // BF16_SUBLANES) * BF16_SUBLANES
    tm = min(ROW_TILE, tp)
    tiles_p = tp // tm

    mod_rows = -(-(bs + bp) // 8) * 8
    c_all = _pad_rows(jnp.concatenate([c_sample, c_prompt], axis=0), max(mod_rows, srows))
    mods = _ada(c_all, ada_w, ada_b)

    lg = jnp.log1p(-jnp.exp2(-5.0 - jnp.arange(ret_heads, dtype=F32)))
    cos_p, sin_p = _rotary_tables(jnp.arange(tp), key_dim)
    cos_s, sin_s = _rotary_tables(jnp.full((srows,), past_len), key_dim)

    yp = x_prompt.reshape(bp * tp, d)
    ys = _pad_rows(x_sample.reshape(bs, d), srows)
    fg = final_g.reshape(1, d)
    outs = {name: [] for name in ("kp", "vp", "lfp", "ks", "vs", "lfs", "sp", "ss")}

    for i in range(depth):
        mods_p = mods[i, bs:bs + bp].reshape(bp, 1, 6 * d)
        mods_s = mods[i, :srows].reshape(1, srows, 6 * d)
        g1 = norm1_g[i].reshape(1, d)
        g2 = norm2_g[i].reshape(1, d)
        j = i // 2
        if i % 2 == 0:
            w_f = fox_w_in[j, :, 3 * d:].astype(BF16)
            b_f = fox_b_f[j].reshape(1, fox_heads)
            (q, kf, vf, kb, vb, lf), sample = _fox_proj(
                yp, ys, g1, mods_p, mods_s, tiles_p, fox_w_in, j, w_f, b_f, tm,
                fox_hd ** -0.5 * LOG2E, fox_hd ** -0.5)
            aug = _cumsum(lf, bp, tp, fox_hd)
            o = _fox_attn(q, kb, vb, aug, bp, tp, fox_heads, fox_hd)
            yp = _proj_residual(o, fox_w_out, j, yp, mods_p, 1, tiles_p, 2, tm)
            outs["kp"].append(kf.reshape(bp, tp, fox_heads, fox_hd))
            outs["vp"].append(vf.reshape(bp, tp, fox_heads, fox_hd))
            outs["lfp"].append(lf.reshape(bp, tp, fox_heads))
            q, kf, vf, kb, vb, lf = sample
            o = _fox_decode(
                page_table, j * n_pool,
                q[:bs].reshape(bs, fox_heads, fox_hd), kb[:bs].astype(F32).reshape(bs, fox_heads, fox_hd),
                vb[:bs].astype(F32).reshape(bs, fox_heads, fox_hd),
                jnp.tile(lf[:bs], (1, page)).reshape(bs, 1, page * fox_heads),
                cache_fox_k.reshape(-1, page * fox_heads, fox_hd),
                cache_fox_v.reshape(-1, page * fox_heads, fox_hd),
                cache_fox_logf.reshape(-1, page * fox_heads))
            o = _pad_rows(o.reshape(bs, d), srows).astype(BF16)
            ys = _proj_residual(o, fox_w_out, j, ys, mods_s, srows, 1, 2, srows)
            outs["ks"].append(kf[:bs].reshape(bs, ts, fox_heads, fox_hd))
            outs["vs"].append(vf[:bs].reshape(bs, ts, fox_heads, fox_hd))
            outs["lfs"].append(lf[:bs].reshape(bs, ts, fox_heads))
        else:
            (q, k, v, gate), sample = _ret_proj(yp, ys, g1, mods_p, mods_s, tiles_p, ret_w_in, j,
                                                cos_p, sin_p, cos_s, sin_s, key_dim, tm)
            y, s = _ret_prompt(lg, q, k, v, gate, bp, tp, ret_heads, key_dim, val_dim)
            yp = _proj_residual(y, ret_w_out, j, yp, mods_p, 1, tiles_p, 2, tm)
            outs["sp"].append(s)
            q, k, v, gate = sample
            y, s = _ret_decode(lg, q[:bs].astype(F32), k[:bs], v[:bs].astype(F32), gate[:bs],
                               state_ret[j].astype(F32))
            ys = _proj_residual(_pad_rows(y, srows).astype(BF16), ret_w_out, j, ys, mods_s, srows, 1, 2,
                                srows)
            outs["ss"].append(s)
        yp, ys = _mlp(yp, ys, g2, mods_p, mods_s, tiles_p, mlp_w_up, mlp_w_down, i, fg, i == depth - 1, tm)

    return (yp.reshape(bp, tp, d), ys[:bs].reshape(bs, ts, d),
            jnp.stack(outs["kp"]), jnp.stack(outs["vp"]), jnp.stack(outs["lfp"]),
            jnp.stack(outs["ks"]), jnp.stack(outs["vs"]), jnp.stack(outs["lfs"]),
            jnp.stack(outs["sp"]), jnp.stack(outs["ss"]))
```

```python
import functools

import jax
import jax.numpy as jnp
from jax import lax
from jax.experimental import pallas as pl
from jax.experimental.pallas import tpu as pltpu

F32 = jnp.float32
BF16 = jnp.bfloat16

EPS = 1e-6
ROPE_BASE = 10000.0
LOG2E = 1.4426950408889634
NEG_BIG = -1e30

V7X_VMEM_LIMIT_BYTES = 60000 * 1024
F32_SUBLANES = 8
BF16_SUBLANES = 16

ROW_TILE = 1024
COL_TILE = 512
ATTN_Q_TILE = 1024
ATTN_K_TILE = 1024
ATTN_K_SUB = 512
ATTN_HEADS_PER_STEP = 2
CUMSUM_TILE = 512
RET_CHUNK = 512
PAGES_PER_STEP = 8


def _params(semantics, vmem_bytes):
    return pltpu.CompilerParams(dimension_semantics=semantics,
                                vmem_limit_bytes=int(min(vmem_bytes, V7X_VMEM_LIMIT_BYTES)))


def _nbytes(shape, dtype):
    n = 1
    for s in shape:
        n *= s
    return n * jnp.dtype(dtype).itemsize


def _norm_mod(x, g, shift, scale):
    y = x * lax.rsqrt(jnp.mean(x * x, axis=-1, keepdims=True) + EPS)
    return (y * g) * (1.0 + scale) + shift


def _silu(x):
    return x * jax.nn.sigmoid(x)


def _log_sigmoid(x):
    return jnp.minimum(x, 0.0) - jnp.log1p(jnp.exp(-jnp.abs(x)))


def _split3(x):
    hi = x.astype(BF16)
    r1 = x - hi.astype(F32)
    mid = r1.astype(BF16)
    lo = (r1 - mid.astype(F32)).astype(BF16)
    return hi, mid, lo


def _mod_spec(rows, width, chunk, tiles_per_group):
    return pl.BlockSpec((None, rows, width), lambda m, n: (m // tiles_per_group, 0, chunk))


def _ada_kernel(c_ref, w_ref, b_ref, o_ref):
    o_ref[...] = jnp.dot(_silu(c_ref[...]), w_ref[...], preferred_element_type=F32) + b_ref[...]


def _ada(c_all, ada_w, ada_b):
    depth, d, n6 = ada_w.shape
    rows = c_all.shape[0]
    tn = min(1024, n6)
    vmem = 2 * (_nbytes((d, tn), F32) + _nbytes((rows, tn), F32)) + _nbytes((rows, d), F32) * 2 + (4 << 20)
    return pl.pallas_call(
        _ada_kernel,
        out_shape=jax.ShapeDtypeStruct((depth, rows, n6), F32),
        grid=(depth, n6 // tn),
        in_specs=[pl.BlockSpec((rows, d), lambda i, n: (0, 0)),
                  pl.BlockSpec((None, d, tn), lambda i, n: (i, 0, n)),
                  pl.BlockSpec((None, 1, tn), lambda i, n: (i, 0, n))],
        out_specs=pl.BlockSpec((None, rows, tn), lambda i, n: (i, 0, n)),
        compiler_params=_params(("arbitrary", "arbitrary"), vmem),
        name="ada_modulation",
    )(c_all, ada_w, ada_b.reshape(depth, 1, n6))


def _fox_proj_kernel(x_ref, xs_ref, g_ref, sh_ref, sc_ref, shs_ref, scs_ref, w_ref, wf_ref, bf_ref,
                     q_ref, kf_ref, vf_ref, kb_ref, vb_ref, lf_ref,
                     qs_ref, kfs_ref, vfs_ref, kbs_ref, vbs_ref, lfs_ref, h_ref, *,
                     nq, tm, q_scale, q_scale_s):
    n = pl.program_id(1)

    @pl.when(n == 0)
    def _():
        w_f = wf_ref[...].astype(BF16)
        for x_r, sh_r, sc_r, lf_r, rows in ((x_ref, sh_ref, sc_ref, lf_ref, slice(0, tm)),
                                           (xs_ref, shs_ref, scs_ref, lfs_ref, slice(tm, None))):
            h = _norm_mod(x_r[...], g_ref[...], sh_r[...], sc_r[...]).astype(BF16)
            h_ref[rows, :] = h
            fl = lax.dot_general(h, w_f, (((1,), (1,)), ((), ())), preferred_element_type=F32)
            lf_r[...] = _log_sigmoid(fl + bf_ref[...])

    z = lax.dot_general(h_ref[...], w_ref[...].astype(BF16), (((1,), (1,)), ((), ())),
                        preferred_element_type=F32)
    zp = z[:tm]
    zs = z[tm:]
    last_tile = pl.program_id(0) == pl.num_programs(0) - 1
    in_q = n < nq
    in_k = jnp.logical_and(n >= nq, n < 2 * nq)
    in_v = n >= 2 * nq

    @pl.when(in_q)
    def _():
        q_ref[...] = (zp * q_scale).astype(BF16)

    @pl.when(in_k)
    def _():
        kf_ref[...] = zp
        kb_ref[...] = zp.astype(BF16)

    @pl.when(in_v)
    def _():
        vf_ref[...] = zp
        vb_ref[...] = zp.astype(BF16)

    @pl.when(jnp.logical_and(last_tile, in_q))
    def _():
        qs_ref[...] = (zs * q_scale_s).astype(BF16)

    @pl.when(jnp.logical_and(last_tile, in_k))
    def _():
        kfs_ref[...] = zs
        kbs_ref[...] = zs.astype(BF16)

    @pl.when(jnp.logical_and(last_tile, in_v))
    def _():
        vfs_ref[...] = zs
        vbs_ref[...] = zs.astype(BF16)


def _fox_proj(x, xs, g, mods_p, mods_s, tiles_per_batch, w_in, layer, b_f, tm, q_scale, q_scale_s):
    m_rows, d = x.shape
    srows = xs.shape[0]
    heads = b_f.shape[1]
    assert w_in.shape[2] - 3 * d == heads and (3 * d) % heads == 0
    w_t = jnp.swapaxes(w_in, 1, 2)
    tn = min(COL_TILE, d)
    nq = d // tn
    kernel = functools.partial(_fox_proj_kernel, nq=nq, tm=tm, q_scale=q_scale, q_scale_s=q_scale_s)

    last_tile = m_rows // tm - 1

    def col(lo, by_tile):
        if by_tile:
            return lambda m, n: (m, jnp.clip(n - lo, 0, nq - 1))
        return lambda m, n: (0, jnp.where(m == last_tile, jnp.clip(n - lo, 0, nq - 1), 0))

    vmem = (2 * _nbytes((tm, d), F32) + _nbytes((tm + srows, d), BF16) + 2 * _nbytes((d, tn), F32)
            + _nbytes((d, tn), BF16) + 2 * (3 * _nbytes((tm, tn), BF16) + 2 * _nbytes((tm, tn), F32))
            + 3 * _nbytes((tm, tn), F32) + (4 << 20))

    def shapes(rows):
        f = jax.ShapeDtypeStruct((rows, d), F32)
        b = jax.ShapeDtypeStruct((rows, d), BF16)
        return (b, f, f, b, b, jax.ShapeDtypeStruct((rows, heads), F32))

    def out_specs(rows, by_tile):
        blk = lambda lo: pl.BlockSpec((rows, tn), col(lo, by_tile))
        last = (lambda m, n: (m, 0)) if by_tile else (lambda m, n: (0, 0))
        return (blk(0), blk(nq), blk(2 * nq), blk(nq), blk(2 * nq), pl.BlockSpec((rows, heads), last))

    outs = pl.pallas_call(
        kernel,
        out_shape=shapes(m_rows) + shapes(srows),
        grid=(m_rows // tm, 3 * nq),
        in_specs=[pl.BlockSpec((tm, d), lambda m, n: (m, 0)),
                  pl.BlockSpec((srows, d), lambda m, n: (0, 0)),
                  pl.BlockSpec((1, d), lambda m, n: (0, 0)),
                  _mod_spec(1, d, 0, tiles_per_batch),
                  _mod_spec(1, d, 1, tiles_per_batch),
                  _mod_spec(srows, d, 0, m_rows),
                  _mod_spec(srows, d, 1, m_rows),
                  pl.BlockSpec((None, tn, d), lambda m, n: (layer, n, 0)),
                  pl.BlockSpec((None, heads, d), lambda m, n: (layer, 3 * d // heads, 0)),
                  pl.BlockSpec((1, heads), lambda m, n: (0, 0))],
        out_specs=out_specs(tm, True) + out_specs(srows, False),
        scratch_shapes=[pltpu.VMEM((tm + srows, d), BF16)],
        compiler_params=_params(("arbitrary", "arbitrary"), vmem),
        name="fox_in_proj",
    )(x, xs, g, mods_p, mods_p, mods_s, mods_s, w_t, w_t, b_f)
    return outs[:6], outs[6:]


def _cumsum_kernel(lf_ref, aug_ref, carry_ref, *, hd):
    @pl.when(pl.program_id(1) == 0)
    def _():
        carry_ref[...] = jnp.zeros_like(carry_ref)

    tc, heads = lf_ref.shape
    row = lax.broadcasted_iota(jnp.int32, (tc, tc), 0)
    col = lax.broadcasted_iota(jnp.int32, (tc, tc), 1)
    tri = jnp.where(col <= row, 1.0, 0.0).astype(BF16)
    c = carry_ref[...]
    for piece in _split3(lf_ref[...]):
        c = c + jnp.dot(tri, piece, preferred_element_type=F32)
    carry_ref[...] = c[tc - 1:tc, :]

    row = lax.broadcasted_iota(jnp.int32, (3 * heads, heads * hd), 0)
    lane = lax.broadcasted_iota(jnp.int32, (3 * heads, heads * hd), 1)
    place = jnp.where(lane == (row % heads) * hd + row // heads, 1.0, 0.0).astype(BF16)
    pieces = jnp.concatenate(_split3(c * (-LOG2E)), axis=1)
    aug_ref[...] = jnp.dot(pieces, place, preferred_element_type=F32).astype(BF16)


def _cumsum(logf, batch, seq, hd):
    rows, heads = logf.shape
    tc = min(CUMSUM_TILE, seq)
    per = seq // tc
    return pl.pallas_call(
        functools.partial(_cumsum_kernel, hd=hd),
        out_shape=jax.ShapeDtypeStruct((rows, heads * hd), BF16),
        grid=(batch, per),
        in_specs=[pl.BlockSpec((tc, heads), lambda b, t: (b * per + t, 0))],
        out_specs=pl.BlockSpec((tc, heads * hd), lambda b, t: (b * per + t, 0)),
        scratch_shapes=[pltpu.VMEM((1, heads), F32)],
        compiler_params=_params(("arbitrary", "arbitrary"), 24 << 20),
        name="fox_logf_cumsum",
    )(logf)


def _fox_attn_kernel(q_ref, k_ref, v_ref, aug_ref, o_ref, vt_ref, *, tq, tk, ts, hd, heads_per_step):
    i = pl.program_id(2)
    seq = k_ref.shape[0]
    q0 = pl.multiple_of(i * tq, tq)
    lane_tile = 128

    @pl.when(i == 0)
    def _():
        for t in range(seq // tk):
            vt_ref[:, t * tk:(t + 1) * tk] = v_ref[t * tk:(t + 1) * tk, :].astype(F32).T.astype(BF16)

    ones = jnp.where(lax.broadcasted_iota(jnp.int32, (tq, hd), 1) < 3, 1.0, 0.0).astype(BF16)
    q_ext = [jnp.concatenate([q_ref[:, hh * hd:(hh + 1) * hd], ones], axis=1)
             for hh in range(heads_per_step)]

    def step(j, carry, masked):
        carry = list(carry)
        items = [(sub, hh) for sub in range(tk // ts) for hh in range(heads_per_step)]

        def scores(sub, hh):
            k0 = pl.multiple_of(j * tk + sub * ts, ts)
            lanes = slice(hh * hd, (hh + 1) * hd)
            k_ext = jnp.concatenate([k_ref[pl.ds(k0, ts), lanes], aug_ref[pl.ds(k0, ts), lanes]], axis=1)
            return lax.dot_general(k_ext, q_ext[hh], (((1,), (1,)), ((), ())),
                                   preferred_element_type=F32)

        s_next = scores(*items[0])
        for n, (sub, hh) in enumerate(items):
            s = s_next
            if n + 1 < len(items):
                s_next = scores(*items[n + 1])
            m, l, acc = carry[3 * hh:3 * hh + 3]
            k0 = pl.multiple_of(j * tk + sub * ts, ts)
            m_t, l_t, p_t = [], [], []
            for qt in range(tq // lane_tile):
                ql = slice(qt * lane_tile, (qt + 1) * lane_tile)
                sq = s[:, ql]
                if masked:
                    keep = (k0 + lax.broadcasted_iota(jnp.int32, (ts, lane_tile), 0)
                            <= q0 + qt * lane_tile + lax.broadcasted_iota(jnp.int32, (ts, lane_tile), 1))
                    sq = jnp.where(keep, sq, NEG_BIG)
                mq = jnp.maximum(m[:, ql], jnp.max(sq, axis=0, keepdims=True))
                pq = jnp.exp2(sq - mq)
                l_t.append(jnp.exp2(m[:, ql] - mq) * l[:, ql] + jnp.sum(pq, axis=0, keepdims=True))
                m_t.append(mq)
                p_t.append(pq.astype(BF16))
            m_new = jnp.concatenate(m_t, axis=1)
            alpha = jnp.exp2(m - m_new)
            vt = vt_ref[hh * hd:(hh + 1) * hd, pl.ds(k0, ts)]
            acc = alpha * acc + jnp.dot(vt, jnp.concatenate(p_t, axis=1), preferred_element_type=F32)
            carry[3 * hh:3 * hh + 3] = [m_new, jnp.concatenate(l_t, axis=1), acc]
        return tuple(carry)

    init = (jnp.full((1, tq), NEG_BIG, F32), jnp.zeros((1, tq), F32),
            jnp.zeros((hd, tq), F32)) * heads_per_step
    n_full = q0 // tk
    carry = lax.fori_loop(0, n_full, lambda j, c: step(j, c, False), init)
    carry = step(n_full, carry, True)
    for hh in range(heads_per_step):
        _, l, acc = carry[3 * hh:3 * hh + 3]
        o_ref[:, hh * hd:(hh + 1) * hd] = (acc / l).T.astype(o_ref.dtype)


def _fox_attn(q, k, v, aug, batch, seq, heads, head_dim):
    tq = min(ATTN_Q_TILE, seq)
    tk = min(ATTN_K_TILE, seq)
    nq = seq // tq
    hps = ATTN_HEADS_PER_STEP if heads % ATTN_HEADS_PER_STEP == 0 else 1
    width = hps * head_dim
    kernel = functools.partial(_fox_attn_kernel, tq=tq, tk=tk, ts=min(ATTN_K_SUB, tk), hd=head_dim,
                               heads_per_step=hps)
    vmem = 7 * _nbytes((seq, width), BF16) + 6 * hps * _nbytes((tk, tq), F32) + (8 << 20)
    return pl.pallas_call(
        kernel,
        out_shape=jax.ShapeDtypeStruct(q.shape, BF16),
        grid=(batch, heads // hps, nq),
        in_specs=[pl.BlockSpec((tq, width), lambda b, h, i: (b * nq + i, h)),
                  pl.BlockSpec((seq, width), lambda b, h, i: (b, h)),
                  pl.BlockSpec((seq, width), lambda b, h, i: (b, h)),
                  pl.BlockSpec((seq, width), lambda b, h, i: (b, h))],
        out_specs=pl.BlockSpec((tq, width), lambda b, h, i: (b * nq + i, h)),
        scratch_shapes=[pltpu.VMEM((width, seq), BF16)],
        compiler_params=_params(("arbitrary", "arbitrary", "arbitrary"), vmem),
        name="fox_prompt_attention",
    )(q, k, v, aug)


def _page_suffix_kernel(lf_ref, suf_ref, tot_ref, *, heads):
    x = lf_ref[...]
    cols = x.shape[1]
    col = lax.broadcasted_iota(jnp.int32, x.shape, 1)
    incl = x
    tot = x
    sh = heads
    while sh < cols:
        incl = incl + jnp.where(col + sh < cols, pltpu.roll(incl, cols - sh, 1), 0.0)
        tot = tot + pltpu.roll(tot, sh, 1)
        sh *= 2
    suf_ref[...] = incl - x
    tot_ref[...] = tot


def _page_suffix(cache_lf, heads):
    n_rows, cols = cache_lf.shape
    tr = 256 if n_rows % 256 == 0 else n_rows
    shape = jax.ShapeDtypeStruct((n_rows, cols), F32)
    spec = pl.BlockSpec((tr, cols), lambda r: (r, 0))
    return pl.pallas_call(
        functools.partial(_page_suffix_kernel, heads=heads),
        out_shape=(shape, shape),
        grid=(n_rows // tr,),
        in_specs=[spec],
        out_specs=(spec, spec),
        compiler_params=_params(("arbitrary",), 12 * _nbytes((tr, cols), F32) + (4 << 20)),
        name="fox_page_logf_suffix",
    )(cache_lf)


def _fox_decode_kernel(pt_ref, q_ref, kn_ref, vn_ref, cn_ref, *refs, pages, page_id):
    k_refs = refs[:pages]
    v_refs = refs[pages:2 * pages]
    suf_refs = refs[2 * pages:3 * pages]
    tot_refs = refs[3 * pages:4 * pages]
    o_ref, m_ref, l_ref, acc_ref, run_ref = refs[4 * pages:]
    s_idx = pl.program_id(1)
    heads = q_ref.shape[0]
    cols = k_refs[0].shape[0]
    head_mask = (lax.broadcasted_iota(jnp.int32, (heads, cols), 1) % heads
                 == lax.broadcasted_iota(jnp.int32, (heads, cols), 0))

    @pl.when(s_idx == 0)
    def _():
        m_ref[...] = jnp.full_like(m_ref, NEG_BIG)
        l_ref[...] = jnp.zeros_like(l_ref)
        acc_ref[...] = jnp.zeros_like(acc_ref)
        run_ref[...] = jnp.zeros_like(run_ref)

    q = q_ref[...]
    cn = cn_ref[...]
    run = run_ref[...]
    scores = []
    for g in range(pages):
        s = lax.dot_general(q, k_refs[g][...].astype(BF16), (((1,), (1,)), ((), ())),
                            preferred_element_type=F32)
        row = page_id(pt_ref, pl.program_id(0), s_idx, g) % F32_SUBLANES
        scores.append(jnp.where(head_mask, s + (suf_refs[g][pl.ds(row, 1), :] + (run + cn)), NEG_BIG))
        run = run + tot_refs[g][pl.ds(row, 1), :]
    run_ref[...] = run
    m = m_ref[...]
    m_new = m
    for s in scores:
        m_new = jnp.maximum(m_new, jnp.max(s, axis=-1, keepdims=True))
    alpha = jnp.exp(m - m_new)
    l = alpha * l_ref[...]
    acc = alpha * acc_ref[...]
    for g in range(pages):
        p = jnp.exp(scores[g] - m_new)
        l = l + jnp.sum(p, axis=-1, keepdims=True)
        acc = acc + jnp.dot(p.astype(BF16), v_refs[g][...].astype(BF16), preferred_element_type=F32)
    m_ref[...] = m_new
    l_ref[...] = l
    acc_ref[...] = acc

    @pl.when(s_idx == pl.num_programs(1) - 1)
    def _():
        s_new = jnp.sum(q.astype(F32) * kn_ref[...], axis=-1, keepdims=True)
        m = m_ref[...]
        m_new = jnp.maximum(m, s_new)
        alpha = jnp.exp(m - m_new)
        p_new = jnp.exp(s_new - m_new)
        l = alpha * l_ref[...] + p_new
        acc = alpha * acc_ref[...] + p_new.astype(BF16).astype(F32) * vn_ref[...]
        o_ref[...] = acc / l


def _fox_decode(page_table, pool_base, q, k_new, v_new, logf_new, cache_k, cache_v, cache_lf):
    bs, n_pages = page_table.shape
    _, heads, hd = q.shape
    n_pool, cols, _ = cache_k.shape
    assert n_pool % F32_SUBLANES == 0 and pool_base % F32_SUBLANES == 0
    suffix, total = _page_suffix(cache_lf, heads)
    g_pages = PAGES_PER_STEP if n_pages % PAGES_PER_STEP == 0 else 1
    steps = n_pages // g_pages

    def page_id(pt, b, s, g):
        return pool_base + pt[b * n_pages + (n_pages - 1 - (s * g_pages + g))]

    kernel = functools.partial(_fox_decode_kernel, pages=g_pages, page_id=page_id)

    def page_idx(g):
        return lambda b, s, pt: (page_id(pt, b, s, g), 0, 0)

    def row_block_idx(g):
        return lambda b, s, pt: (page_id(pt, b, s, g) // F32_SUBLANES, 0)

    tok = pl.BlockSpec((None, heads, hd), lambda b, s, pt: (b, 0, 0))
    in_specs = [tok, tok, tok, pl.BlockSpec((None, 1, cols), lambda b, s, pt: (b, 0, 0))]
    in_specs += [pl.BlockSpec((None, cols, hd), page_idx(g)) for g in range(g_pages)]
    in_specs += [pl.BlockSpec((None, cols, hd), page_idx(g)) for g in range(g_pages)]
    in_specs += [pl.BlockSpec((F32_SUBLANES, cols), row_block_idx(g)) for g in range(g_pages)]
    in_specs += [pl.BlockSpec((F32_SUBLANES, cols), row_block_idx(g)) for g in range(g_pages)]
    vmem = (4 * g_pages * _nbytes((cols, hd), F32) + 2 * g_pages * _nbytes((cols, hd), BF16)
            + 4 * g_pages * _nbytes((heads, cols), F32) + (4 << 20))
    grid_spec = pltpu.PrefetchScalarGridSpec(
        num_scalar_prefetch=1,
        grid=(bs, steps),
        in_specs=in_specs,
        out_specs=pl.BlockSpec((None, heads, hd), lambda b, s, pt: (b, 0, 0)),
        scratch_shapes=[pltpu.VMEM((heads, 1), F32), pltpu.VMEM((heads, 1), F32),
                        pltpu.VMEM((heads, hd), F32), pltpu.VMEM((1, cols), F32)])
    return pl.pallas_call(
        kernel,
        out_shape=jax.ShapeDtypeStruct((bs, heads, hd), F32),
        grid_spec=grid_spec,
        compiler_params=_params(("arbitrary", "arbitrary"), vmem),
        name="fox_decode_attention",
    )(page_table.reshape(-1), q, k_new, v_new, logf_new,
      *([cache_k] * g_pages), *([cache_v] * g_pages), *([suffix] * g_pages), *([total] * g_pages))


def _proj_residual_kernel(a_ref, w_ref, x_ref, gate_ref, o_ref):
    z = jnp.dot(a_ref[...], w_ref[...].astype(BF16), preferred_element_type=F32)
    o_ref[...] = x_ref[...] + gate_ref[...] * z


def _proj_residual(a, w, layer, x, mods, mod_rows, tiles_per_group, gate_chunk, tm):
    m_rows, k = a.shape
    d = w.shape[2]
    tn = min(COL_TILE, d)
    vmem = (2 * _nbytes((tm, k), BF16) + 2 * _nbytes((k, tn), F32) + _nbytes((k, tn), BF16)
            + 5 * _nbytes((tm, tn), F32) + (4 << 20))
    return pl.pallas_call(
        _proj_residual_kernel,
        out_shape=jax.ShapeDtypeStruct((m_rows, d), F32),
        grid=(m_rows // tm, d // tn),
        in_specs=[pl.BlockSpec((tm, k), lambda m, n: (m, 0)),
                  pl.BlockSpec((None, k, tn), lambda m, n: (layer, 0, n)),
                  pl.BlockSpec((tm, tn), lambda m, n: (m, n)),
                  pl.BlockSpec((None, mod_rows, tn),
                               lambda m, n: (m // tiles_per_group, 0, gate_chunk * (d // tn) + n))],
        out_specs=pl.BlockSpec((tm, tn), lambda m, n: (m, n)),
        compiler_params=_params(("arbitrary", "arbitrary"), vmem),
        name="mixer_out_proj",
    )(a, w, x, mods)


def _mlp_kernel(x_ref, xs_ref, g_ref, sh_ref, sc_ref, gate_ref, shs_ref, scs_ref, gates_ref,
                wu_ref, wd_ref, fg_ref, o_ref, os_ref, h_ref, *, tm, final_norm):
    f = pl.program_id(1)

    @pl.when(f == 0)
    def _():
        h_ref[:tm, :] = _norm_mod(x_ref[...], g_ref[...], sh_ref[...], sc_ref[...]).astype(BF16)
        h_ref[tm:, :] = _norm_mod(xs_ref[...], g_ref[...], shs_ref[...], scs_ref[...]).astype(BF16)
        o_ref[...] = jnp.zeros_like(o_ref)
        os_ref[...] = jnp.zeros_like(os_ref)

    u = jnp.dot(h_ref[...], wu_ref[...].astype(BF16), preferred_element_type=F32)
    u = jnp.square(jnp.maximum(u, 0.0)).astype(BF16)
    z = jnp.dot(u, wd_ref[...].astype(BF16), preferred_element_type=F32)
    o_ref[...] += z[:tm]
    os_ref[...] += z[tm:]

    @pl.when(f == pl.num_programs(1) - 1)
    def _():
        for x_r, gate_r, o_r in ((x_ref, gate_ref, o_ref), (xs_ref, gates_ref, os_ref)):
            y = x_r[...] + gate_r[...] * o_r[...]
            if final_norm:
                y = y * lax.rsqrt(jnp.mean(y * y, axis=-1, keepdims=True) + EPS) * fg_ref[...]
            o_r[...] = y


def _mlp(x, xs, g, mods_p, mods_s, tiles_per_batch, w_up, w_down, layer, final_g, final_norm, tm):
    m_rows, d = x.shape
    srows = xs.shape[0]
    d_ff = w_up.shape[2]
    tf = min(COL_TILE, d_ff)
    kernel = functools.partial(_mlp_kernel, tm=tm, final_norm=final_norm)
    vmem = (3 * _nbytes((tm, d), F32) + _nbytes((tm + srows, d), BF16) + 4 * _nbytes((d, tf), F32)
            + 2 * _nbytes((d, tf), BF16) + 3 * _nbytes((tm, tf), F32) + (4 << 20))
    full = lambda m, f: (0, 0)
    vec = pl.BlockSpec((1, d), full)
    return pl.pallas_call(
        kernel,
        out_shape=(jax.ShapeDtypeStruct((m_rows, d), F32), jax.ShapeDtypeStruct((srows, d), F32)),
        grid=(m_rows // tm, d_ff // tf),
        in_specs=[pl.BlockSpec((tm, d), lambda m, f: (m, 0), pipeline_mode=pl.Buffered(1)),
                  pl.BlockSpec((srows, d), full),
                  vec,
                  _mod_spec(1, d, 3, tiles_per_batch),
                  _mod_spec(1, d, 4, tiles_per_batch),
                  _mod_spec(1, d, 5, tiles_per_batch),
                  _mod_spec(srows, d, 3, m_rows),
                  _mod_spec(srows, d, 4, m_rows),
                  _mod_spec(srows, d, 5, m_rows),
                  pl.BlockSpec((None, d, tf), lambda m, f: (layer, 0, f)),
                  pl.BlockSpec((None, tf, d), lambda m, f: (layer, f, 0)),
                  vec],
        out_specs=(pl.BlockSpec((tm, d), lambda m, f: (m, 0)), pl.BlockSpec((srows, d), full)),
        scratch_shapes=[pltpu.VMEM((tm + srows, d), BF16)],
        compiler_params=_params(("arbitrary", "arbitrary"), vmem),
        name="relu2_mlp",
    )(x, xs, g, mods_p, mods_p, mods_p, mods_s, mods_s, mods_s, w_up, w_down, final_g)


def _ret_proj_kernel(x_ref, xs_ref, g_ref, sh_ref, sc_ref, shs_ref, scs_ref, w_ref,
                     cos_ref, sin_ref, coss_ref, sins_ref,
                     q_ref, k_ref, v_ref, gt_ref, qs_ref, ks_ref, vs_ref, gts_ref, h_ref, *,
                     nq, tm, key_dim, k_scale):
    n = pl.program_id(1)

    @pl.when(n == 0)
    def _():
        h_ref[:tm, :] = _norm_mod(x_ref[...], g_ref[...], sh_ref[...], sc_ref[...]).astype(BF16)
        h_ref[tm:, :] = _norm_mod(xs_ref[...], g_ref[...], shs_ref[...], scs_ref[...]).astype(BF16)

    z = jnp.dot(h_ref[...], w_ref[...].astype(BF16), preferred_element_type=F32)
    zp = z[:tm]
    zs = z[tm:]
    half = key_dim // 2

    def rotate(zz, c_ref, s_ref):
        cos = c_ref[...]
        sin = s_ref[...]
        parts = []
        for hh in range(zz.shape[1] // key_dim):
            x1 = zz[:, hh * key_dim:hh * key_dim + half]
            x2 = zz[:, hh * key_dim + half:(hh + 1) * key_dim]
            parts += [x1 * cos - x2 * sin, x1 * sin + x2 * cos]
        return jnp.concatenate(parts, axis=1)

    last_tile = pl.program_id(0) == pl.num_programs(0) - 1
    in_q = n < nq
    in_k = jnp.logical_and(n >= nq, n < 2 * nq)
    in_v = jnp.logical_and(n >= 2 * nq, n < 4 * nq)
    in_g = n >= 4 * nq

    @pl.when(in_q)
    def _():
        q_ref[...] = rotate(zp, cos_ref, sin_ref).astype(BF16)

    @pl.when(in_k)
    def _():
        k_ref[...] = rotate(zp, cos_ref, sin_ref) * k_scale

    @pl.when(in_v)
    def _():
        v_ref[...] = zp.astype(BF16)

    @pl.when(in_g)
    def _():
        gt_ref[...] = zp

    @pl.when(jnp.logical_and(last_tile, in_q))
    def _():
        qs_ref[...] = rotate(zs, coss_ref, sins_ref).astype(BF16)

    @pl.when(jnp.logical_and(last_tile, in_k))
    def _():
        ks_ref[...] = rotate(zs, coss_ref, sins_ref) * k_scale

    @pl.when(jnp.logical_and(last_tile, in_v))
    def _():
        vs_ref[...] = zs.astype(BF16)

    @pl.when(jnp.logical_and(last_tile, in_g))
    def _():
        gts_ref[...] = zs


def _ret_proj(x, xs, g, mods_p, mods_s, tiles_per_batch, w_in, layer, cos, sin, cos_s, sin_s, key_dim, tm):
    m_rows, d = x.shape
    srows = xs.shape[0]
    tn = min(COL_TILE, d)
    nq = d // tn
    half = key_dim // 2
    kernel = functools.partial(_ret_proj_kernel, nq=nq, tm=tm, key_dim=key_dim, k_scale=key_dim ** -0.5)

    last_tile = m_rows // tm - 1

    def col(lo, width, by_tile):
        if by_tile:
            return lambda m, n: (m, jnp.clip(n - lo, 0, width - 1))
        return lambda m, n: (0, jnp.where(m == last_tile, jnp.clip(n - lo, 0, width - 1), 0))

    vmem = (2 * _nbytes((tm, d), F32) + _nbytes((tm + srows, d), BF16) + 2 * _nbytes((d, tn), F32)
            + _nbytes((d, tn), BF16) + 2 * (2 * _nbytes((tm, tn), BF16) + 2 * _nbytes((tm, tn), F32))
            + 4 * _nbytes((tm, half), F32) + 4 * _nbytes((tm, tn), F32) + (4 << 20))

    def shapes(rows):
        return (jax.ShapeDtypeStruct((rows, d), BF16), jax.ShapeDtypeStruct((rows, d), F32),
                jax.ShapeDtypeStruct((rows, 2 * d), BF16), jax.ShapeDtypeStruct((rows, 2 * d), F32))

    def out_specs(rows, by_tile):
        return (pl.BlockSpec((rows, tn), col(0, nq, by_tile)),
                pl.BlockSpec((rows, tn), col(nq, nq, by_tile)),
                pl.BlockSpec((rows, tn), col(2 * nq, 2 * nq, by_tile)),
                pl.BlockSpec((rows, tn), col(4 * nq, 2 * nq, by_tile)))

    full = lambda m, n: (0, 0)
    outs = pl.pallas_call(
        kernel,
        out_shape=shapes(m_rows) + shapes(srows),
        grid=(m_rows // tm, 6 * nq),
        in_specs=[pl.BlockSpec((tm, d), lambda m, n: (m, 0)),
                  pl.BlockSpec((srows, d), full),
                  pl.BlockSpec((1, d), full),
                  _mod_spec(1, d, 0, tiles_per_batch),
                  _mod_spec(1, d, 1, tiles_per_batch),
                  _mod_spec(srows, d, 0, m_rows),
                  _mod_spec(srows, d, 1, m_rows),
                  pl.BlockSpec((None, d, tn), lambda m, n: (layer, 0, n)),
                  pl.BlockSpec((tm, half), lambda m, n: (m % tiles_per_batch, 0)),
                  pl.BlockSpec((tm, half), lambda m, n: (m % tiles_per_batch, 0)),
                  pl.BlockSpec((srows, half), full),
                  pl.BlockSpec((srows, half), full)],
        out_specs=out_specs(tm, True) + out_specs(srows, False),
        scratch_shapes=[pltpu.VMEM((tm + srows, d), BF16)],
        compiler_params=_params(("arbitrary", "arbitrary"), vmem),
        name="ret_in_proj",
    )(x, xs, g, mods_p, mods_p, mods_s, mods_s, w_in, cos, sin, cos_s, sin_s)
    return outs[:4], outs[4:]


def _group_norm_gate(o, gate):
    mu = jnp.mean(o, axis=-1, keepdims=True)
    cen = o - mu
    var = jnp.mean(cen * cen, axis=-1, keepdims=True)
    return _silu(gate) * (cen * lax.rsqrt(var + EPS))


def _ret_chunk_kernel(lg_ref, q_ref, k_ref, v_ref, gt_ref, y_ref, s_ref):
    lg = jnp.full((1, 1), lg_ref[pl.program_id(1)], F32)

    @pl.when(pl.program_id(2) == 0)
    def _():
        s_ref[...] = jnp.zeros_like(s_ref)

    q = q_ref[...]
    k = k_ref[...]
    v = v_ref[...]
    ln = q.shape[0]
    diff = (lax.broadcasted_iota(jnp.int32, (ln, ln), 0)
            - lax.broadcasted_iota(jnp.int32, (ln, ln), 1)).astype(F32)
    dec = jnp.where(diff >= 0, jnp.exp(lg * jnp.maximum(diff, 0.0)), 0.0)
    n = lax.broadcasted_iota(jnp.int32, (ln, 1), 0).astype(F32)
    xi = jnp.exp(lg * (n + 1.0))
    zeta = jnp.exp(lg * (ln - 1.0 - n))

    s0 = s_ref[...]
    qk = lax.dot_general(q, k.astype(BF16), (((1,), (1,)), ((), ())), preferred_element_type=F32)
    a = (qk * dec).astype(BF16)
    o = (jnp.dot(a, v, preferred_element_type=F32)
         + jnp.dot(q, s0.astype(BF16), preferred_element_type=F32) * xi)
    kz = (k * zeta).astype(BF16)
    s_ref[...] = jnp.exp(lg * ln) * s0 + lax.dot_general(
        kz, v, (((0,), (0,)), ((), ())), preferred_element_type=F32)
    y_ref[...] = _group_norm_gate(o, gt_ref[...]).astype(y_ref.dtype)


def _ret_prompt(lg, q, k, v, gate, batch, seq, heads, key_dim, val_dim):
    ln = min(RET_CHUNK, seq)
    nc = seq // ln
    vmem = (2 * (_nbytes((ln, key_dim), BF16) + _nbytes((ln, key_dim), F32) + _nbytes((ln, val_dim), BF16)
                 + _nbytes((ln, val_dim), F32) + _nbytes((ln, val_dim), BF16))
            + 4 * _nbytes((key_dim, val_dim), F32) + 8 * _nbytes((ln, val_dim), F32) + (8 << 20))
    grid_spec = pltpu.PrefetchScalarGridSpec(
        num_scalar_prefetch=1,
        grid=(batch, heads, nc),
        in_specs=[pl.BlockSpec((ln, key_dim), lambda b, h, c, lg: (b * nc + c, h)),
                  pl.BlockSpec((ln, key_dim), lambda b, h, c, lg: (b * nc + c, h)),
                  pl.BlockSpec((ln, val_dim), lambda b, h, c, lg: (b * nc + c, h)),
                  pl.BlockSpec((ln, val_dim), lambda b, h, c, lg: (b * nc + c, h))],
        out_specs=(pl.BlockSpec((ln, val_dim), lambda b, h, c, lg: (b * nc + c, h)),
                   pl.BlockSpec((None, None, key_dim, val_dim), lambda b, h, c, lg: (b, h, 0, 0))))
    return pl.pallas_call(
        _ret_chunk_kernel,
        out_shape=(jax.ShapeDtypeStruct((batch * seq, heads * val_dim), BF16),
                   jax.ShapeDtypeStruct((batch, heads, key_dim, val_dim), F32)),
        grid_spec=grid_spec,
        compiler_params=_params(("arbitrary", "arbitrary", "arbitrary"), vmem),
        name="ret_prompt_chunks",
    )(lg, q, k, v, gate)


def _ret_decode_kernel(lg_ref, q_ref, k_ref, v_ref, gt_ref, s0_ref, y_ref, s_ref):
    q = q_ref[...]
    k = k_ref[...]
    q_cols = q.T
    k_cols = k.T
    a_all = jnp.sum(q * k, axis=-1, keepdims=True)
    for h in range(s0_ref.shape[0]):
        gamma = jnp.exp(jnp.full((1, 1), lg_ref[h], F32))
        s0 = s0_ref[h]
        v = v_ref[h:h + 1, :]
        qs = jnp.sum(q_cols[:, h:h + 1] * s0, axis=0, keepdims=True)
        o = a_all[h:h + 1, :] * v + qs * gamma
        s_ref[h] = gamma * s0 + k_cols[:, h:h + 1] * v
        y_ref[h:h + 1, :] = _group_norm_gate(o, gt_ref[h:h + 1, :])


def _ret_decode(lg, q, k, v, gate, state):
    bs, heads, key_dim, val_dim = state.shape

    def vec(width):
        return pl.BlockSpec((None, heads, width), lambda b, lg: (b, 0, 0))

    state_spec = pl.BlockSpec((None, heads, key_dim, val_dim), lambda b, lg: (b, 0, 0, 0))
    grid_spec = pltpu.PrefetchScalarGridSpec(
        num_scalar_prefetch=1,
        grid=(bs,),
        in_specs=[vec(key_dim), vec(key_dim), vec(val_dim), vec(val_dim), state_spec],
        out_specs=(vec(val_dim), state_spec))
    y, s = pl.pallas_call(
        _ret_decode_kernel,
        out_shape=(jax.ShapeDtypeStruct((bs, heads, val_dim), F32),
                   jax.ShapeDtypeStruct(state.shape, F32)),
        grid_spec=grid_spec,
        compiler_params=_params(("arbitrary",), 4 * _nbytes(state.shape[1:], F32) + (16 << 20)),
        name="ret_decode_step",
    )(lg, q.reshape(bs, heads, key_dim), k.reshape(bs, heads, key_dim),
      v.reshape(bs, heads, val_dim), gate.reshape(bs, heads, val_dim), state)
    return y.reshape(bs, heads * val_dim), s


def _rotary_tables(pos, key_dim):
    half = key_dim // 2
    inv = 1.0 / (ROPE_BASE ** jnp.linspace(0.0, 1.0, half, dtype=F32))
    ang = pos.astype(F32)[:, None] * inv[None, :]
    return jnp.cos(ang), jnp.sin(ang)


def _pad_rows(a, rows):
    return jnp.pad(a, ((0, rows - a.shape[0]), (0, 0)))


def kernel(x_prompt, x_sample, c_prompt, c_sample, cache_fox_k, cache_fox_v, cache_fox_logf, state_ret, page_table, norm1_g, norm2_g, ada_w, ada_b, fox_w_in, fox_b_f, fox_w_out, ret_w_in, ret_w_out, mlp_w_up, mlp_w_down, final_g):
    bp, tp, d = x_prompt.shape
    bs, ts, _ = x_sample.shape
    assert ts == 1, "the decode kernels handle one new token per sequence"
    depth = ada_w.shape[0]
    _, n_pool, page, fox_heads, fox_hd = cache_fox_k.shape
    _, _, ret_heads, key_dim, val_dim = state_ret.shape
    past_len = page_table.shape[1] * page
    srows = -(-bs // BF16_SUBLANES) * BF16_SUBLANES
    tm = min(ROW_TILE, tp)
    tiles_p = tp // tm

    mod_rows = -(-(bs + bp) // 8) * 8
    c_all = _pad_rows(jnp.concatenate([c_sample, c_prompt], axis=0), max(mod_rows, srows))
    mods = _ada(c_all, ada_w, ada_b)

    lg = jnp.log1p(-jnp.exp2(-5.0 - jnp.arange(ret_heads, dtype=F32)))
    cos_p, sin_p = _rotary_tables(jnp.arange(tp), key_dim)
    cos_s, sin_s = _rotary_tables(jnp.full((srows,), past_len), key_dim)

    yp = x_prompt.reshape(bp * tp, d)
    ys = _pad_rows(x_sample.reshape(bs, d), srows)
    fg = final_g.reshape(1, d)
    outs = {name: [] for name in ("kp", "vp", "lfp", "ks", "vs", "lfs", "sp", "ss")}

    for i in range(depth):
        mods_p = mods[i, bs:bs + bp].reshape(bp, 1, 6 * d)
        mods_s = mods[i, :srows].reshape(1, srows, 6 * d)
        g1 = norm1_g[i].reshape(1, d)
        g2 = norm2_g[i].reshape(1, d)
        j = i // 2
        if i % 2 == 0:
            b_f = fox_b_f[j].reshape(1, fox_heads)
            (q, kf, vf, kb, vb, lf), sample = _fox_proj(
                yp, ys, g1, mods_p, mods_s, tiles_p, fox_w_in, j, b_f, tm,
                fox_hd ** -0.5 * LOG2E, fox_hd ** -0.5)
            aug = _cumsum(lf, bp, tp, fox_hd)
            o = _fox_attn(q, kb, vb, aug, bp, tp, fox_heads, fox_hd)
            yp = _proj_residual(o, fox_w_out, j, yp, mods_p, 1, tiles_p, 2, tm)
            outs["kp"].append(kf.reshape(bp, tp, fox_heads, fox_hd))
            outs["vp"].append(vf.reshape(bp, tp, fox_heads, fox_hd))
            outs["lfp"].append(lf.reshape(bp, tp, fox_heads))
            q, kf, vf, kb, vb, lf = sample
            o = _fox_decode(
                page_table, j * n_pool,
                q[:bs].reshape(bs, fox_heads, fox_hd), kb[:bs].astype(F32).reshape(bs, fox_heads, fox_hd),
                vb[:bs].astype(F32).reshape(bs, fox_heads, fox_hd),
                jnp.tile(lf[:bs], (1, page)).reshape(bs, 1, page * fox_heads),
                cache_fox_k.reshape(-1, page * fox_heads, fox_hd),
                cache_fox_v.reshape(-1, page * fox_heads, fox_hd),
                cache_fox_logf.reshape(-1, page * fox_heads))
            o = _pad_rows(o.reshape(bs, d), srows).astype(BF16)
            ys = _proj_residual(o, fox_w_out, j, ys, mods_s, srows, 1, 2, srows)
            outs["ks"].append(kf[:bs].reshape(bs, ts, fox_heads, fox_hd))
            outs["vs"].append(vf[:bs].reshape(bs, ts, fox_heads, fox_hd))
            outs["lfs"].append(lf[:bs].reshape(bs, ts, fox_heads))
        else:
            (q, k, v, gate), sample = _ret_proj(yp, ys, g1, mods_p, mods_s, tiles_p, ret_w_in, j,
                                                cos_p, sin_p, cos_s, sin_s, key_dim, tm)
            y, s = _ret_prompt(lg, q, k, v, gate, bp, tp, ret_heads, key_dim, val_dim)
            yp = _proj_residual(y, ret_w_out, j, yp, mods_p, 1, tiles_p, 2, tm)
            outs["sp"].append(s)
            q, k, v, gate = sample
            y, s = _ret_decode(lg, q[:bs].astype(F32), k[:bs], v[:bs].astype(F32), gate[:bs],
                               state_ret[j].astype(F32))
            ys = _proj_residual(_pad_rows(y, srows).astype(BF16), ret_w_out, j, ys, mods_s, srows, 1, 2,
                                srows)
            outs["ss"].append(s)
        yp, ys = _mlp(yp, ys, g2, mods_p, mods_s, tiles_p, mlp_w_up, mlp_w_down, i, fg, i == depth - 1, tm)

    return (yp.reshape(bp, tp, d), ys[:bs].reshape(bs, ts, d),
            jnp.stack(outs["kp"]), jnp.stack(outs["vp"]), jnp.stack(outs["lfp"]),
            jnp.stack(outs["ks"]), jnp.stack(outs["vs"]), jnp.stack(outs["lfs"]),
            jnp.stack(outs["sp"]), jnp.stack(outs["ss"]))
```

```python
import functools

import jax
import jax.numpy as jnp
from jax import lax
from jax.experimental import pallas as pl
from jax.experimental.pallas import tpu as pltpu

F32 = jnp.float32
BF16 = jnp.bfloat16

EPS = 1e-6
ROPE_BASE = 10000.0
LOG2E = 1.4426950408889634
NEG_BIG = -1e30

V7X_VMEM_LIMIT_BYTES = 60000 * 1024
F32_SUBLANES = 8
BF16_SUBLANES = 16

ROW_TILE = 1024
COL_TILE = 512
ATTN_Q_TILE = 1024
ATTN_K_TILE = 1024
ATTN_K_SUB = 512
ATTN_HEADS_PER_STEP = 2
CUMSUM_TILE = 512
RET_CHUNK = 512
PAGES_PER_STEP = 8


def _params(semantics, vmem_bytes):
    return pltpu.CompilerParams(dimension_semantics=semantics,
                                vmem_limit_bytes=int(min(vmem_bytes, V7X_VMEM_LIMIT_BYTES)))


def _nbytes(shape, dtype):
    n = 1
    for s in shape:
        n *= s
    return n * jnp.dtype(dtype).itemsize


def _norm_mod(x, g, shift, scale):
    y = x * lax.rsqrt(jnp.mean(x * x, axis=-1, keepdims=True) + EPS)
    return (y * g) * (1.0 + scale) + shift


def _silu(x):
    return x * jax.nn.sigmoid(x)


def _log_sigmoid(x):
    return jnp.minimum(x, 0.0) - jnp.log1p(jnp.exp(-jnp.abs(x)))


def _split3(x):
    hi = x.astype(BF16)
    r1 = x - hi.astype(F32)
    mid = r1.astype(BF16)
    lo = (r1 - mid.astype(F32)).astype(BF16)
    return hi, mid, lo


class _WeightRounding:
    def __init__(self, weights, n_steps, step_index):
        self.items = []
        for w, layer in weights:
            slab = w.shape[1] // n_steps
            assert w.shape[1] % n_steps == 0 and slab % BF16_SUBLANES == 0, (w.shape, n_steps)
            self.items.append((w, layer, slab))
        self.step_index = step_index

    def __len__(self):
        return len(self.items)

    def in_specs(self):
        return [pl.BlockSpec((None, slab, w.shape[2]),
                             lambda *ids, layer=layer: (layer, self.step_index(*ids), 0))
                for w, layer, slab in self.items]

    def out_specs(self):
        return [pl.BlockSpec((None, slab, w.shape[2]), lambda *ids: (0, self.step_index(*ids), 0))
                for w, _, slab in self.items]

    def out_shapes(self):
        return [jax.ShapeDtypeStruct((1,) + w.shape[1:], BF16) for w, _, _ in self.items]

    def operands(self):
        return [w for w, _, _ in self.items]

    def vmem_bytes(self):
        return sum(2 * _nbytes((slab, w.shape[2]), F32) + 2 * _nbytes((slab, w.shape[2]), BF16)
                   for w, _, slab in self.items)

    @staticmethod
    def run(in_refs, out_refs):
        for i_ref, o_ref in zip(in_refs, out_refs):
            o_ref[...] = i_ref[...].astype(BF16)


def _mod_spec(rows, width, chunk, tiles_per_group):
    return pl.BlockSpec((None, rows, width), lambda m, n: (m // tiles_per_group, 0, chunk))


def _ada_kernel(c_ref, w_ref, b_ref, o_ref):
    o_ref[...] = jnp.dot(_silu(c_ref[...]), w_ref[...], preferred_element_type=F32) + b_ref[...]


def _ada(c_all, ada_w, ada_b):
    depth, d, n6 = ada_w.shape
    rows = c_all.shape[0]
    tn = min(1024, n6)
    vmem = 2 * (_nbytes((d, tn), F32) + _nbytes((rows, tn), F32)) + _nbytes((rows, d), F32) * 2 + (4 << 20)
    return pl.pallas_call(
        _ada_kernel,
        out_shape=jax.ShapeDtypeStruct((depth, rows, n6), F32),
        grid=(depth, n6 // tn),
        in_specs=[pl.BlockSpec((rows, d), lambda i, n: (0, 0)),
                  pl.BlockSpec((None, d, tn), lambda i, n: (i, 0, n)),
                  pl.BlockSpec((None, 1, tn), lambda i, n: (i, 0, n))],
        out_specs=pl.BlockSpec((None, rows, tn), lambda i, n: (i, 0, n)),
        compiler_params=_params(("arbitrary", "arbitrary"), vmem),
        name="ada_modulation",
    )(c_all, ada_w, ada_b.reshape(depth, 1, n6))


def _fox_proj_kernel(x_ref, xs_ref, g_ref, sh_ref, sc_ref, shs_ref, scs_ref, w_ref, wf_ref, bf_ref,
                     q_ref, kf_ref, vf_ref, kb_ref, vb_ref, lf_ref,
                     qs_ref, kfs_ref, vfs_ref, kbs_ref, vbs_ref, lfs_ref, h_ref, *,
                     nq, tm, q_scale, q_scale_s):
    n = pl.program_id(1)

    @pl.when(n == 0)
    def _():
        w_f = wf_ref[...].astype(BF16)
        for x_r, sh_r, sc_r, lf_r, rows in ((x_ref, sh_ref, sc_ref, lf_ref, slice(0, tm)),
                                           (xs_ref, shs_ref, scs_ref, lfs_ref, slice(tm, None))):
            h = _norm_mod(x_r[...], g_ref[...], sh_r[...], sc_r[...]).astype(BF16)
            h_ref[rows, :] = h
            fl = lax.dot_general(h, w_f, (((1,), (1,)), ((), ())), preferred_element_type=F32)
            lf_r[...] = _log_sigmoid(fl + bf_ref[...])

    z = lax.dot_general(h_ref[...], w_ref[...].astype(BF16), (((1,), (1,)), ((), ())),
                        preferred_element_type=F32)
    zp = z[:tm]
    zs = z[tm:]
    last_tile = pl.program_id(0) == pl.num_programs(0) - 1
    in_q = n < nq
    in_k = jnp.logical_and(n >= nq, n < 2 * nq)
    in_v = n >= 2 * nq

    @pl.when(in_q)
    def _():
        q_ref[...] = (zp * q_scale).astype(BF16)

    @pl.when(in_k)
    def _():
        kf_ref[...] = zp
        kb_ref[...] = zp.astype(BF16)

    @pl.when(in_v)
    def _():
        vf_ref[...] = zp
        vb_ref[...] = zp.astype(BF16)

    @pl.when(jnp.logical_and(last_tile, in_q))
    def _():
        qs_ref[...] = (zs * q_scale_s).astype(BF16)

    @pl.when(jnp.logical_and(last_tile, in_k))
    def _():
        kfs_ref[...] = zs
        kbs_ref[...] = zs.astype(BF16)

    @pl.when(jnp.logical_and(last_tile, in_v))
    def _():
        vfs_ref[...] = zs
        vbs_ref[...] = zs.astype(BF16)


def _fox_proj(x, xs, g, mods_p, mods_s, tiles_per_batch, w_in, layer, b_f, tm, q_scale, q_scale_s):
    m_rows, d = x.shape
    srows = xs.shape[0]
    heads = b_f.shape[1]
    assert w_in.shape[2] - 3 * d == heads and (3 * d) % heads == 0
    w_t = jnp.swapaxes(w_in, 1, 2)
    tn = min(COL_TILE, d)
    nq = d // tn
    kernel = functools.partial(_fox_proj_kernel, nq=nq, tm=tm, q_scale=q_scale, q_scale_s=q_scale_s)

    last_tile = m_rows // tm - 1

    def col(lo, by_tile):
        if by_tile:
            return lambda m, n: (m, jnp.clip(n - lo, 0, nq - 1))
        return lambda m, n: (0, jnp.where(m == last_tile, jnp.clip(n - lo, 0, nq - 1), 0))

    vmem = (2 * _nbytes((tm, d), F32) + _nbytes((tm + srows, d), BF16) + 2 * _nbytes((d, tn), F32)
            + _nbytes((d, tn), BF16) + 2 * (3 * _nbytes((tm, tn), BF16) + 2 * _nbytes((tm, tn), F32))
            + 3 * _nbytes((tm, tn), F32) + (4 << 20))

    def shapes(rows):
        f = jax.ShapeDtypeStruct((rows, d), F32)
        b = jax.ShapeDtypeStruct((rows, d), BF16)
        return (b, f, f, b, b, jax.ShapeDtypeStruct((rows, heads), F32))

    def out_specs(rows, by_tile):
        blk = lambda lo: pl.BlockSpec((rows, tn), col(lo, by_tile))
        last = (lambda m, n: (m, 0)) if by_tile else (lambda m, n: (0, 0))
        return (blk(0), blk(nq), blk(2 * nq), blk(nq), blk(2 * nq), pl.BlockSpec((rows, heads), last))

    outs = pl.pallas_call(
        kernel,
        out_shape=shapes(m_rows) + shapes(srows),
        grid=(m_rows // tm, 3 * nq),
        in_specs=[pl.BlockSpec((tm, d), lambda m, n: (m, 0)),
                  pl.BlockSpec((srows, d), lambda m, n: (0, 0)),
                  pl.BlockSpec((1, d), lambda m, n: (0, 0)),
                  _mod_spec(1, d, 0, tiles_per_batch),
                  _mod_spec(1, d, 1, tiles_per_batch),
                  _mod_spec(srows, d, 0, m_rows),
                  _mod_spec(srows, d, 1, m_rows),
                  pl.BlockSpec((None, tn, d), lambda m, n: (layer, n, 0)),
                  pl.BlockSpec((None, heads, d), lambda m, n: (layer, 3 * d // heads, 0)),
                  pl.BlockSpec((1, heads), lambda m, n: (0, 0))],
        out_specs=out_specs(tm, True) + out_specs(srows, False),
        scratch_shapes=[pltpu.VMEM((tm + srows, d), BF16)],
        compiler_params=_params(("arbitrary", "arbitrary"), vmem),
        name="fox_in_proj",
    )(x, xs, g, mods_p, mods_p, mods_s, mods_s, w_t, w_t, b_f)
    return outs[:6], outs[6:]


def _cumsum_kernel(lf_ref, aug_ref, carry_ref, *, hd):
    @pl.when(pl.program_id(1) == 0)
    def _():
        carry_ref[...] = jnp.zeros_like(carry_ref)

    tc, heads = lf_ref.shape
    row = lax.broadcasted_iota(jnp.int32, (tc, tc), 0)
    col = lax.broadcasted_iota(jnp.int32, (tc, tc), 1)
    tri = jnp.where(col <= row, 1.0, 0.0).astype(BF16)
    c = carry_ref[...]
    for piece in _split3(lf_ref[...]):
        c = c + jnp.dot(tri, piece, preferred_element_type=F32)
    carry_ref[...] = c[tc - 1:tc, :]

    row = lax.broadcasted_iota(jnp.int32, (3 * heads, heads * hd), 0)
    lane = lax.broadcasted_iota(jnp.int32, (3 * heads, heads * hd), 1)
    place = jnp.where(lane == (row % heads) * hd + row // heads, 1.0, 0.0).astype(BF16)
    pieces = jnp.concatenate(_split3(c * (-LOG2E)), axis=1)
    aug_ref[...] = jnp.dot(pieces, place, preferred_element_type=F32).astype(BF16)


def _cumsum(logf, batch, seq, hd):
    rows, heads = logf.shape
    tc = min(CUMSUM_TILE, seq)
    per = seq // tc
    return pl.pallas_call(
        functools.partial(_cumsum_kernel, hd=hd),
        out_shape=jax.ShapeDtypeStruct((rows, heads * hd), BF16),
        grid=(batch, per),
        in_specs=[pl.BlockSpec((tc, heads), lambda b, t: (b * per + t, 0))],
        out_specs=pl.BlockSpec((tc, heads * hd), lambda b, t: (b * per + t, 0)),
        scratch_shapes=[pltpu.VMEM((1, heads), F32)],
        compiler_params=_params(("arbitrary", "arbitrary"), 24 << 20),
        name="fox_logf_cumsum",
    )(logf)


def _fox_attn_kernel(q_ref, k_ref, v_ref, aug_ref, *rest, tq, tk, ts, hd, heads_per_step, n_round):
    o_ref, vt_ref = rest[n_round], rest[-1]
    _WeightRounding.run(rest[:n_round], rest[n_round + 1:-1])
    i = pl.program_id(2)
    seq = k_ref.shape[0]
    q0 = pl.multiple_of(i * tq, tq)
    lane_tile = 128

    @pl.when(i == 0)
    def _():
        for t in range(seq // tk):
            vt_ref[:, t * tk:(t + 1) * tk] = v_ref[t * tk:(t + 1) * tk, :].astype(F32).T.astype(BF16)

    ones = jnp.where(lax.broadcasted_iota(jnp.int32, (tq, hd), 1) < 3, 1.0, 0.0).astype(BF16)
    q_ext = [jnp.concatenate([q_ref[:, hh * hd:(hh + 1) * hd], ones], axis=1)
             for hh in range(heads_per_step)]

    def step(j, carry, masked):
        carry = list(carry)
        items = [(sub, hh) for sub in range(tk // ts) for hh in range(heads_per_step)]

        def scores(sub, hh):
            k0 = pl.multiple_of(j * tk + sub * ts, ts)
            lanes = slice(hh * hd, (hh + 1) * hd)
            k_ext = jnp.concatenate([k_ref[pl.ds(k0, ts), lanes], aug_ref[pl.ds(k0, ts), lanes]], axis=1)
            return lax.dot_general(k_ext, q_ext[hh], (((1,), (1,)), ((), ())),
                                   preferred_element_type=F32)

        s_next = scores(*items[0])
        for n, (sub, hh) in enumerate(items):
            s = s_next
            if n + 1 < len(items):
                s_next = scores(*items[n + 1])
            m, l, acc = carry[3 * hh:3 * hh + 3]
            k0 = pl.multiple_of(j * tk + sub * ts, ts)
            m_t, l_t, p_t = [], [], []
            for qt in range(tq // lane_tile):
                ql = slice(qt * lane_tile, (qt + 1) * lane_tile)
                sq = s[:, ql]
                if masked:
                    keep = (k0 + lax.broadcasted_iota(jnp.int32, (ts, lane_tile), 0)
                            <= q0 + qt * lane_tile + lax.broadcasted_iota(jnp.int32, (ts, lane_tile), 1))
                    sq = jnp.where(keep, sq, NEG_BIG)
                mq = jnp.maximum(m[:, ql], jnp.max(sq, axis=0, keepdims=True))
                pq = jnp.exp2(sq - mq)
                l_t.append(jnp.exp2(m[:, ql] - mq) * l[:, ql] + jnp.sum(pq, axis=0, keepdims=True))
                m_t.append(mq)
                p_t.append(pq.astype(BF16))
            m_new = jnp.concatenate(m_t, axis=1)
            alpha = jnp.exp2(m - m_new)
            vt = vt_ref[hh * hd:(hh + 1) * hd, pl.ds(k0, ts)]
            acc = alpha * acc + jnp.dot(vt, jnp.concatenate(p_t, axis=1), preferred_element_type=F32)
            carry[3 * hh:3 * hh + 3] = [m_new, jnp.concatenate(l_t, axis=1), acc]
        return tuple(carry)

    init = (jnp.full((1, tq), NEG_BIG, F32), jnp.zeros((1, tq), F32),
            jnp.zeros((hd, tq), F32)) * heads_per_step
    n_full = q0 // tk
    carry = lax.fori_loop(0, n_full, lambda j, c: step(j, c, False), init)
    carry = step(n_full, carry, True)
    for hh in range(heads_per_step):
        _, l, acc = carry[3 * hh:3 * hh + 3]
        o_ref[:, hh * hd:(hh + 1) * hd] = (acc / l).T.astype(o_ref.dtype)


def _fox_attn(q, k, v, aug, batch, seq, heads, head_dim, weights_to_round):
    tq = min(ATTN_Q_TILE, seq)
    tk = min(ATTN_K_TILE, seq)
    nq = seq // tq
    hps = ATTN_HEADS_PER_STEP if heads % ATTN_HEADS_PER_STEP == 0 else 1
    groups = heads // hps
    width = hps * head_dim
    rounding = _WeightRounding(weights_to_round, batch * groups * nq,
                               lambda b, h, i: (b * groups + h) * nq + i)
    kernel = functools.partial(_fox_attn_kernel, tq=tq, tk=tk, ts=min(ATTN_K_SUB, tk), hd=head_dim,
                               heads_per_step=hps, n_round=len(rounding))
    vmem = (7 * _nbytes((seq, width), BF16) + 6 * hps * _nbytes((tk, tq), F32) + rounding.vmem_bytes()
            + (8 << 20))
    outs = pl.pallas_call(
        kernel,
        out_shape=[jax.ShapeDtypeStruct(q.shape, BF16)] + rounding.out_shapes(),
        grid=(batch, groups, nq),
        in_specs=[pl.BlockSpec((tq, width), lambda b, h, i: (b * nq + i, h)),
                  pl.BlockSpec((seq, width), lambda b, h, i: (b, h)),
                  pl.BlockSpec((seq, width), lambda b, h, i: (b, h)),
                  pl.BlockSpec((seq, width), lambda b, h, i: (b, h))] + rounding.in_specs(),
        out_specs=[pl.BlockSpec((tq, width), lambda b, h, i: (b * nq + i, h))] + rounding.out_specs(),
        scratch_shapes=[pltpu.VMEM((width, seq), BF16)],
        compiler_params=_params(("arbitrary", "arbitrary", "arbitrary"), vmem),
        name="fox_prompt_attention",
    )(q, k, v, aug, *rounding.operands())
    return outs[0], outs[1:]


def _page_suffix_kernel(lf_ref, suf_ref, tot_ref, *, heads):
    x = lf_ref[...]
    cols = x.shape[1]
    col = lax.broadcasted_iota(jnp.int32, x.shape, 1)
    incl = x
    tot = x
    sh = heads
    while sh < cols:
        incl = incl + jnp.where(col + sh < cols, pltpu.roll(incl, cols - sh, 1), 0.0)
        tot = tot + pltpu.roll(tot, sh, 1)
        sh *= 2
    suf_ref[...] = incl - x
    tot_ref[...] = tot


def _page_suffix(cache_lf, heads):
    n_rows, cols = cache_lf.shape
    tr = 256 if n_rows % 256 == 0 else n_rows
    shape = jax.ShapeDtypeStruct((n_rows, cols), F32)
    spec = pl.BlockSpec((tr, cols), lambda r: (r, 0))
    return pl.pallas_call(
        functools.partial(_page_suffix_kernel, heads=heads),
        out_shape=(shape, shape),
        grid=(n_rows // tr,),
        in_specs=[spec],
        out_specs=(spec, spec),
        compiler_params=_params(("arbitrary",), 12 * _nbytes((tr, cols), F32) + (4 << 20)),
        name="fox_page_logf_suffix",
    )(cache_lf)


def _fox_decode_kernel(pt_ref, q_ref, kn_ref, vn_ref, cn_ref, *refs, pages, page_id):
    k_refs = refs[:pages]
    v_refs = refs[pages:2 * pages]
    suf_refs = refs[2 * pages:3 * pages]
    tot_refs = refs[3 * pages:4 * pages]
    o_ref, m_ref, l_ref, acc_ref, run_ref = refs[4 * pages:]
    s_idx = pl.program_id(1)
    heads = q_ref.shape[0]
    cols = k_refs[0].shape[0]
    head_mask = (lax.broadcasted_iota(jnp.int32, (heads, cols), 1) % heads
                 == lax.broadcasted_iota(jnp.int32, (heads, cols), 0))

    @pl.when(s_idx == 0)
    def _():
        m_ref[...] = jnp.full_like(m_ref, NEG_BIG)
        l_ref[...] = jnp.zeros_like(l_ref)
        acc_ref[...] = jnp.zeros_like(acc_ref)
        run_ref[...] = jnp.zeros_like(run_ref)

    q = q_ref[...]
    cn = cn_ref[...]
    run = run_ref[...]
    scores = []
    for g in range(pages):
        s = lax.dot_general(q, k_refs[g][...].astype(BF16), (((1,), (1,)), ((), ())),
                            preferred_element_type=F32)
        row = page_id(pt_ref, pl.program_id(0), s_idx, g) % F32_SUBLANES
        scores.append(jnp.where(head_mask, s + (suf_refs[g][pl.ds(row, 1), :] + (run + cn)), NEG_BIG))
        run = run + tot_refs[g][pl.ds(row, 1), :]
    run_ref[...] = run
    m = m_ref[...]
    m_new = m
    for s in scores:
        m_new = jnp.maximum(m_new, jnp.max(s, axis=-1, keepdims=True))
    alpha = jnp.exp(m - m_new)
    l = alpha * l_ref[...]
    acc = alpha * acc_ref[...]
    for g in range(pages):
        p = jnp.exp(scores[g] - m_new)
        l = l + jnp.sum(p, axis=-1, keepdims=True)
        acc = acc + jnp.dot(p.astype(BF16), v_refs[g][...].astype(BF16), preferred_element_type=F32)
    m_ref[...] = m_new
    l_ref[...] = l
    acc_ref[...] = acc

    @pl.when(s_idx == pl.num_programs(1) - 1)
    def _():
        s_new = jnp.sum(q.astype(F32) * kn_ref[...], axis=-1, keepdims=True)
        m = m_ref[...]
        m_new = jnp.maximum(m, s_new)
        alpha = jnp.exp(m - m_new)
        p_new = jnp.exp(s_new - m_new)
        l = alpha * l_ref[...] + p_new
        acc = alpha * acc_ref[...] + p_new.astype(BF16).astype(F32) * vn_ref[...]
        o_ref[...] = acc / l


def _fox_decode(page_table, pool_base, q, k_new, v_new, logf_new, cache_k, cache_v, cache_lf):
    bs, n_pages = page_table.shape
    _, heads, hd = q.shape
    n_pool, cols, _ = cache_k.shape
    assert n_pool % F32_SUBLANES == 0 and pool_base % F32_SUBLANES == 0
    suffix, total = _page_suffix(cache_lf, heads)
    g_pages = PAGES_PER_STEP if n_pages % PAGES_PER_STEP == 0 else 1
    steps = n_pages // g_pages

    def page_id(pt, b, s, g):
        return pool_base + pt[b * n_pages + (n_pages - 1 - (s * g_pages + g))]

    kernel = functools.partial(_fox_decode_kernel, pages=g_pages, page_id=page_id)

    def page_idx(g):
        return lambda b, s, pt: (page_id(pt, b, s, g), 0, 0)

    def row_block_idx(g):
        return lambda b, s, pt: (page_id(pt, b, s, g) // F32_SUBLANES, 0)

    tok = pl.BlockSpec((None, heads, hd), lambda b, s, pt: (b, 0, 0))
    in_specs = [tok, tok, tok, pl.BlockSpec((None, 1, cols), lambda b, s, pt: (b, 0, 0))]
    in_specs += [pl.BlockSpec((None, cols, hd), page_idx(g)) for g in range(g_pages)]
    in_specs += [pl.BlockSpec((None, cols, hd), page_idx(g)) for g in range(g_pages)]
    in_specs += [pl.BlockSpec((F32_SUBLANES, cols), row_block_idx(g)) for g in range(g_pages)]
    in_specs += [pl.BlockSpec((F32_SUBLANES, cols), row_block_idx(g)) for g in range(g_pages)]
    vmem = (4 * g_pages * _nbytes((cols, hd), F32) + 2 * g_pages * _nbytes((cols, hd), BF16)
            + 4 * g_pages * _nbytes((heads, cols), F32) + (4 << 20))
    grid_spec = pltpu.PrefetchScalarGridSpec(
        num_scalar_prefetch=1,
        grid=(bs, steps),
        in_specs=in_specs,
        out_specs=pl.BlockSpec((None, heads, hd), lambda b, s, pt: (b, 0, 0)),
        scratch_shapes=[pltpu.VMEM((heads, 1), F32), pltpu.VMEM((heads, 1), F32),
                        pltpu.VMEM((heads, hd), F32), pltpu.VMEM((1, cols), F32)])
    return pl.pallas_call(
        kernel,
        out_shape=jax.ShapeDtypeStruct((bs, heads, hd), F32),
        grid_spec=grid_spec,
        compiler_params=_params(("arbitrary", "arbitrary"), vmem),
        name="fox_decode_attention",
    )(page_table.reshape(-1), q, k_new, v_new, logf_new,
      *([cache_k] * g_pages), *([cache_v] * g_pages), *([suffix] * g_pages), *([total] * g_pages))


def _proj_residual_kernel(a_ref, w_ref, x_ref, gate_ref, o_ref):
    z = jnp.dot(a_ref[...], w_ref[...].astype(BF16), preferred_element_type=F32)
    o_ref[...] = x_ref[...] + gate_ref[...] * z


def _proj_residual(a, w, layer, x, mods, mod_rows, tiles_per_group, gate_chunk, tm):
    m_rows, k = a.shape
    d = w.shape[2]
    tn = min(COL_TILE, d)
    vmem = (2 * _nbytes((tm, k), BF16) + 2 * _nbytes((k, tn), F32) + _nbytes((k, tn), BF16)
            + 5 * _nbytes((tm, tn), F32) + (4 << 20))
    return pl.pallas_call(
        _proj_residual_kernel,
        out_shape=jax.ShapeDtypeStruct((m_rows, d), F32),
        grid=(m_rows // tm, d // tn),
        in_specs=[pl.BlockSpec((tm, k), lambda m, n: (m, 0)),
                  pl.BlockSpec((None, k, tn), lambda m, n: (layer, 0, n)),
                  pl.BlockSpec((tm, tn), lambda m, n: (m, n)),
                  pl.BlockSpec((None, mod_rows, tn),
                               lambda m, n: (m // tiles_per_group, 0, gate_chunk * (d // tn) + n))],
        out_specs=pl.BlockSpec((tm, tn), lambda m, n: (m, n)),
        compiler_params=_params(("arbitrary", "arbitrary"), vmem),
        name="mixer_out_proj",
    )(a, w, x, mods)


def _mlp_kernel(x_ref, xs_ref, g_ref, sh_ref, sc_ref, gate_ref, shs_ref, scs_ref, gates_ref,
                wu_ref, wd_ref, fg_ref, o_ref, os_ref, h_ref, *, tm, final_norm):
    f = pl.program_id(1)

    @pl.when(f == 0)
    def _():
        h_ref[:tm, :] = _norm_mod(x_ref[...], g_ref[...], sh_ref[...], sc_ref[...]).astype(BF16)
        h_ref[tm:, :] = _norm_mod(xs_ref[...], g_ref[...], shs_ref[...], scs_ref[...]).astype(BF16)
        o_ref[...] = jnp.zeros_like(o_ref)
        os_ref[...] = jnp.zeros_like(os_ref)

    u = jnp.dot(h_ref[...], wu_ref[...].astype(BF16), preferred_element_type=F32)
    u = jnp.square(jnp.maximum(u, 0.0)).astype(BF16)
    z = jnp.dot(u, wd_ref[...].astype(BF16), preferred_element_type=F32)
    o_ref[...] += z[:tm]
    os_ref[...] += z[tm:]

    @pl.when(f == pl.num_programs(1) - 1)
    def _():
        for x_r, gate_r, o_r in ((x_ref, gate_ref, o_ref), (xs_ref, gates_ref, os_ref)):
            y = x_r[...] + gate_r[...] * o_r[...]
            if final_norm:
                y = y * lax.rsqrt(jnp.mean(y * y, axis=-1, keepdims=True) + EPS) * fg_ref[...]
            o_r[...] = y


def _mlp(x, xs, g, mods_p, mods_s, tiles_per_batch, w_up, w_down, layer, final_g, final_norm, tm):
    m_rows, d = x.shape
    srows = xs.shape[0]
    d_ff = w_up.shape[2]
    tf = min(COL_TILE, d_ff)
    kernel = functools.partial(_mlp_kernel, tm=tm, final_norm=final_norm)
    x_buffers = 1 if w_up.dtype == F32 else 2
    x_mode = {"pipeline_mode": pl.Buffered(1)} if x_buffers == 1 else {}
    vmem = ((2 + x_buffers) * _nbytes((tm, d), F32) + _nbytes((tm + srows, d), BF16)
            + 4 * _nbytes((d, tf), w_up.dtype) + 2 * _nbytes((d, tf), BF16) + 4 * _nbytes((tm, tf), F32)
            + (4 << 20))
    full = lambda m, f: (0, 0)
    vec = pl.BlockSpec((1, d), full)
    return pl.pallas_call(
        kernel,
        out_shape=(jax.ShapeDtypeStruct((m_rows, d), F32), jax.ShapeDtypeStruct((srows, d), F32)),
        grid=(m_rows // tm, d_ff // tf),
        in_specs=[pl.BlockSpec((tm, d), lambda m, f: (m, 0), **x_mode),
                  pl.BlockSpec((srows, d), full),
                  vec,
                  _mod_spec(1, d, 3, tiles_per_batch),
                  _mod_spec(1, d, 4, tiles_per_batch),
                  _mod_spec(1, d, 5, tiles_per_batch),
                  _mod_spec(srows, d, 3, m_rows),
                  _mod_spec(srows, d, 4, m_rows),
                  _mod_spec(srows, d, 5, m_rows),
                  pl.BlockSpec((None, d, tf), lambda m, f: (layer, 0, f)),
                  pl.BlockSpec((None, tf, d), lambda m, f: (layer, f, 0)),
                  vec],
        out_specs=(pl.BlockSpec((tm, d), lambda m, f: (m, 0)), pl.BlockSpec((srows, d), full)),
        scratch_shapes=[pltpu.VMEM((tm + srows, d), BF16)],
        compiler_params=_params(("arbitrary", "arbitrary"), vmem),
        name="relu2_mlp",
    )(x, xs, g, mods_p, mods_p, mods_p, mods_s, mods_s, mods_s, w_up, w_down, final_g)


def _ret_proj_kernel(x_ref, xs_ref, g_ref, sh_ref, sc_ref, shs_ref, scs_ref, w_ref,
                     cos_ref, sin_ref, coss_ref, sins_ref,
                     q_ref, k_ref, v_ref, gt_ref, qs_ref, ks_ref, vs_ref, gts_ref, h_ref, *,
                     nq, tm, key_dim, k_scale):
    n = pl.program_id(1)

    @pl.when(n == 0)
    def _():
        h_ref[:tm, :] = _norm_mod(x_ref[...], g_ref[...], sh_ref[...], sc_ref[...]).astype(BF16)
        h_ref[tm:, :] = _norm_mod(xs_ref[...], g_ref[...], shs_ref[...], scs_ref[...]).astype(BF16)

    z = jnp.dot(h_ref[...], w_ref[...].astype(BF16), preferred_element_type=F32)
    zp = z[:tm]
    zs = z[tm:]
    half = key_dim // 2

    def rotate(zz, c_ref, s_ref):
        cos = c_ref[...]
        sin = s_ref[...]
        parts = []
        for hh in range(zz.shape[1] // key_dim):
            x1 = zz[:, hh * key_dim:hh * key_dim + half]
            x2 = zz[:, hh * key_dim + half:(hh + 1) * key_dim]
            parts += [x1 * cos - x2 * sin, x1 * sin + x2 * cos]
        return jnp.concatenate(parts, axis=1)

    last_tile = pl.program_id(0) == pl.num_programs(0) - 1
    in_q = n < nq
    in_k = jnp.logical_and(n >= nq, n < 2 * nq)
    in_v = jnp.logical_and(n >= 2 * nq, n < 4 * nq)
    in_g = n >= 4 * nq

    @pl.when(in_q)
    def _():
        q_ref[...] = rotate(zp, cos_ref, sin_ref).astype(BF16)

    @pl.when(in_k)
    def _():
        k_ref[...] = rotate(zp, cos_ref, sin_ref) * k_scale

    @pl.when(in_v)
    def _():
        v_ref[...] = zp.astype(BF16)

    @pl.when(in_g)
    def _():
        gt_ref[...] = zp

    @pl.when(jnp.logical_and(last_tile, in_q))
    def _():
        qs_ref[...] = rotate(zs, coss_ref, sins_ref).astype(BF16)

    @pl.when(jnp.logical_and(last_tile, in_k))
    def _():
        ks_ref[...] = rotate(zs, coss_ref, sins_ref) * k_scale

    @pl.when(jnp.logical_and(last_tile, in_v))
    def _():
        vs_ref[...] = zs.astype(BF16)

    @pl.when(jnp.logical_and(last_tile, in_g))
    def _():
        gts_ref[...] = zs


def _ret_proj(x, xs, g, mods_p, mods_s, tiles_per_batch, w_in, layer, cos, sin, cos_s, sin_s, key_dim, tm):
    m_rows, d = x.shape
    srows = xs.shape[0]
    tn = min(COL_TILE, d)
    nq = d // tn
    half = key_dim // 2
    kernel = functools.partial(_ret_proj_kernel, nq=nq, tm=tm, key_dim=key_dim, k_scale=key_dim ** -0.5)

    last_tile = m_rows // tm - 1

    def col(lo, width, by_tile):
        if by_tile:
            return lambda m, n: (m, jnp.clip(n - lo, 0, width - 1))
        return lambda m, n: (0, jnp.where(m == last_tile, jnp.clip(n - lo, 0, width - 1), 0))

    vmem = (2 * _nbytes((tm, d), F32) + _nbytes((tm + srows, d), BF16) + 2 * _nbytes((d, tn), F32)
            + _nbytes((d, tn), BF16) + 2 * (2 * _nbytes((tm, tn), BF16) + 2 * _nbytes((tm, tn), F32))
            + 4 * _nbytes((tm, half), F32) + 4 * _nbytes((tm, tn), F32) + (4 << 20))

    def shapes(rows):
        return (jax.ShapeDtypeStruct((rows, d), BF16), jax.ShapeDtypeStruct((rows, d), F32),
                jax.ShapeDtypeStruct((rows, 2 * d), BF16), jax.ShapeDtypeStruct((rows, 2 * d), F32))

    def out_specs(rows, by_tile):
        return (pl.BlockSpec((rows, tn), col(0, nq, by_tile)),
                pl.BlockSpec((rows, tn), col(nq, nq, by_tile)),
                pl.BlockSpec((rows, tn), col(2 * nq, 2 * nq, by_tile)),
                pl.BlockSpec((rows, tn), col(4 * nq, 2 * nq, by_tile)))

    full = lambda m, n: (0, 0)
    outs = pl.pallas_call(
        kernel,
        out_shape=shapes(m_rows) + shapes(srows),
        grid=(m_rows // tm, 6 * nq),
        in_specs=[pl.BlockSpec((tm, d), lambda m, n: (m, 0)),
                  pl.BlockSpec((srows, d), full),
                  pl.BlockSpec((1, d), full),
                  _mod_spec(1, d, 0, tiles_per_batch),
                  _mod_spec(1, d, 1, tiles_per_batch),
                  _mod_spec(srows, d, 0, m_rows),
                  _mod_spec(srows, d, 1, m_rows),
                  pl.BlockSpec((None, d, tn), lambda m, n: (layer, 0, n)),
                  pl.BlockSpec((tm, half), lambda m, n: (m % tiles_per_batch, 0)),
                  pl.BlockSpec((tm, half), lambda m, n: (m % tiles_per_batch, 0)),
                  pl.BlockSpec((srows, half), full),
                  pl.BlockSpec((srows, half), full)],
        out_specs=out_specs(tm, True) + out_specs(srows, False),
        scratch_shapes=[pltpu.VMEM((tm + srows, d), BF16)],
        compiler_params=_params(("arbitrary", "arbitrary"), vmem),
        name="ret_in_proj",
    )(x, xs, g, mods_p, mods_p, mods_s, mods_s, w_in, cos, sin, cos_s, sin_s)
    return outs[:4], outs[4:]


def _group_norm_gate(o, gate):
    mu = jnp.mean(o, axis=-1, keepdims=True)
    cen = o - mu
    var = jnp.mean(cen * cen, axis=-1, keepdims=True)
    return _silu(gate) * (cen * lax.rsqrt(var + EPS))


def _ret_chunk_kernel(lg_ref, q_ref, k_ref, v_ref, gt_ref, *rest, n_round):
    y_ref, s_ref = rest[n_round:n_round + 2]
    _WeightRounding.run(rest[:n_round], rest[n_round + 2:])
    lg = jnp.full((1, 1), lg_ref[pl.program_id(1)], F32)

    @pl.when(pl.program_id(2) == 0)
    def _():
        s_ref[...] = jnp.zeros_like(s_ref)

    q = q_ref[...]
    k = k_ref[...]
    v = v_ref[...]
    ln = q.shape[0]
    diff = (lax.broadcasted_iota(jnp.int32, (ln, ln), 0)
            - lax.broadcasted_iota(jnp.int32, (ln, ln), 1)).astype(F32)
    dec = jnp.where(diff >= 0, jnp.exp(lg * jnp.maximum(diff, 0.0)), 0.0)
    n = lax.broadcasted_iota(jnp.int32, (ln, 1), 0).astype(F32)
    xi = jnp.exp(lg * (n + 1.0))
    zeta = jnp.exp(lg * (ln - 1.0 - n))

    s0 = s_ref[...]
    qk = lax.dot_general(q, k.astype(BF16), (((1,), (1,)), ((), ())), preferred_element_type=F32)
    a = (qk * dec).astype(BF16)
    o = (jnp.dot(a, v, preferred_element_type=F32)
         + jnp.dot(q, s0.astype(BF16), preferred_element_type=F32) * xi)
    kz = (k * zeta).astype(BF16)
    s_ref[...] = jnp.exp(lg * ln) * s0 + lax.dot_general(
        kz, v, (((0,), (0,)), ((), ())), preferred_element_type=F32)
    y_ref[...] = _group_norm_gate(o, gt_ref[...]).astype(y_ref.dtype)


def _ret_prompt(lg, q, k, v, gate, batch, seq, heads, key_dim, val_dim, weights_to_round):
    ln = min(RET_CHUNK, seq)
    nc = seq // ln
    rounding = _WeightRounding(weights_to_round, batch * heads * nc,
                               lambda b, h, c, *_: (b * heads + h) * nc + c)
    vmem = (2 * (_nbytes((ln, key_dim), BF16) + _nbytes((ln, key_dim), F32) + _nbytes((ln, val_dim), BF16)
                 + _nbytes((ln, val_dim), F32) + _nbytes((ln, val_dim), BF16))
            + 4 * _nbytes((key_dim, val_dim), F32) + 8 * _nbytes((ln, val_dim), F32)
            + rounding.vmem_bytes() + (8 << 20))
    grid_spec = pltpu.PrefetchScalarGridSpec(
        num_scalar_prefetch=1,
        grid=(batch, heads, nc),
        in_specs=[pl.BlockSpec((ln, key_dim), lambda b, h, c, lg: (b * nc + c, h)),
                  pl.BlockSpec((ln, key_dim), lambda b, h, c, lg: (b * nc + c, h)),
                  pl.BlockSpec((ln, val_dim), lambda b, h, c, lg: (b * nc + c, h)),
                  pl.BlockSpec((ln, val_dim), lambda b, h, c, lg: (b * nc + c, h))] + rounding.in_specs(),
        out_specs=[pl.BlockSpec((ln, val_dim), lambda b, h, c, lg: (b * nc + c, h)),
                   pl.BlockSpec((None, None, key_dim, val_dim), lambda b, h, c, lg: (b, h, 0, 0))]
        + rounding.out_specs())
    outs = pl.pallas_call(
        functools.partial(_ret_chunk_kernel, n_round=len(rounding)),
        out_shape=[jax.ShapeDtypeStruct((batch * seq, heads * val_dim), BF16),
                   jax.ShapeDtypeStruct((batch, heads, key_dim, val_dim), F32)] + rounding.out_shapes(),
        grid_spec=grid_spec,
        compiler_params=_params(("arbitrary", "arbitrary", "arbitrary"), vmem),
        name="ret_prompt_chunks",
    )(lg, q, k, v, gate, *rounding.operands())
    return outs[0], outs[1], outs[2:]


def _ret_decode_kernel(lg_ref, q_ref, k_ref, v_ref, gt_ref, s0_ref, y_ref, s_ref):
    q = q_ref[...]
    k = k_ref[...]
    q_cols = q.T
    k_cols = k.T
    a_all = jnp.sum(q * k, axis=-1, keepdims=True)
    for h in range(s0_ref.shape[0]):
        gamma = jnp.exp(jnp.full((1, 1), lg_ref[h], F32))
        s0 = s0_ref[h]
        v = v_ref[h:h + 1, :]
        qs = jnp.sum(q_cols[:, h:h + 1] * s0, axis=0, keepdims=True)
        o = a_all[h:h + 1, :] * v + qs * gamma
        s_ref[h] = gamma * s0 + k_cols[:, h:h + 1] * v
        y_ref[h:h + 1, :] = _group_norm_gate(o, gt_ref[h:h + 1, :])


def _ret_decode(lg, q, k, v, gate, state):
    bs, heads, key_dim, val_dim = state.shape

    def vec(width):
        return pl.BlockSpec((None, heads, width), lambda b, lg: (b, 0, 0))

    state_spec = pl.BlockSpec((None, heads, key_dim, val_dim), lambda b, lg: (b, 0, 0, 0))
    grid_spec = pltpu.PrefetchScalarGridSpec(
        num_scalar_prefetch=1,
        grid=(bs,),
        in_specs=[vec(key_dim), vec(key_dim), vec(val_dim), vec(val_dim), state_spec],
        out_specs=(vec(val_dim), state_spec))
    y, s = pl.pallas_call(
        _ret_decode_kernel,
        out_shape=(jax.ShapeDtypeStruct((bs, heads, val_dim), F32),
                   jax.ShapeDtypeStruct(state.shape, F32)),
        grid_spec=grid_spec,
        compiler_params=_params(("arbitrary",), 4 * _nbytes(state.shape[1:], F32) + (16 << 20)),
        name="ret_decode_step",
    )(lg, q.reshape(bs, heads, key_dim), k.reshape(bs, heads, key_dim),
      v.reshape(bs, heads, val_dim), gate.reshape(bs, heads, val_dim), state)
    return y.reshape(bs, heads * val_dim), s


def _rotary_tables(pos, key_dim):
    half = key_dim // 2
    inv = 1.0 / (ROPE_BASE ** jnp.linspace(0.0, 1.0, half, dtype=F32))
    ang = pos.astype(F32)[:, None] * inv[None, :]
    return jnp.cos(ang), jnp.sin(ang)


def _pad_rows(a, rows):
    return jnp.pad(a, ((0, rows - a.shape[0]), (0, 0)))


def kernel(x_prompt, x_sample, c_prompt, c_sample, cache_fox_k, cache_fox_v, cache_fox_logf, state_ret, page_table, norm1_g, norm2_g, ada_w, ada_b, fox_w_in, fox_b_f, fox_w_out, ret_w_in, ret_w_out, mlp_w_up, mlp_w_down, final_g):
    bp, tp, d = x_prompt.shape
    bs, ts, _ = x_sample.shape
    assert ts == 1, "the decode kernels handle one new token per sequence"
    depth = ada_w.shape[0]
    _, n_pool, page, fox_heads, fox_hd = cache_fox_k.shape
    _, _, ret_heads, key_dim, val_dim = state_ret.shape
    past_len = page_table.shape[1] * page
    srows = -(-bs // BF16_SUBLANES) * BF16_SUBLANES
    tm = min(ROW_TILE, tp)
    tiles_p = tp // tm

    mod_rows = -(-(bs + bp) // 8) * 8
    c_all = _pad_rows(jnp.concatenate([c_sample, c_prompt], axis=0), max(mod_rows, srows))
    mods = _ada(c_all, ada_w, ada_b)

    lg = jnp.log1p(-jnp.exp2(-5.0 - jnp.arange(ret_heads, dtype=F32)))
    cos_p, sin_p = _rotary_tables(jnp.arange(tp), key_dim)
    cos_s, sin_s = _rotary_tables(jnp.full((srows,), past_len), key_dim)

    yp = x_prompt.reshape(bp * tp, d)
    ys = _pad_rows(x_sample.reshape(bs, d), srows)
    fg = final_g.reshape(1, d)
    outs = {name: [] for name in ("kp", "vp", "lfp", "ks", "vs", "lfs", "sp", "ss")}

    ret_w_in_rounded = None
    for i in range(depth):
        mods_p = mods[i, bs:bs + bp].reshape(bp, 1, 6 * d)
        mods_s = mods[i, :srows].reshape(1, srows, 6 * d)
        g1 = norm1_g[i].reshape(1, d)
        g2 = norm2_g[i].reshape(1, d)
        j = i // 2
        if i % 2 == 0:
            b_f = fox_b_f[j].reshape(1, fox_heads)
            (q, kf, vf, kb, vb, lf), sample = _fox_proj(
                yp, ys, g1, mods_p, mods_s, tiles_p, fox_w_in, j, b_f, tm,
                fox_hd ** -0.5 * LOG2E, fox_hd ** -0.5)
            aug = _cumsum(lf, bp, tp, fox_hd)
            to_round = [(fox_w_out, j), (mlp_w_up, i), (mlp_w_down, i)]
            if i + 1 < depth:
                to_round.append((ret_w_in, (i + 1) // 2))
            o, rounded = _fox_attn(q, kb, vb, aug, bp, tp, fox_heads, fox_hd, to_round)
            w_out, w_up, w_down = rounded[:3]
            ret_w_in_rounded = rounded[3] if i + 1 < depth else None
            yp = _proj_residual(o, w_out, 0, yp, mods_p, 1, tiles_p, 2, tm)
            outs["kp"].append(kf.reshape(bp, tp, fox_heads, fox_hd))
            outs["vp"].append(vf.reshape(bp, tp, fox_heads, fox_hd))
            outs["lfp"].append(lf.reshape(bp, tp, fox_heads))
            q, kf, vf, kb, vb, lf = sample
            o = _fox_decode(
                page_table, j * n_pool,
                q[:bs].reshape(bs, fox_heads, fox_hd), kb[:bs].astype(F32).reshape(bs, fox_heads, fox_hd),
                vb[:bs].astype(F32).reshape(bs, fox_heads, fox_hd),
                jnp.tile(lf[:bs], (1, page)).reshape(bs, 1, page * fox_heads),
                cache_fox_k.reshape(-1, page * fox_heads, fox_hd),
                cache_fox_v.reshape(-1, page * fox_heads, fox_hd),
                cache_fox_logf.reshape(-1, page * fox_heads))
            o = _pad_rows(o.reshape(bs, d), srows).astype(BF16)
            ys = _proj_residual(o, w_out, 0, ys, mods_s, srows, 1, 2, srows)
            outs["ks"].append(kf[:bs].reshape(bs, ts, fox_heads, fox_hd))
            outs["vs"].append(vf[:bs].reshape(bs, ts, fox_heads, fox_hd))
            outs["lfs"].append(lf[:bs].reshape(bs, ts, fox_heads))
        else:
            w_in, w_in_layer = (ret_w_in, j) if ret_w_in_rounded is None else (ret_w_in_rounded, 0)
            (q, k, v, gate), sample = _ret_proj(yp, ys, g1, mods_p, mods_s, tiles_p, w_in, w_in_layer,
                                                cos_p, sin_p, cos_s, sin_s, key_dim, tm)
            y, s, (w_out, w_up, w_down) = _ret_prompt(
                lg, q, k, v, gate, bp, tp, ret_heads, key_dim, val_dim,
                [(ret_w_out, j), (mlp_w_up, i), (mlp_w_down, i)])
            yp = _proj_residual(y, w_out, 0, yp, mods_p, 1, tiles_p, 2, tm)
            outs["sp"].append(s)
            q, k, v, gate = sample
            y, s = _ret_decode(lg, q[:bs].astype(F32), k[:bs], v[:bs].astype(F32), gate[:bs],
                               state_ret[j].astype(F32))
            ys = _proj_residual(_pad_rows(y, srows).astype(BF16), w_out, 0, ys, mods_s, srows, 1, 2, srows)
            outs["ss"].append(s)
        yp, ys = _mlp(yp, ys, g2, mods_p, mods_s, tiles_p, w_up, w_down, 0, fg, i == depth - 1, tm)

    return (yp.reshape(bp, tp, d), ys[:bs].reshape(bs, ts, d),
            jnp.stack(outs["kp"]), jnp.stack(outs["vp"]), jnp.stack(outs["lfp"]),
            jnp.stack(outs["ks"]), jnp.stack(outs["vs"]), jnp.stack(outs["lfs"]),
            jnp.stack(outs["sp"]), jnp.stack(outs["ss"]))
```

```python
import functools

import jax
import jax.numpy as jnp
from jax import lax
from jax.experimental import pallas as pl
from jax.experimental.pallas import tpu as pltpu

F32 = jnp.float32
BF16 = jnp.bfloat16

EPS = 1e-6
ROPE_BASE = 10000.0
LOG2E = 1.4426950408889634
NEG_BIG = -1e30

V7X_VMEM_LIMIT_BYTES = 60000 * 1024
F32_SUBLANES = 8
BF16_SUBLANES = 16

ROW_TILE = 1024
COL_TILE = 512
ATTN_Q_TILE = 1024
ATTN_K_TILE = 1024
ATTN_K_SUB = 512
ATTN_HEADS_PER_STEP = 2
CUMSUM_TILE = 512
RET_CHUNK = 512
PAGES_PER_STEP = 8


def _params(semantics, vmem_bytes):
    return pltpu.CompilerParams(dimension_semantics=semantics,
                                vmem_limit_bytes=int(min(vmem_bytes, V7X_VMEM_LIMIT_BYTES)))


def _nbytes(shape, dtype):
    n = 1
    for s in shape:
        n *= s
    return n * jnp.dtype(dtype).itemsize


def _norm_mod(x, g, shift, scale):
    y = x * lax.rsqrt(jnp.mean(x * x, axis=-1, keepdims=True) + EPS)
    return (y * g) * (1.0 + scale) + shift


def _silu(x):
    return x * jax.nn.sigmoid(x)


def _log_sigmoid(x):
    return jnp.minimum(x, 0.0) - jnp.log1p(jnp.exp(-jnp.abs(x)))


def _split3(x):
    hi = x.astype(BF16)
    r1 = x - hi.astype(F32)
    mid = r1.astype(BF16)
    lo = (r1 - mid.astype(F32)).astype(BF16)
    return hi, mid, lo


class _WeightRounding:
    def __init__(self, weights, n_steps, step_index):
        self.items = []
        for w, layer in weights:
            slab = w.shape[1] // n_steps
            assert w.shape[1] % n_steps == 0 and slab % BF16_SUBLANES == 0, (w.shape, n_steps)
            self.items.append((w, layer, slab))
        self.step_index = step_index

    def __len__(self):
        return len(self.items)

    def in_specs(self):
        return [pl.BlockSpec((None, slab, w.shape[2]),
                             lambda *ids, layer=layer: (layer, self.step_index(*ids), 0))
                for w, layer, slab in self.items]

    def out_specs(self):
        return [pl.BlockSpec((None, slab, w.shape[2]), lambda *ids: (0, self.step_index(*ids), 0))
                for w, _, slab in self.items]

    def out_shapes(self):
        return [jax.ShapeDtypeStruct((1,) + w.shape[1:], BF16) for w, _, _ in self.items]

    def operands(self):
        return [w for w, _, _ in self.items]

    def vmem_bytes(self):
        return sum(2 * _nbytes((slab, w.shape[2]), F32) + 2 * _nbytes((slab, w.shape[2]), BF16)
                   for w, _, slab in self.items)

    @staticmethod
    def run(in_refs, out_refs):
        for i_ref, o_ref in zip(in_refs, out_refs):
            o_ref[...] = i_ref[...].astype(BF16)


def _mod_spec(rows, width, chunk, tiles_per_group):
    return pl.BlockSpec((None, rows, width), lambda m, n: (m // tiles_per_group, 0, chunk))


def _ada_kernel(c_ref, w_ref, b_ref, o_ref):
    o_ref[...] = jnp.dot(_silu(c_ref[...]), w_ref[...], preferred_element_type=F32) + b_ref[...]


def _ada(c_all, ada_w, ada_b):
    depth, d, n6 = ada_w.shape
    rows = c_all.shape[0]
    tn = min(1024, n6)
    vmem = 2 * (_nbytes((d, tn), F32) + _nbytes((rows, tn), F32)) + _nbytes((rows, d), F32) * 2 + (4 << 20)
    return pl.pallas_call(
        _ada_kernel,
        out_shape=jax.ShapeDtypeStruct((depth, rows, n6), F32),
        grid=(depth, n6 // tn),
        in_specs=[pl.BlockSpec((rows, d), lambda i, n: (0, 0)),
                  pl.BlockSpec((None, d, tn), lambda i, n: (i, 0, n)),
                  pl.BlockSpec((None, 1, tn), lambda i, n: (i, 0, n))],
        out_specs=pl.BlockSpec((None, rows, tn), lambda i, n: (i, 0, n)),
        compiler_params=_params(("arbitrary", "arbitrary"), vmem),
        name="ada_modulation",
    )(c_all, ada_w, ada_b.reshape(depth, 1, n6))


def _fox_proj_kernel(x_ref, xs_ref, g_ref, sh_ref, sc_ref, shs_ref, scs_ref, w_ref, wf_ref, bf_ref,
                     q_ref, kf_ref, vf_ref, kb_ref, vb_ref, lf_ref,
                     qs_ref, kfs_ref, vfs_ref, kbs_ref, vbs_ref, lfs_ref, h_ref, *,
                     nq, tm, q_scale, q_scale_s):
    n = pl.program_id(1)

    @pl.when(n == 0)
    def _():
        w_f = wf_ref[...].astype(BF16)
        for x_r, sh_r, sc_r, lf_r, rows in ((x_ref, sh_ref, sc_ref, lf_ref, slice(0, tm)),
                                           (xs_ref, shs_ref, scs_ref, lfs_ref, slice(tm, None))):
            h = _norm_mod(x_r[...], g_ref[...], sh_r[...], sc_r[...]).astype(BF16)
            h_ref[rows, :] = h
            fl = lax.dot_general(h, w_f, (((1,), (1,)), ((), ())), preferred_element_type=F32)
            lf_r[...] = _log_sigmoid(fl + bf_ref[...])

    z = lax.dot_general(h_ref[...], w_ref[...].astype(BF16), (((1,), (1,)), ((), ())),
                        preferred_element_type=F32)
    zp = z[:tm]
    zs = z[tm:]
    last_tile = pl.program_id(0) == pl.num_programs(0) - 1
    in_q = n < nq
    in_k = jnp.logical_and(n >= nq, n < 2 * nq)
    in_v = n >= 2 * nq

    @pl.when(in_q)
    def _():
        q_ref[...] = (zp * q_scale).astype(BF16)

    @pl.when(in_k)
    def _():
        kf_ref[...] = zp
        kb_ref[...] = zp.astype(BF16)

    @pl.when(in_v)
    def _():
        vf_ref[...] = zp
        vb_ref[...] = zp.astype(BF16)

    @pl.when(jnp.logical_and(last_tile, in_q))
    def _():
        qs_ref[...] = (zs * q_scale_s).astype(BF16)

    @pl.when(jnp.logical_and(last_tile, in_k))
    def _():
        kfs_ref[...] = zs
        kbs_ref[...] = zs.astype(BF16)

    @pl.when(jnp.logical_and(last_tile, in_v))
    def _():
        vfs_ref[...] = zs
        vbs_ref[...] = zs.astype(BF16)


def _fox_proj(x, xs, g, mods_p, mods_s, tiles_per_batch, w_in, layer, b_f, tm, q_scale, q_scale_s):
    m_rows, d = x.shape
    srows = xs.shape[0]
    heads = b_f.shape[1]
    assert w_in.shape[2] - 3 * d == heads and (3 * d) % heads == 0
    w_t = jnp.swapaxes(w_in, 1, 2)
    tn = min(COL_TILE, d)
    nq = d // tn
    kernel = functools.partial(_fox_proj_kernel, nq=nq, tm=tm, q_scale=q_scale, q_scale_s=q_scale_s)

    last_tile = m_rows // tm - 1

    def col(lo, by_tile):
        if by_tile:
            return lambda m, n: (m, jnp.clip(n - lo, 0, nq - 1))
        return lambda m, n: (0, jnp.where(m == last_tile, jnp.clip(n - lo, 0, nq - 1), 0))

    vmem = (2 * _nbytes((tm, d), F32) + _nbytes((tm + srows, d), BF16) + 2 * _nbytes((d, tn), F32)
            + _nbytes((d, tn), BF16) + 2 * (3 * _nbytes((tm, tn), BF16) + 2 * _nbytes((tm, tn), F32))
            + 3 * _nbytes((tm, tn), F32) + (4 << 20))

    def shapes(rows):
        f = jax.ShapeDtypeStruct((rows, d), F32)
        b = jax.ShapeDtypeStruct((rows, d), BF16)
        return (b, f, f, b, b, jax.ShapeDtypeStruct((rows, heads), F32))

    def out_specs(rows, by_tile):
        blk = lambda lo: pl.BlockSpec((rows, tn), col(lo, by_tile))
        last = (lambda m, n: (m, 0)) if by_tile else (lambda m, n: (0, 0))
        return (blk(0), blk(nq), blk(2 * nq), blk(nq), blk(2 * nq), pl.BlockSpec((rows, heads), last))

    outs = pl.pallas_call(
        kernel,
        out_shape=shapes(m_rows) + shapes(srows),
        grid=(m_rows // tm, 3 * nq),
        in_specs=[pl.BlockSpec((tm, d), lambda m, n: (m, 0)),
                  pl.BlockSpec((srows, d), lambda m, n: (0, 0)),
                  pl.BlockSpec((1, d), lambda m, n: (0, 0)),
                  _mod_spec(1, d, 0, tiles_per_batch),
                  _mod_spec(1, d, 1, tiles_per_batch),
                  _mod_spec(srows, d, 0, m_rows),
                  _mod_spec(srows, d, 1, m_rows),
                  pl.BlockSpec((None, tn, d), lambda m, n: (layer, n, 0)),
                  pl.BlockSpec((None, heads, d), lambda m, n: (layer, 3 * d // heads, 0)),
                  pl.BlockSpec((1, heads), lambda m, n: (0, 0))],
        out_specs=out_specs(tm, True) + out_specs(srows, False),
        scratch_shapes=[pltpu.VMEM((tm + srows, d), BF16)],
        compiler_params=_params(("arbitrary", "arbitrary"), vmem),
        name="fox_in_proj",
    )(x, xs, g, mods_p, mods_p, mods_s, mods_s, w_t, w_t, b_f)
    return outs[:6], outs[6:]


def _cumsum_kernel(lf_ref, aug_ref, carry_ref, *, hd):
    @pl.when(pl.program_id(1) == 0)
    def _():
        carry_ref[...] = jnp.zeros_like(carry_ref)

    tc, heads = lf_ref.shape
    row = lax.broadcasted_iota(jnp.int32, (tc, tc), 0)
    col = lax.broadcasted_iota(jnp.int32, (tc, tc), 1)
    tri = jnp.where(col <= row, 1.0, 0.0).astype(BF16)
    c = carry_ref[...]
    for piece in _split3(lf_ref[...]):
        c = c + jnp.dot(tri, piece, preferred_element_type=F32)
    carry_ref[...] = c[tc - 1:tc, :]

    row = lax.broadcasted_iota(jnp.int32, (3 * heads, heads * hd), 0)
    lane = lax.broadcasted_iota(jnp.int32, (3 * heads, heads * hd), 1)
    place = jnp.where(lane == (row % heads) * hd + row // heads, 1.0, 0.0).astype(BF16)
    pieces = jnp.concatenate(_split3(c * (-LOG2E)), axis=1)
    aug_ref[...] = jnp.dot(pieces, place, preferred_element_type=F32).astype(BF16)


def _cumsum(logf, batch, seq, hd):
    rows, heads = logf.shape
    tc = min(CUMSUM_TILE, seq)
    per = seq // tc
    return pl.pallas_call(
        functools.partial(_cumsum_kernel, hd=hd),
        out_shape=jax.ShapeDtypeStruct((rows, heads * hd), BF16),
        grid=(batch, per),
        in_specs=[pl.BlockSpec((tc, heads), lambda b, t: (b * per + t, 0))],
        out_specs=pl.BlockSpec((tc, heads * hd), lambda b, t: (b * per + t, 0)),
        scratch_shapes=[pltpu.VMEM((1, heads), F32)],
        compiler_params=_params(("arbitrary", "arbitrary"), 24 << 20),
        name="fox_logf_cumsum",
    )(logf)


def _fox_attn_kernel(q_ref, k_ref, v_ref, aug_ref, *rest, tq, tk, ts, hd, heads_per_step, n_round):
    o_ref, vt_ref = rest[n_round], rest[-1]
    _WeightRounding.run(rest[:n_round], rest[n_round + 1:-1])
    i = pl.program_id(2)
    seq = k_ref.shape[0]
    q0 = pl.multiple_of(i * tq, tq)
    lane_tile = 128

    @pl.when(i == 0)
    def _():
        for t in range(seq // tk):
            vt_ref[:, t * tk:(t + 1) * tk] = v_ref[t * tk:(t + 1) * tk, :].astype(F32).T.astype(BF16)

    ones = jnp.where(lax.broadcasted_iota(jnp.int32, (tq, hd), 1) < 3, 1.0, 0.0).astype(BF16)
    q_ext = [jnp.concatenate([q_ref[:, hh * hd:(hh + 1) * hd], ones], axis=1)
             for hh in range(heads_per_step)]

    def step(j, carry, masked):
        carry = list(carry)
        items = [(sub, hh) for sub in range(tk // ts) for hh in range(heads_per_step)]
        n_tiles = tq // lane_tile

        def tile_range(sub):
            if masked and tq == tk:
                return sub * ts // lane_tile, min((sub + 1) * ts // lane_tile, n_tiles)
            return 0, (n_tiles if masked else 0)

        def scores(sub, hh):
            k0 = pl.multiple_of(j * tk + sub * ts, ts)
            lanes = slice(hh * hd, (hh + 1) * hd)
            k_ext = jnp.concatenate([k_ref[pl.ds(k0, ts), lanes], aug_ref[pl.ds(k0, ts), lanes]], axis=1)
            return lax.dot_general(k_ext, q_ext[hh][tile_range(sub)[0] * lane_tile:],
                                   (((1,), (1,)), ((), ())), preferred_element_type=F32)

        s_next = scores(*items[0])
        for n, (sub, hh) in enumerate(items):
            s = s_next
            if n + 1 < len(items):
                s_next = scores(*items[n + 1])
            m, l, acc = carry[3 * hh:3 * hh + 3]
            k0 = pl.multiple_of(j * tk + sub * ts, ts)
            first, full = tile_range(sub)
            lo = first * lane_tile
            m_t, l_t, p_t = ([m[:, :lo]], [l[:, :lo]], []) if lo else ([], [], [])
            for qt in range(first, n_tiles):
                ql = slice(qt * lane_tile, (qt + 1) * lane_tile)
                sq = s[:, (qt - first) * lane_tile:(qt - first + 1) * lane_tile]
                if qt < full:
                    keep = (k0 + lax.broadcasted_iota(jnp.int32, (ts, lane_tile), 0)
                            <= q0 + qt * lane_tile + lax.broadcasted_iota(jnp.int32, (ts, lane_tile), 1))
                    sq = jnp.where(keep, sq, NEG_BIG)
                mq = jnp.maximum(m[:, ql], jnp.max(sq, axis=0, keepdims=True))
                pq = jnp.exp2(sq - mq)
                l_t.append(jnp.exp2(m[:, ql] - mq) * l[:, ql] + jnp.sum(pq, axis=0, keepdims=True))
                m_t.append(mq)
                p_t.append(pq.astype(BF16))
            m_new = jnp.concatenate(m_t, axis=1)
            alpha = jnp.exp2(m[:, lo:] - m_new[:, lo:])
            vt = vt_ref[hh * hd:(hh + 1) * hd, pl.ds(k0, ts)]
            acc_hi = alpha * acc[:, lo:] + jnp.dot(vt, jnp.concatenate(p_t, axis=1),
                                                   preferred_element_type=F32)
            carry[3 * hh:3 * hh + 3] = [m_new, jnp.concatenate(l_t, axis=1),
                                        jnp.concatenate([acc[:, :lo], acc_hi], axis=1) if lo else acc_hi]
        return tuple(carry)

    init = (jnp.full((1, tq), NEG_BIG, F32), jnp.zeros((1, tq), F32),
            jnp.zeros((hd, tq), F32)) * heads_per_step
    n_full = q0 // tk
    carry = lax.fori_loop(0, n_full, lambda j, c: step(j, c, False), init)
    carry = step(n_full, carry, True)
    for hh in range(heads_per_step):
        _, l, acc = carry[3 * hh:3 * hh + 3]
        o_ref[:, hh * hd:(hh + 1) * hd] = (acc / l).T.astype(o_ref.dtype)


def _fox_attn(q, k, v, aug, batch, seq, heads, head_dim, weights_to_round):
    tq = min(ATTN_Q_TILE, seq)
    tk = min(ATTN_K_TILE, seq)
    nq = seq // tq
    hps = ATTN_HEADS_PER_STEP if heads % ATTN_HEADS_PER_STEP == 0 else 1
    groups = heads // hps
    width = hps * head_dim
    rounding = _WeightRounding(weights_to_round, batch * groups * nq,
                               lambda b, h, i: (b * groups + h) * nq + i)
    kernel = functools.partial(_fox_attn_kernel, tq=tq, tk=tk, ts=min(ATTN_K_SUB, tk), hd=head_dim,
                               heads_per_step=hps, n_round=len(rounding))
    vmem = (7 * _nbytes((seq, width), BF16) + 6 * hps * _nbytes((tk, tq), F32) + rounding.vmem_bytes()
            + (8 << 20))
    outs = pl.pallas_call(
        kernel,
        out_shape=[jax.ShapeDtypeStruct(q.shape, BF16)] + rounding.out_shapes(),
        grid=(batch, groups, nq),
        in_specs=[pl.BlockSpec((tq, width), lambda b, h, i: (b * nq + i, h)),
                  pl.BlockSpec((seq, width), lambda b, h, i: (b, h)),
                  pl.BlockSpec((seq, width), lambda b, h, i: (b, h)),
                  pl.BlockSpec((seq, width), lambda b, h, i: (b, h))] + rounding.in_specs(),
        out_specs=[pl.BlockSpec((tq, width), lambda b, h, i: (b * nq + i, h))] + rounding.out_specs(),
        scratch_shapes=[pltpu.VMEM((width, seq), BF16)],
        compiler_params=_params(("arbitrary", "arbitrary", "arbitrary"), vmem),
        name="fox_prompt_attention",
    )(q, k, v, aug, *rounding.operands())
    return outs[0], outs[1:]


def _page_suffix_kernel(lf_ref, suf_ref, tot_ref, *, heads):
    x = lf_ref[...]
    cols = x.shape[1]
    col = lax.broadcasted_iota(jnp.int32, x.shape, 1)
    incl = x
    tot = x
    sh = heads
    while sh < cols:
        incl = incl + jnp.where(col + sh < cols, pltpu.roll(incl, cols - sh, 1), 0.0)
        tot = tot + pltpu.roll(tot, sh, 1)
        sh *= 2
    suf_ref[...] = incl - x
    tot_ref[...] = tot


def _page_suffix(cache_lf, heads):
    n_rows, cols = cache_lf.shape
    tr = 256 if n_rows % 256 == 0 else n_rows
    shape = jax.ShapeDtypeStruct((n_rows, cols), F32)
    spec = pl.BlockSpec((tr, cols), lambda r: (r, 0))
    return pl.pallas_call(
        functools.partial(_page_suffix_kernel, heads=heads),
        out_shape=(shape, shape),
        grid=(n_rows // tr,),
        in_specs=[spec],
        out_specs=(spec, spec),
        compiler_params=_params(("arbitrary",), 12 * _nbytes((tr, cols), F32) + (4 << 20)),
        name="fox_page_logf_suffix",
    )(cache_lf)


def _fox_decode_kernel(pt_ref, q_ref, kn_ref, vn_ref, cn_ref, *refs, pages, page_id):
    k_refs = refs[:pages]
    v_refs = refs[pages:2 * pages]
    suf_refs = refs[2 * pages:3 * pages]
    tot_refs = refs[3 * pages:4 * pages]
    o_ref, m_ref, l_ref, acc_ref, run_ref = refs[4 * pages:]
    s_idx = pl.program_id(1)
    heads = q_ref.shape[0]
    cols = k_refs[0].shape[0]
    head_mask = (lax.broadcasted_iota(jnp.int32, (heads, cols), 1) % heads
                 == lax.broadcasted_iota(jnp.int32, (heads, cols), 0))

    @pl.when(s_idx == 0)
    def _():
        m_ref[...] = jnp.full_like(m_ref, NEG_BIG)
        l_ref[...] = jnp.zeros_like(l_ref)
        acc_ref[...] = jnp.zeros_like(acc_ref)
        run_ref[...] = jnp.zeros_like(run_ref)

    q = q_ref[...]
    cn = cn_ref[...]
    run = run_ref[...]
    scores = []
    for g in range(pages):
        s = lax.dot_general(q, k_refs[g][...].astype(BF16), (((1,), (1,)), ((), ())),
                            preferred_element_type=F32)
        row = page_id(pt_ref, pl.program_id(0), s_idx, g) % F32_SUBLANES
        scores.append(jnp.where(head_mask, s + (suf_refs[g][pl.ds(row, 1), :] + (run + cn)), NEG_BIG))
        run = run + tot_refs[g][pl.ds(row, 1), :]
    run_ref[...] = run
    m = m_ref[...]
    m_new = m
    for s in scores:
        m_new = jnp.maximum(m_new, jnp.max(s, axis=-1, keepdims=True))
    alpha = jnp.exp(m - m_new)
    l = alpha * l_ref[...]
    acc = alpha * acc_ref[...]
    for g in range(pages):
        p = jnp.exp(scores[g] - m_new)
        l = l + jnp.sum(p, axis=-1, keepdims=True)
        acc = acc + jnp.dot(p.astype(BF16), v_refs[g][...].astype(BF16), preferred_element_type=F32)
    m_ref[...] = m_new
    l_ref[...] = l
    acc_ref[...] = acc

    @pl.when(s_idx == pl.num_programs(1) - 1)
    def _():
        s_new = jnp.sum(q.astype(F32) * kn_ref[...], axis=-1, keepdims=True)
        m = m_ref[...]
        m_new = jnp.maximum(m, s_new)
        alpha = jnp.exp(m - m_new)
        p_new = jnp.exp(s_new - m_new)
        l = alpha * l_ref[...] + p_new
        acc = alpha * acc_ref[...] + p_new.astype(BF16).astype(F32) * vn_ref[...]
        o_ref[...] = acc / l


def _fox_decode(page_table, pool_base, q, k_new, v_new, logf_new, cache_k, cache_v, cache_lf):
    bs, n_pages = page_table.shape
    _, heads, hd = q.shape
    n_pool, cols, _ = cache_k.shape
    assert n_pool % F32_SUBLANES == 0 and pool_base % F32_SUBLANES == 0
    suffix, total = _page_suffix(cache_lf, heads)
    g_pages = PAGES_PER_STEP if n_pages % PAGES_PER_STEP == 0 else 1
    steps = n_pages // g_pages

    def page_id(pt, b, s, g):
        return pool_base + pt[b * n_pages + (n_pages - 1 - (s * g_pages + g))]

    kernel = functools.partial(_fox_decode_kernel, pages=g_pages, page_id=page_id)

    def page_idx(g):
        return lambda b, s, pt: (page_id(pt, b, s, g), 0, 0)

    def row_block_idx(g):
        return lambda b, s, pt: (page_id(pt, b, s, g) // F32_SUBLANES, 0)

    tok = pl.BlockSpec((None, heads, hd), lambda b, s, pt: (b, 0, 0))
    in_specs = [tok, tok, tok, pl.BlockSpec((None, 1, cols), lambda b, s, pt: (b, 0, 0))]
    in_specs += [pl.BlockSpec((None, cols, hd), page_idx(g)) for g in range(g_pages)]
    in_specs += [pl.BlockSpec((None, cols, hd), page_idx(g)) for g in range(g_pages)]
    in_specs += [pl.BlockSpec((F32_SUBLANES, cols), row_block_idx(g)) for g in range(g_pages)]
    in_specs += [pl.BlockSpec((F32_SUBLANES, cols), row_block_idx(g)) for g in range(g_pages)]
    vmem = (4 * g_pages * _nbytes((cols, hd), F32) + 2 * g_pages * _nbytes((cols, hd), BF16)
            + 4 * g_pages * _nbytes((heads, cols), F32) + (4 << 20))
    grid_spec = pltpu.PrefetchScalarGridSpec(
        num_scalar_prefetch=1,
        grid=(bs, steps),
        in_specs=in_specs,
        out_specs=pl.BlockSpec((None, heads, hd), lambda b, s, pt: (b, 0, 0)),
        scratch_shapes=[pltpu.VMEM((heads, 1), F32), pltpu.VMEM((heads, 1), F32),
                        pltpu.VMEM((heads, hd), F32), pltpu.VMEM((1, cols), F32)])
    return pl.pallas_call(
        kernel,
        out_shape=jax.ShapeDtypeStruct((bs, heads, hd), F32),
        grid_spec=grid_spec,
        compiler_params=_params(("arbitrary", "arbitrary"), vmem),
        name="fox_decode_attention",
    )(page_table.reshape(-1), q, k_new, v_new, logf_new,
      *([cache_k] * g_pages), *([cache_v] * g_pages), *([suffix] * g_pages), *([total] * g_pages))


def _proj_residual_kernel(a_ref, w_ref, x_ref, gate_ref, o_ref):
    z = jnp.dot(a_ref[...], w_ref[...].astype(BF16), preferred_element_type=F32)
    o_ref[...] = x_ref[...] + gate_ref[...] * z


def _proj_residual(a, w, layer, x, mods, mod_rows, tiles_per_group, gate_chunk, tm):
    m_rows, k = a.shape
    d = w.shape[2]
    tn = min(COL_TILE, d)
    vmem = (2 * _nbytes((tm, k), BF16) + 2 * _nbytes((k, tn), F32) + _nbytes((k, tn), BF16)
            + 5 * _nbytes((tm, tn), F32) + (4 << 20))
    return pl.pallas_call(
        _proj_residual_kernel,
        out_shape=jax.ShapeDtypeStruct((m_rows, d), F32),
        grid=(m_rows // tm, d // tn),
        in_specs=[pl.BlockSpec((tm, k), lambda m, n: (m, 0)),
                  pl.BlockSpec((None, k, tn), lambda m, n: (layer, 0, n)),
                  pl.BlockSpec((tm, tn), lambda m, n: (m, n)),
                  pl.BlockSpec((None, mod_rows, tn),
                               lambda m, n: (m // tiles_per_group, 0, gate_chunk * (d // tn) + n))],
        out_specs=pl.BlockSpec((tm, tn), lambda m, n: (m, n)),
        compiler_params=_params(("arbitrary", "arbitrary"), vmem),
        name="mixer_out_proj",
    )(a, w, x, mods)


def _mlp_kernel(x_ref, xs_ref, g_ref, sh_ref, sc_ref, gate_ref, shs_ref, scs_ref, gates_ref,
                wu_ref, wd_ref, fg_ref, o_ref, os_ref, h_ref, *, tm, final_norm):
    f = pl.program_id(1)

    @pl.when(f == 0)
    def _():
        h_ref[:tm, :] = _norm_mod(x_ref[...], g_ref[...], sh_ref[...], sc_ref[...]).astype(BF16)
        h_ref[tm:, :] = _norm_mod(xs_ref[...], g_ref[...], shs_ref[...], scs_ref[...]).astype(BF16)
        o_ref[...] = jnp.zeros_like(o_ref)
        os_ref[...] = jnp.zeros_like(os_ref)

    u = jnp.dot(h_ref[...], wu_ref[...].astype(BF16), preferred_element_type=F32)
    u = jnp.square(jnp.maximum(u, 0.0)).astype(BF16)
    z = jnp.dot(u, wd_ref[...].astype(BF16), preferred_element_type=F32)
    o_ref[...] += z[:tm]
    os_ref[...] += z[tm:]

    @pl.when(f == pl.num_programs(1) - 1)
    def _():
        for x_r, gate_r, o_r in ((x_ref, gate_ref, o_ref), (xs_ref, gates_ref, os_ref)):
            y = x_r[...] + gate_r[...] * o_r[...]
            if final_norm:
                y = y * lax.rsqrt(jnp.mean(y * y, axis=-1, keepdims=True) + EPS) * fg_ref[...]
            o_r[...] = y


def _mlp(x, xs, g, mods_p, mods_s, tiles_per_batch, w_up, w_down, layer, final_g, final_norm, tm):
    m_rows, d = x.shape
    srows = xs.shape[0]
    d_ff = w_up.shape[2]
    tf = min(COL_TILE, d_ff)
    kernel = functools.partial(_mlp_kernel, tm=tm, final_norm=final_norm)
    x_buffers = 1 if w_up.dtype == F32 else 2
    x_mode = {"pipeline_mode": pl.Buffered(1)} if x_buffers == 1 else {}
    vmem = ((2 + x_buffers) * _nbytes((tm, d), F32) + _nbytes((tm + srows, d), BF16)
            + 4 * _nbytes((d, tf), w_up.dtype) + 2 * _nbytes((d, tf), BF16) + 4 * _nbytes((tm, tf), F32)
            + (4 << 20))
    full = lambda m, f: (0, 0)
    vec = pl.BlockSpec((1, d), full)
    return pl.pallas_call(
        kernel,
        out_shape=(jax.ShapeDtypeStruct((m_rows, d), F32), jax.ShapeDtypeStruct((srows, d), F32)),
        grid=(m_rows // tm, d_ff // tf),
        in_specs=[pl.BlockSpec((tm, d), lambda m, f: (m, 0), **x_mode),
                  pl.BlockSpec((srows, d), full),
                  vec,
                  _mod_spec(1, d, 3, tiles_per_batch),
                  _mod_spec(1, d, 4, tiles_per_batch),
                  _mod_spec(1, d, 5, tiles_per_batch),
                  _mod_spec(srows, d, 3, m_rows),
                  _mod_spec(srows, d, 4, m_rows),
                  _mod_spec(srows, d, 5, m_rows),
                  pl.BlockSpec((None, d, tf), lambda m, f: (layer, 0, f)),
                  pl.BlockSpec((None, tf, d), lambda m, f: (layer, f, 0)),
                  vec],
        out_specs=(pl.BlockSpec((tm, d), lambda m, f: (m, 0)), pl.BlockSpec((srows, d), full)),
        scratch_shapes=[pltpu.VMEM((tm + srows, d), BF16)],
        compiler_params=_params(("arbitrary", "arbitrary"), vmem),
        name="relu2_mlp",
    )(x, xs, g, mods_p, mods_p, mods_p, mods_s, mods_s, mods_s, w_up, w_down, final_g)


def _ret_proj_kernel(x_ref, xs_ref, g_ref, sh_ref, sc_ref, shs_ref, scs_ref, w_ref,
                     cos_ref, sin_ref, coss_ref, sins_ref,
                     q_ref, k_ref, v_ref, gt_ref, qs_ref, ks_ref, vs_ref, gts_ref, h_ref, *,
                     nq, tm, key_dim, k_scale):
    n = pl.program_id(1)

    @pl.when(n == 0)
    def _():
        h_ref[:tm, :] = _norm_mod(x_ref[...], g_ref[...], sh_ref[...], sc_ref[...]).astype(BF16)
        h_ref[tm:, :] = _norm_mod(xs_ref[...], g_ref[...], shs_ref[...], scs_ref[...]).astype(BF16)

    z = jnp.dot(h_ref[...], w_ref[...].astype(BF16), preferred_element_type=F32)
    zp = z[:tm]
    zs = z[tm:]
    half = key_dim // 2

    def rotate(zz, c_ref, s_ref):
        cos = c_ref[...]
        sin = s_ref[...]
        parts = []
        for hh in range(zz.shape[1] // key_dim):
            x1 = zz[:, hh * key_dim:hh * key_dim + half]
            x2 = zz[:, hh * key_dim + half:(hh + 1) * key_dim]
            parts += [x1 * cos - x2 * sin, x1 * sin + x2 * cos]
        return jnp.concatenate(parts, axis=1)

    last_tile = pl.program_id(0) == pl.num_programs(0) - 1
    in_q = n < nq
    in_k = jnp.logical_and(n >= nq, n < 2 * nq)
    in_v = jnp.logical_and(n >= 2 * nq, n < 4 * nq)
    in_g = n >= 4 * nq

    @pl.when(in_q)
    def _():
        q_ref[...] = rotate(zp, cos_ref, sin_ref).astype(BF16)

    @pl.when(in_k)
    def _():
        k_ref[...] = rotate(zp, cos_ref, sin_ref) * k_scale

    @pl.when(in_v)
    def _():
        v_ref[...] = zp.astype(BF16)

    @pl.when(in_g)
    def _():
        gt_ref[...] = zp

    @pl.when(jnp.logical_and(last_tile, in_q))
    def _():
        qs_ref[...] = rotate(zs, coss_ref, sins_ref).astype(BF16)

    @pl.when(jnp.logical_and(last_tile, in_k))
    def _():
        ks_ref[...] = rotate(zs, coss_ref, sins_ref) * k_scale

    @pl.when(jnp.logical_and(last_tile, in_v))
    def _():
        vs_ref[...] = zs.astype(BF16)

    @pl.when(jnp.logical_and(last_tile, in_g))
    def _():
        gts_ref[...] = zs


def _ret_proj(x, xs, g, mods_p, mods_s, tiles_per_batch, w_in, layer, cos, sin, cos_s, sin_s, key_dim, tm):
    m_rows, d = x.shape
    srows = xs.shape[0]
    tn = min(COL_TILE, d)
    nq = d // tn
    half = key_dim // 2
    kernel = functools.partial(_ret_proj_kernel, nq=nq, tm=tm, key_dim=key_dim, k_scale=key_dim ** -0.5)

    last_tile = m_rows // tm - 1

    def col(lo, width, by_tile):
        if by_tile:
            return lambda m, n: (m, jnp.clip(n - lo, 0, width - 1))
        return lambda m, n: (0, jnp.where(m == last_tile, jnp.clip(n - lo, 0, width - 1), 0))

    vmem = (2 * _nbytes((tm, d), F32) + _nbytes((tm + srows, d), BF16) + 2 * _nbytes((d, tn), F32)
            + _nbytes((d, tn), BF16) + 2 * (2 * _nbytes((tm, tn), BF16) + 2 * _nbytes((tm, tn), F32))
            + 4 * _nbytes((tm, half), F32) + 4 * _nbytes((tm, tn), F32) + (4 << 20))

    def shapes(rows):
        return (jax.ShapeDtypeStruct((rows, d), BF16), jax.ShapeDtypeStruct((rows, d), F32),
                jax.ShapeDtypeStruct((rows, 2 * d), BF16), jax.ShapeDtypeStruct((rows, 2 * d), F32))

    def out_specs(rows, by_tile):
        return (pl.BlockSpec((rows, tn), col(0, nq, by_tile)),
                pl.BlockSpec((rows, tn), col(nq, nq, by_tile)),
                pl.BlockSpec((rows, tn), col(2 * nq, 2 * nq, by_tile)),
                pl.BlockSpec((rows, tn), col(4 * nq, 2 * nq, by_tile)))

    full = lambda m, n: (0, 0)
    outs = pl.pallas_call(
        kernel,
        out_shape=shapes(m_rows) + shapes(srows),
        grid=(m_rows // tm, 6 * nq),
        in_specs=[pl.BlockSpec((tm, d), lambda m, n: (m, 0)),
                  pl.BlockSpec((srows, d), full),
                  pl.BlockSpec((1, d), full),
                  _mod_spec(1, d, 0, tiles_per_batch),
                  _mod_spec(1, d, 1, tiles_per_batch),
                  _mod_spec(srows, d, 0, m_rows),
                  _mod_spec(srows, d, 1, m_rows),
                  pl.BlockSpec((None, d, tn), lambda m, n: (layer, 0, n)),
                  pl.BlockSpec((tm, half), lambda m, n: (m % tiles_per_batch, 0)),
                  pl.BlockSpec((tm, half), lambda m, n: (m % tiles_per_batch, 0)),
                  pl.BlockSpec((srows, half), full),
                  pl.BlockSpec((srows, half), full)],
        out_specs=out_specs(tm, True) + out_specs(srows, False),
        scratch_shapes=[pltpu.VMEM((tm + srows, d), BF16)],
        compiler_params=_params(("arbitrary", "arbitrary"), vmem),
        name="ret_in_proj",
    )(x, xs, g, mods_p, mods_p, mods_s, mods_s, w_in, cos, sin, cos_s, sin_s)
    return outs[:4], outs[4:]


def _group_norm_gate(o, gate):
    mu = jnp.mean(o, axis=-1, keepdims=True)
    cen = o - mu
    var = jnp.mean(cen * cen, axis=-1, keepdims=True)
    return _silu(gate) * (cen * lax.rsqrt(var + EPS))


def _ret_chunk_kernel(lg_ref, q_ref, k_ref, v_ref, gt_ref, *rest, n_round):
    y_ref, s_ref = rest[n_round:n_round + 2]
    dec_ref, xi_ref, zeta_ref = rest[-3:]
    _WeightRounding.run(rest[:n_round], rest[n_round + 2:-3])
    lg = jnp.full((1, 1), lg_ref[pl.program_id(1)], F32)
    ln = q_ref.shape[0]

    @pl.when(pl.program_id(2) == 0)
    def _():
        s_ref[...] = jnp.zeros_like(s_ref)
        diff = (lax.broadcasted_iota(jnp.int32, (ln, ln), 0)
                - lax.broadcasted_iota(jnp.int32, (ln, ln), 1)).astype(F32)
        dec_ref[...] = jnp.where(diff >= 0, jnp.exp(lg * jnp.maximum(diff, 0.0)), 0.0)
        n = lax.broadcasted_iota(jnp.int32, (ln, 1), 0).astype(F32)
        xi_ref[...] = jnp.exp(lg * (n + 1.0))
        zeta_ref[...] = jnp.exp(lg * (ln - 1.0 - n))

    q = q_ref[...]
    k = k_ref[...]
    v = v_ref[...]
    s0 = s_ref[...]
    qk = lax.dot_general(q, k.astype(BF16), (((1,), (1,)), ((), ())), preferred_element_type=F32)
    a = (qk * dec_ref[...]).astype(BF16)
    o = (jnp.dot(a, v, preferred_element_type=F32)
         + jnp.dot(q, s0.astype(BF16), preferred_element_type=F32) * xi_ref[...])
    kz = (k * zeta_ref[...]).astype(BF16)
    s_ref[...] = jnp.exp(lg * ln) * s0 + lax.dot_general(
        kz, v, (((0,), (0,)), ((), ())), preferred_element_type=F32)
    y_ref[...] = _group_norm_gate(o, gt_ref[...]).astype(y_ref.dtype)


def _ret_prompt(lg, q, k, v, gate, batch, seq, heads, key_dim, val_dim, weights_to_round):
    ln = min(RET_CHUNK, seq)
    nc = seq // ln
    rounding = _WeightRounding(weights_to_round, batch * heads * nc,
                               lambda b, h, c, *_: (b * heads + h) * nc + c)
    vmem = (2 * (_nbytes((ln, key_dim), BF16) + _nbytes((ln, key_dim), F32) + _nbytes((ln, val_dim), BF16)
                 + _nbytes((ln, val_dim), F32) + _nbytes((ln, val_dim), BF16))
            + 4 * _nbytes((key_dim, val_dim), F32) + 8 * _nbytes((ln, val_dim), F32)
            + rounding.vmem_bytes() + (8 << 20))
    grid_spec = pltpu.PrefetchScalarGridSpec(
        num_scalar_prefetch=1,
        grid=(batch, heads, nc),
        in_specs=[pl.BlockSpec((ln, key_dim), lambda b, h, c, lg: (b * nc + c, h)),
                  pl.BlockSpec((ln, key_dim), lambda b, h, c, lg: (b * nc + c, h)),
                  pl.BlockSpec((ln, val_dim), lambda b, h, c, lg: (b * nc + c, h)),
                  pl.BlockSpec((ln, val_dim), lambda b, h, c, lg: (b * nc + c, h))] + rounding.in_specs(),
        out_specs=[pl.BlockSpec((ln, val_dim), lambda b, h, c, lg: (b * nc + c, h)),
                   pl.BlockSpec((None, None, key_dim, val_dim), lambda b, h, c, lg: (b, h, 0, 0))]
        + rounding.out_specs(),
        scratch_shapes=[pltpu.VMEM((ln, ln), F32), pltpu.VMEM((ln, 1), F32), pltpu.VMEM((ln, 1), F32)])
    outs = pl.pallas_call(
        functools.partial(_ret_chunk_kernel, n_round=len(rounding)),
        out_shape=[jax.ShapeDtypeStruct((batch * seq, heads * val_dim), BF16),
                   jax.ShapeDtypeStruct((batch, heads, key_dim, val_dim), F32)] + rounding.out_shapes(),
        grid_spec=grid_spec,
        compiler_params=_params(("arbitrary", "arbitrary", "arbitrary"), vmem),
        name="ret_prompt_chunks",
    )(lg, q, k, v, gate, *rounding.operands())
    return outs[0], outs[1], outs[2:]


def _ret_decode_kernel(lg_ref, q_ref, k_ref, v_ref, gt_ref, s0_ref, y_ref, s_ref):
    q = q_ref[...]
    k = k_ref[...]
    q_cols = q.T
    k_cols = k.T
    a_all = jnp.sum(q * k, axis=-1, keepdims=True)
    for h in range(s0_ref.shape[0]):
        gamma = jnp.exp(jnp.full((1, 1), lg_ref[h], F32))
        s0 = s0_ref[h]
        v = v_ref[h:h + 1, :]
        qs = jnp.sum(q_cols[:, h:h + 1] * s0, axis=0, keepdims=True)
        o = a_all[h:h + 1, :] * v + qs * gamma
        s_ref[h] = gamma * s0 + k_cols[:, h:h + 1] * v
        y_ref[h:h + 1, :] = _group_norm_gate(o, gt_ref[h:h + 1, :])


def _ret_decode(lg, q, k, v, gate, state):
    bs, heads, key_dim, val_dim = state.shape

    def vec(width):
        return pl.BlockSpec((None, heads, width), lambda b, lg: (b, 0, 0))

    state_spec = pl.BlockSpec((None, heads, key_dim, val_dim), lambda b, lg: (b, 0, 0, 0))
    grid_spec = pltpu.PrefetchScalarGridSpec(
        num_scalar_prefetch=1,
        grid=(bs,),
        in_specs=[vec(key_dim), vec(key_dim), vec(val_dim), vec(val_dim), state_spec],
        out_specs=(vec(val_dim), state_spec))
    y, s = pl.pallas_call(
        _ret_decode_kernel,
        out_shape=(jax.ShapeDtypeStruct((bs, heads, val_dim), F32),
                   jax.ShapeDtypeStruct(state.shape, F32)),
        grid_spec=grid_spec,
        compiler_params=_params(("arbitrary",), 4 * _nbytes(state.shape[1:], F32) + (16 << 20)),
        name="ret_decode_step",
    )(lg, q.reshape(bs, heads, key_dim), k.reshape(bs, heads, key_dim),
      v.reshape(bs, heads, val_dim), gate.reshape(bs, heads, val_dim), state)
    return y.reshape(bs, heads * val_dim), s


def _rotary_tables(pos, key_dim):
    half = key_dim // 2
    inv = 1.0 / (ROPE_BASE ** jnp.linspace(0.0, 1.0, half, dtype=F32))
    ang = pos.astype(F32)[:, None] * inv[None, :]
    return jnp.cos(ang), jnp.sin(ang)


def _pad_rows(a, rows):
    return jnp.pad(a, ((0, rows - a.shape[0]), (0, 0)))


def kernel(x_prompt, x_sample, c_prompt, c_sample, cache_fox_k, cache_fox_v, cache_fox_logf, state_ret, page_table, norm1_g, norm2_g, ada_w, ada_b, fox_w_in, fox_b_f, fox_w_out, ret_w_in, ret_w_out, mlp_w_up, mlp_w_down, final_g):
    bp, tp, d = x_prompt.shape
    bs, ts, _ = x_sample.shape
    assert ts == 1, "the decode kernels handle one new token per sequence"
    depth = ada_w.shape[0]
    _, n_pool, page, fox_heads, fox_hd = cache_fox_k.shape
    _, _, ret_heads, key_dim, val_dim = state_ret.shape
    past_len = page_table.shape[1] * page
    srows = -(-bs // BF16_SUBLANES) * BF16_SUBLANES
    tm = min(ROW_TILE, tp)
    tiles_p = tp // tm

    mod_rows = -(-(bs + bp) // 8) * 8
    c_all = _pad_rows(jnp.concatenate([c_sample, c_prompt], axis=0), max(mod_rows, srows))
    mods = _ada(c_all, ada_w, ada_b)

    lg = jnp.log1p(-jnp.exp2(-5.0 - jnp.arange(ret_heads, dtype=F32)))
    cos_p, sin_p = _rotary_tables(jnp.arange(tp), key_dim)
    cos_s, sin_s = _rotary_tables(jnp.full((srows,), past_len), key_dim)

    yp = x_prompt.reshape(bp * tp, d)
    ys = _pad_rows(x_sample.reshape(bs, d), srows)
    fg = final_g.reshape(1, d)
    outs = {name: [] for name in ("kp", "vp", "lfp", "ks", "vs", "lfs", "sp", "ss")}

    ret_rounded = None
    for i in range(depth):
        mods_p = mods[i, bs:bs + bp].reshape(bp, 1, 6 * d)
        mods_s = mods[i, :srows].reshape(1, srows, 6 * d)
        g1 = norm1_g[i].reshape(1, d)
        g2 = norm2_g[i].reshape(1, d)
        j = i // 2
        if i % 2 == 0:
            b_f = fox_b_f[j].reshape(1, fox_heads)
            (q, kf, vf, kb, vb, lf), sample = _fox_proj(
                yp, ys, g1, mods_p, mods_s, tiles_p, fox_w_in, j, b_f, tm,
                fox_hd ** -0.5 * LOG2E, fox_hd ** -0.5)
            aug = _cumsum(lf, bp, tp, fox_hd)
            to_round = [(fox_w_out, j), (mlp_w_up, i), (mlp_w_down, i)]
            if i + 1 < depth:
                jr = (i + 1) // 2
                to_round += [(ret_w_in, jr), (ret_w_out, jr), (mlp_w_up, i + 1), (mlp_w_down, i + 1)]
            o, rounded = _fox_attn(q, kb, vb, aug, bp, tp, fox_heads, fox_hd, to_round)
            w_out, w_up, w_down = rounded[:3]
            l_mlp = 0
            ret_rounded = rounded[3:] if i + 1 < depth else None
            yp = _proj_residual(o, w_out, 0, yp, mods_p, 1, tiles_p, 2, tm)
            outs["kp"].append(kf.reshape(bp, tp, fox_heads, fox_hd))
            outs["vp"].append(vf.reshape(bp, tp, fox_heads, fox_hd))
            outs["lfp"].append(lf.reshape(bp, tp, fox_heads))
            q, kf, vf, kb, vb, lf = sample
            o = _fox_decode(
                page_table, j * n_pool,
                q[:bs].reshape(bs, fox_heads, fox_hd), kb[:bs].astype(F32).reshape(bs, fox_heads, fox_hd),
                vb[:bs].astype(F32).reshape(bs, fox_heads, fox_hd),
                jnp.tile(lf[:bs], (1, page)).reshape(bs, 1, page * fox_heads),
                cache_fox_k.reshape(-1, page * fox_heads, fox_hd),
                cache_fox_v.reshape(-1, page * fox_heads, fox_hd),
                cache_fox_logf.reshape(-1, page * fox_heads))
            o = _pad_rows(o.reshape(bs, d), srows).astype(BF16)
            ys = _proj_residual(o, w_out, 0, ys, mods_s, srows, 1, 2, srows)
            outs["ks"].append(kf[:bs].reshape(bs, ts, fox_heads, fox_hd))
            outs["vs"].append(vf[:bs].reshape(bs, ts, fox_heads, fox_hd))
            outs["lfs"].append(lf[:bs].reshape(bs, ts, fox_heads))
        else:
            if ret_rounded is None:
                w_in, w_out, w_up, w_down = ret_w_in, ret_w_out, mlp_w_up, mlp_w_down
                l_mix, l_mlp = j, i
            else:
                w_in, w_out, w_up, w_down = ret_rounded
                l_mix = l_mlp = 0
            (q, k, v, gate), sample = _ret_proj(yp, ys, g1, mods_p, mods_s, tiles_p, w_in, l_mix,
                                                cos_p, sin_p, cos_s, sin_s, key_dim, tm)
            y, s, _ = _ret_prompt(lg, q, k, v, gate, bp, tp, ret_heads, key_dim, val_dim, [])
            yp = _proj_residual(y, w_out, l_mix, yp, mods_p, 1, tiles_p, 2, tm)
            outs["sp"].append(s)
            q, k, v, gate = sample
            y, s = _ret_decode(lg, q[:bs].astype(F32), k[:bs], v[:bs].astype(F32), gate[:bs],
                               state_ret[j].astype(F32))
            ys = _proj_residual(_pad_rows(y, srows).astype(BF16), w_out, l_mix, ys, mods_s, srows, 1, 2, srows)
            outs["ss"].append(s)
        yp, ys = _mlp(yp, ys, g2, mods_p, mods_s, tiles_p, w_up, w_down, l_mlp, fg, i == depth - 1, tm)

    return (yp.reshape(bp, tp, d), ys[:bs].reshape(bs, ts, d),
            jnp.stack(outs["kp"]), jnp.stack(outs["vp"]), jnp.stack(outs["lfp"]),
            jnp.stack(outs["ks"]), jnp.stack(outs["vs"]), jnp.stack(outs["lfs"]),
            jnp.stack(outs["sp"]), jnp.stack(outs["ss"]))
```

```python
import functools

import jax
import jax.numpy as jnp
import numpy as np
from jax import lax
from jax.experimental import pallas as pl
from jax.experimental.pallas import tpu as pltpu

F32 = jnp.float32
BF16 = jnp.bfloat16

EPS = 1e-6
ROPE_BASE = 10000.0
LOG2E = 1.4426950408889634
NEG_BIG = -1e30

V7X_VMEM_LIMIT_BYTES = 60000 * 1024
F32_SUBLANES = 8
BF16_SUBLANES = 16

ROW_TILE = 1024
COL_TILE = 512
ATTN_Q_TILE = 1024
ATTN_K_TILE = 1024
ATTN_K_SUB = 512
ATTN_HEADS_PER_STEP = 2
CUMSUM_TILE = 512
RET_CHUNK = 512
PAGES_PER_STEP = 8


def _params(semantics, vmem_bytes):
    return pltpu.CompilerParams(dimension_semantics=semantics,
                                vmem_limit_bytes=int(min(vmem_bytes, V7X_VMEM_LIMIT_BYTES)))


def _nbytes(shape, dtype):
    n = 1
    for s in shape:
        n *= s
    return n * jnp.dtype(dtype).itemsize


def _norm_mod(x, g, shift, scale):
    y = x * lax.rsqrt(jnp.mean(x * x, axis=-1, keepdims=True) + EPS)
    return (y * g) * (1.0 + scale) + shift


def _silu(x):
    return x * jax.nn.sigmoid(x)


def _log_sigmoid(x):
    return jnp.minimum(x, 0.0) - jnp.log1p(jnp.exp(-jnp.abs(x)))


def _split3(x):
    hi = x.astype(BF16)
    r1 = x - hi.astype(F32)
    mid = r1.astype(BF16)
    lo = (r1 - mid.astype(F32)).astype(BF16)
    return hi, mid, lo


class _WeightRounding:
    def __init__(self, weights, n_steps, step_index):
        self.items = []
        for w, layer in weights:
            slab = w.shape[1] // n_steps
            assert w.shape[1] % n_steps == 0 and slab % BF16_SUBLANES == 0, (w.shape, n_steps)
            self.items.append((w, layer, slab))
        self.step_index = step_index

    def __len__(self):
        return len(self.items)

    def in_specs(self):
        return [pl.BlockSpec((None, slab, w.shape[2]),
                             lambda *ids, layer=layer: (layer, self.step_index(*ids), 0))
                for w, layer, slab in self.items]

    def out_specs(self):
        return [pl.BlockSpec((None, slab, w.shape[2]), lambda *ids: (0, self.step_index(*ids), 0))
                for w, _, slab in self.items]

    def out_shapes(self):
        return [jax.ShapeDtypeStruct((1,) + w.shape[1:], BF16) for w, _, _ in self.items]

    def operands(self):
        return [w for w, _, _ in self.items]

    def vmem_bytes(self):
        return sum(2 * _nbytes((slab, w.shape[2]), F32) + 2 * _nbytes((slab, w.shape[2]), BF16)
                   for w, _, slab in self.items)

    @staticmethod
    def run(in_refs, out_refs):
        for i_ref, o_ref in zip(in_refs, out_refs):
            o_ref[...] = i_ref[...].astype(BF16)


def _mod_spec(rows, width, chunk, tiles_per_group, n_chunks=1):
    assert chunk % n_chunks == 0
    return pl.BlockSpec((None, rows, n_chunks * width),
                        lambda m, n: (m // tiles_per_group, 0, chunk // n_chunks))


def _ada_kernel(c_ref, w_ref, b_ref, o_ref):
    o_ref[...] = jnp.dot(_silu(c_ref[...]), w_ref[...], preferred_element_type=F32) + b_ref[...]


def _ada(c_all, ada_w, ada_b):
    depth, d, n6 = ada_w.shape
    rows = c_all.shape[0]
    tn = min(1024, n6)
    vmem = 2 * (_nbytes((d, tn), F32) + _nbytes((rows, tn), F32)) + _nbytes((rows, d), F32) * 2 + (4 << 20)
    return pl.pallas_call(
        _ada_kernel,
        out_shape=jax.ShapeDtypeStruct((depth, rows, n6), F32),
        grid=(depth, n6 // tn),
        in_specs=[pl.BlockSpec((rows, d), lambda i, n: (0, 0)),
                  pl.BlockSpec((None, d, tn), lambda i, n: (i, 0, n)),
                  pl.BlockSpec((None, 1, tn), lambda i, n: (i, 0, n))],
        out_specs=pl.BlockSpec((None, rows, tn), lambda i, n: (i, 0, n)),
        compiler_params=_params(("arbitrary", "arbitrary"), vmem),
        name="ada_modulation",
    )(c_all, ada_w, ada_b.reshape(depth, 1, n6))


def _fox_proj_kernel(x_ref, xs_ref, g_ref, mod_ref, mods_ref, w_ref, wf_ref, bf_ref,
                     q_ref, kf_ref, vf_ref, kb_ref, vb_ref, lf_ref, zs_ref, lfs_ref, h_ref, *,
                     nq, tm, q_scale, q_scale_s):
    n = pl.program_id(1)
    d = x_ref.shape[1]

    @pl.when(n == 0)
    def _():
        w_f = wf_ref[...].astype(BF16)
        for x_r, mod_r, lf_r, rows in ((x_ref, mod_ref, lf_ref, slice(0, tm)),
                                       (xs_ref, mods_ref, lfs_ref, slice(tm, None))):
            h = _norm_mod(x_r[...], g_ref[...], mod_r[:, :d], mod_r[:, d:]).astype(BF16)
            h_ref[rows, :] = h
            fl = lax.dot_general(h, w_f, (((1,), (1,)), ((), ())), preferred_element_type=F32)
            lf_r[...] = _log_sigmoid(fl + bf_ref[...])

    z = lax.dot_general(h_ref[...], w_ref[...].astype(BF16), (((1,), (1,)), ((), ())),
                        preferred_element_type=F32)
    zp = z[:tm]
    zs = z[tm:]
    in_q = n < nq

    @pl.when(in_q)
    def _():
        q_ref[...] = (zp * q_scale).astype(BF16)

    @pl.when(jnp.logical_and(n >= nq, n < 2 * nq))
    def _():
        kf_ref[...] = zp
        kb_ref[...] = zp.astype(BF16)

    @pl.when(n >= 2 * nq)
    def _():
        vf_ref[...] = zp
        vb_ref[...] = zp.astype(BF16)

    @pl.when(pl.program_id(0) == pl.num_programs(0) - 1)
    def _():
        zs_ref[...] = zs * jnp.where(in_q, q_scale_s, 1.0)


def _fox_proj(x, xs, g, mods_p, mods_s, tiles_per_batch, w_in, layer, b_f, tm, q_scale, q_scale_s):
    m_rows, d = x.shape
    srows = xs.shape[0]
    heads = b_f.shape[1]
    assert w_in.shape[2] - 3 * d == heads and (3 * d) % heads == 0
    w_t = jnp.swapaxes(w_in, 1, 2)
    tn = min(COL_TILE, d)
    nq = d // tn
    kernel = functools.partial(_fox_proj_kernel, nq=nq, tm=tm, q_scale=q_scale, q_scale_s=q_scale_s)

    last_tile = m_rows // tm - 1

    def col(lo):
        return lambda m, n: (m, jnp.clip(n - lo, 0, nq - 1))

    vmem = (2 * _nbytes((tm, d), F32) + _nbytes((tm + srows, d), BF16) + 2 * _nbytes((d, tn), F32)
            + _nbytes((d, tn), BF16) + 2 * (3 * _nbytes((tm, tn), BF16) + 2 * _nbytes((tm, tn), F32))
            + 3 * _nbytes((tm, tn), F32) + (4 << 20))
    out_f = jax.ShapeDtypeStruct((m_rows, d), F32)
    out_b = jax.ShapeDtypeStruct((m_rows, d), BF16)
    full = lambda m, n: (0, 0)
    outs = pl.pallas_call(
        kernel,
        out_shape=(out_b, out_f, out_f, out_b, out_b, jax.ShapeDtypeStruct((m_rows, heads), F32),
                   jax.ShapeDtypeStruct((srows, 3 * d), F32), jax.ShapeDtypeStruct((srows, heads), F32)),
        grid=(m_rows // tm, 3 * nq),
        in_specs=[pl.BlockSpec((tm, d), lambda m, n: (m, 0)),
                  pl.BlockSpec((srows, d), full),
                  pl.BlockSpec((1, d), full),
                  _mod_spec(1, d, 0, tiles_per_batch, 2),
                  _mod_spec(srows, d, 0, m_rows, 2),
                  pl.BlockSpec((None, tn, d), lambda m, n: (layer, n, 0)),
                  pl.BlockSpec((None, heads, d), lambda m, n: (layer, 3 * d // heads, 0)),
                  pl.BlockSpec((1, heads), full)],
        out_specs=(pl.BlockSpec((tm, tn), col(0)),
                   pl.BlockSpec((tm, tn), col(nq)),
                   pl.BlockSpec((tm, tn), col(2 * nq)),
                   pl.BlockSpec((tm, tn), col(nq)),
                   pl.BlockSpec((tm, tn), col(2 * nq)),
                   pl.BlockSpec((tm, heads), lambda m, n: (m, 0)),
                   pl.BlockSpec((srows, tn), lambda m, n: (0, jnp.where(m == last_tile, n, 0))),
                   pl.BlockSpec((srows, heads), full)),
        scratch_shapes=[pltpu.VMEM((tm + srows, d), BF16)],
        compiler_params=_params(("arbitrary", "arbitrary"), vmem),
        name="fox_in_proj",
    )(x, xs, g, mods_p, mods_s, w_t, w_t, b_f)
    return outs[:6], outs[6:]


def _cumsum_kernel(lf_ref, aug_ref, carry_ref, *, hd):
    @pl.when(pl.program_id(1) == 0)
    def _():
        carry_ref[...] = jnp.zeros_like(carry_ref)

    tc, heads = lf_ref.shape
    row = lax.broadcasted_iota(jnp.int32, (tc, tc), 0)
    col = lax.broadcasted_iota(jnp.int32, (tc, tc), 1)
    tri = jnp.where(col <= row, 1.0, 0.0).astype(BF16)
    c = carry_ref[...]
    for piece in _split3(lf_ref[...]):
        c = c + jnp.dot(tri, piece, preferred_element_type=F32)
    carry_ref[...] = c[tc - 1:tc, :]

    row = lax.broadcasted_iota(jnp.int32, (3 * heads, heads * hd), 0)
    lane = lax.broadcasted_iota(jnp.int32, (3 * heads, heads * hd), 1)
    place = jnp.where(lane == (row % heads) * hd + row // heads, 1.0, 0.0).astype(BF16)
    pieces = jnp.concatenate(_split3(c * (-LOG2E)), axis=1)
    aug_ref[...] = jnp.dot(pieces, place, preferred_element_type=F32).astype(BF16)


def _cumsum(logf, batch, seq, hd):
    rows, heads = logf.shape
    tc = min(CUMSUM_TILE, seq)
    per = seq // tc
    return pl.pallas_call(
        functools.partial(_cumsum_kernel, hd=hd),
        out_shape=jax.ShapeDtypeStruct((rows, heads * hd), BF16),
        grid=(batch, per),
        in_specs=[pl.BlockSpec((tc, heads), lambda b, t: (b * per + t, 0))],
        out_specs=pl.BlockSpec((tc, heads * hd), lambda b, t: (b * per + t, 0)),
        scratch_shapes=[pltpu.VMEM((1, heads), F32)],
        compiler_params=_params(("arbitrary", "arbitrary"), 24 << 20),
        name="fox_logf_cumsum",
    )(logf)


def _fox_attn_kernel(q_ref, k_ref, v_ref, aug_ref, *rest, tq, tk, ts, hd, heads_per_step, n_round):
    o_ref, vt_ref = rest[n_round], rest[-1]
    _WeightRounding.run(rest[:n_round], rest[n_round + 1:-1])
    i = pl.program_id(2)
    seq = k_ref.shape[0]
    q0 = pl.multiple_of(i * tq, tq)
    lane_tile = 128

    @pl.when(i == 0)
    def _():
        for t in range(seq // tk):
            vt_ref[:, t * tk:(t + 1) * tk] = v_ref[t * tk:(t + 1) * tk, :].astype(F32).T.astype(BF16)

    ones = jnp.where(lax.broadcasted_iota(jnp.int32, (tq, hd), 1) < 3, 1.0, 0.0).astype(BF16)
    q_ext = [jnp.concatenate([q_ref[:, hh * hd:(hh + 1) * hd], ones], axis=1)
             for hh in range(heads_per_step)]

    def step(j, carry, masked):
        carry = list(carry)
        items = [(sub, hh) for sub in range(tk // ts) for hh in range(heads_per_step)]
        n_tiles = tq // lane_tile

        def tile_range(sub):
            if masked and tq == tk:
                return sub * ts // lane_tile, min((sub + 1) * ts // lane_tile, n_tiles)
            return 0, (n_tiles if masked else 0)

        def scores(sub, hh):
            k0 = pl.multiple_of(j * tk + sub * ts, ts)
            lanes = slice(hh * hd, (hh + 1) * hd)
            k_ext = jnp.concatenate([k_ref[pl.ds(k0, ts), lanes], aug_ref[pl.ds(k0, ts), lanes]], axis=1)
            return lax.dot_general(k_ext, q_ext[hh][tile_range(sub)[0] * lane_tile:],
                                   (((1,), (1,)), ((), ())), preferred_element_type=F32)

        s_next = scores(*items[0])
        for n, (sub, hh) in enumerate(items):
            s = s_next
            if n + 1 < len(items):
                s_next = scores(*items[n + 1])
            m, l, acc = carry[3 * hh:3 * hh + 3]
            k0 = pl.multiple_of(j * tk + sub * ts, ts)
            first, full = tile_range(sub)
            lo = first * lane_tile
            m_t, l_t, p_t = ([m[:, :lo]], [l[:, :lo]], []) if lo else ([], [], [])
            for qt in range(first, n_tiles):
                ql = slice(qt * lane_tile, (qt + 1) * lane_tile)
                sq = s[:, (qt - first) * lane_tile:(qt - first + 1) * lane_tile]
                if qt < full:
                    keep = (k0 + lax.broadcasted_iota(jnp.int32, (ts, lane_tile), 0)
                            <= q0 + qt * lane_tile + lax.broadcasted_iota(jnp.int32, (ts, lane_tile), 1))
                    sq = jnp.where(keep, sq, NEG_BIG)
                mq = jnp.maximum(m[:, ql], jnp.max(sq, axis=0, keepdims=True))
                pq = jnp.exp2(sq - mq)
                l_t.append(jnp.exp2(m[:, ql] - mq) * l[:, ql] + jnp.sum(pq, axis=0, keepdims=True))
                m_t.append(mq)
                p_t.append(pq.astype(BF16))
            m_new = jnp.concatenate(m_t, axis=1)
            alpha = jnp.exp2(m[:, lo:] - m_new[:, lo:])
            vt = vt_ref[hh * hd:(hh + 1) * hd, pl.ds(k0, ts)]
            acc_hi = alpha * acc[:, lo:] + jnp.dot(vt, jnp.concatenate(p_t, axis=1),
                                                   preferred_element_type=F32)
            carry[3 * hh:3 * hh + 3] = [m_new, jnp.concatenate(l_t, axis=1),
                                        jnp.concatenate([acc[:, :lo], acc_hi], axis=1) if lo else acc_hi]
        return tuple(carry)

    init = (jnp.full((1, tq), NEG_BIG, F32), jnp.zeros((1, tq), F32),
            jnp.zeros((hd, tq), F32)) * heads_per_step
    n_full = q0 // tk
    carry = lax.fori_loop(0, n_full, lambda j, c: step(j, c, False), init)
    carry = step(n_full, carry, True)
    for hh in range(heads_per_step):
        _, l, acc = carry[3 * hh:3 * hh + 3]
        o_ref[:, hh * hd:(hh + 1) * hd] = (acc / l).T.astype(o_ref.dtype)


def _fox_attn(q, k, v, aug, batch, seq, heads, head_dim, weights_to_round):
    tq = min(ATTN_Q_TILE, seq)
    tk = min(ATTN_K_TILE, seq)
    nq = seq // tq
    hps = ATTN_HEADS_PER_STEP if heads % ATTN_HEADS_PER_STEP == 0 else 1
    groups = heads // hps
    width = hps * head_dim
    rounding = _WeightRounding(weights_to_round, batch * groups * nq,
                               lambda b, h, i: (b * groups + h) * nq + i)
    kernel = functools.partial(_fox_attn_kernel, tq=tq, tk=tk, ts=min(ATTN_K_SUB, tk), hd=head_dim,
                               heads_per_step=hps, n_round=len(rounding))
    vmem = (7 * _nbytes((seq, width), BF16) + 6 * hps * _nbytes((tk, tq), F32) + rounding.vmem_bytes()
            + (8 << 20))
    outs = pl.pallas_call(
        kernel,
        out_shape=[jax.ShapeDtypeStruct(q.shape, BF16)] + rounding.out_shapes(),
        grid=(batch, groups, nq),
        in_specs=[pl.BlockSpec((tq, width), lambda b, h, i: (b * nq + i, h)),
                  pl.BlockSpec((seq, width), lambda b, h, i: (b, h)),
                  pl.BlockSpec((seq, width), lambda b, h, i: (b, h)),
                  pl.BlockSpec((seq, width), lambda b, h, i: (b, h))] + rounding.in_specs(),
        out_specs=[pl.BlockSpec((tq, width), lambda b, h, i: (b * nq + i, h))] + rounding.out_specs(),
        scratch_shapes=[pltpu.VMEM((width, seq), BF16)],
        compiler_params=_params(("arbitrary", "arbitrary", "arbitrary"), vmem),
        name="fox_prompt_attention",
    )(q, k, v, aug, *rounding.operands())
    return outs[0], outs[1:]


def _page_suffix_kernel(lf_ref, suf_ref, tot_ref, *, heads):
    x = lf_ref[...]
    cols = x.shape[1]
    col = lax.broadcasted_iota(jnp.int32, x.shape, 1)
    incl = x
    tot = x
    sh = heads
    while sh < cols:
        incl = incl + jnp.where(col + sh < cols, pltpu.roll(incl, cols - sh, 1), 0.0)
        tot = tot + pltpu.roll(tot, sh, 1)
        sh *= 2
    suf_ref[...] = incl - x
    tot_ref[...] = tot


def _page_suffix(cache_lf, heads):
    n_rows, cols = cache_lf.shape
    tr = 256 if n_rows % 256 == 0 else n_rows
    shape = jax.ShapeDtypeStruct((n_rows, cols), F32)
    spec = pl.BlockSpec((tr, cols), lambda r: (r, 0))
    return pl.pallas_call(
        functools.partial(_page_suffix_kernel, heads=heads),
        out_shape=(shape, shape),
        grid=(n_rows // tr,),
        in_specs=[spec],
        out_specs=(spec, spec),
        compiler_params=_params(("arbitrary",), 12 * _nbytes((tr, cols), F32) + (4 << 20)),
        name="fox_page_logf_suffix",
    )(cache_lf)


def _fox_decode_kernel(pt_ref, q_ref, kn_ref, vn_ref, cn_ref, *refs, pages, page_id):
    k_refs = refs[:pages]
    v_refs = refs[pages:2 * pages]
    suf_refs = refs[2 * pages:3 * pages]
    tot_refs = refs[3 * pages:4 * pages]
    o_ref, m_ref, l_ref, acc_ref, run_ref = refs[4 * pages:]
    s_idx = pl.program_id(1)
    heads = q_ref.shape[0]
    cols = k_refs[0].shape[0]
    head_mask = (lax.broadcasted_iota(jnp.int32, (heads, cols), 1) % heads
                 == lax.broadcasted_iota(jnp.int32, (heads, cols), 0))

    @pl.when(s_idx == 0)
    def _():
        m_ref[...] = jnp.full_like(m_ref, NEG_BIG)
        l_ref[...] = jnp.zeros_like(l_ref)
        acc_ref[...] = jnp.zeros_like(acc_ref)
        run_ref[...] = jnp.zeros_like(run_ref)

    q = q_ref[...]
    cn = cn_ref[...]
    run = run_ref[...]
    scores = []
    for g in range(pages):
        s = lax.dot_general(q, k_refs[g][...].astype(BF16), (((1,), (1,)), ((), ())),
                            preferred_element_type=F32)
        row = page_id(pt_ref, pl.program_id(0), s_idx, g) % F32_SUBLANES
        scores.append(jnp.where(head_mask, s + (suf_refs[g][pl.ds(row, 1), :] + (run + cn)), NEG_BIG))
        run = run + tot_refs[g][pl.ds(row, 1), :]
    run_ref[...] = run
    m = m_ref[...]
    m_new = m
    for s in scores:
        m_new = jnp.maximum(m_new, jnp.max(s, axis=-1, keepdims=True))
    alpha = jnp.exp(m - m_new)
    l = alpha * l_ref[...]
    acc = alpha * acc_ref[...]
    for g in range(pages):
        p = jnp.exp(scores[g] - m_new)
        l = l + jnp.sum(p, axis=-1, keepdims=True)
        acc = acc + jnp.dot(p.astype(BF16), v_refs[g][...].astype(BF16), preferred_element_type=F32)
    m_ref[...] = m_new
    l_ref[...] = l
    acc_ref[...] = acc

    @pl.when(s_idx == pl.num_programs(1) - 1)
    def _():
        s_new = jnp.sum(q.astype(F32) * kn_ref[...], axis=-1, keepdims=True)
        m = m_ref[...]
        m_new = jnp.maximum(m, s_new)
        alpha = jnp.exp(m - m_new)
        p_new = jnp.exp(s_new - m_new)
        l = alpha * l_ref[...] + p_new
        acc = alpha * acc_ref[...] + p_new.astype(BF16).astype(F32) * vn_ref[...]
        o_ref[...] = acc / l


def _fox_decode(page_table, pool_base, q, k_new, v_new, logf_new, cache_k, cache_v, cache_lf):
    bs, n_pages = page_table.shape
    _, heads, hd = q.shape
    n_pool, cols, _ = cache_k.shape
    assert n_pool % F32_SUBLANES == 0 and pool_base % F32_SUBLANES == 0
    suffix, total = _page_suffix(cache_lf, heads)
    g_pages = PAGES_PER_STEP if n_pages % PAGES_PER_STEP == 0 else 1
    steps = n_pages // g_pages

    def page_id(pt, b, s, g):
        return pool_base + pt[b * n_pages + (n_pages - 1 - (s * g_pages + g))]

    kernel = functools.partial(_fox_decode_kernel, pages=g_pages, page_id=page_id)

    def page_idx(g):
        return lambda b, s, pt: (page_id(pt, b, s, g), 0, 0)

    def row_block_idx(g):
        return lambda b, s, pt: (page_id(pt, b, s, g) // F32_SUBLANES, 0)

    tok = pl.BlockSpec((None, heads, hd), lambda b, s, pt: (b, 0, 0))
    in_specs = [tok, tok, tok, pl.BlockSpec((None, 1, cols), lambda b, s, pt: (b, 0, 0))]
    in_specs += [pl.BlockSpec((None, cols, hd), page_idx(g)) for g in range(g_pages)]
    in_specs += [pl.BlockSpec((None, cols, hd), page_idx(g)) for g in range(g_pages)]
    in_specs += [pl.BlockSpec((F32_SUBLANES, cols), row_block_idx(g)) for g in range(g_pages)]
    in_specs += [pl.BlockSpec((F32_SUBLANES, cols), row_block_idx(g)) for g in range(g_pages)]
    vmem = (4 * g_pages * _nbytes((cols, hd), F32) + 2 * g_pages * _nbytes((cols, hd), BF16)
            + 4 * g_pages * _nbytes((heads, cols), F32) + (4 << 20))
    grid_spec = pltpu.PrefetchScalarGridSpec(
        num_scalar_prefetch=1,
        grid=(bs, steps),
        in_specs=in_specs,
        out_specs=pl.BlockSpec((None, heads, hd), lambda b, s, pt: (b, 0, 0)),
        scratch_shapes=[pltpu.VMEM((heads, 1), F32), pltpu.VMEM((heads, 1), F32),
                        pltpu.VMEM((heads, hd), F32), pltpu.VMEM((1, cols), F32)])
    return pl.pallas_call(
        kernel,
        out_shape=jax.ShapeDtypeStruct((bs, heads, hd), F32),
        grid_spec=grid_spec,
        compiler_params=_params(("arbitrary", "arbitrary"), vmem),
        name="fox_decode_attention",
    )(page_table.reshape(-1), q, k_new, v_new, logf_new,
      *([cache_k] * g_pages), *([cache_v] * g_pages), *([suffix] * g_pages), *([total] * g_pages))


def _proj_residual_kernel(a_ref, w_ref, x_ref, gate_ref, o_ref):
    z = jnp.dot(a_ref[...], w_ref[...].astype(BF16), preferred_element_type=F32)
    o_ref[...] = x_ref[...] + gate_ref[...] * z


def _proj_residual(a, w, layer, x, mods, mod_rows, tiles_per_group, gate_chunk, tm):
    m_rows, k = a.shape
    d = w.shape[2]
    tn = min(COL_TILE, d)
    vmem = (2 * _nbytes((tm, k), BF16) + 2 * _nbytes((k, tn), F32) + _nbytes((k, tn), BF16)
            + 5 * _nbytes((tm, tn), F32) + (4 << 20))
    return pl.pallas_call(
        _proj_residual_kernel,
        out_shape=jax.ShapeDtypeStruct((m_rows, d), F32),
        grid=(m_rows // tm, d // tn),
        in_specs=[pl.BlockSpec((tm, k), lambda m, n: (m, 0)),
                  pl.BlockSpec((None, k, tn), lambda m, n: (layer, 0, n)),
                  pl.BlockSpec((tm, tn), lambda m, n: (m, n)),
                  pl.BlockSpec((None, mod_rows, tn),
                               lambda m, n: (m // tiles_per_group, 0, gate_chunk * (d // tn) + n))],
        out_specs=pl.BlockSpec((tm, tn), lambda m, n: (m, n)),
        compiler_params=_params(("arbitrary", "arbitrary"), vmem),
        name="mixer_out_proj",
    )(a, w, x, mods)


def _mlp_kernel(x_ref, xs_ref, g_ref, mod_ref, mods_ref, wu_ref, wd_ref, fg_ref, o_ref, os_ref, h_ref, *,
                tm, final_norm):
    f = pl.program_id(1)
    d = x_ref.shape[1]

    @pl.when(f == 0)
    def _():
        h_ref[:tm, :] = _norm_mod(x_ref[...], g_ref[...], mod_ref[:, :d], mod_ref[:, d:2 * d]).astype(BF16)
        h_ref[tm:, :] = _norm_mod(xs_ref[...], g_ref[...], mods_ref[:, :d],
                                  mods_ref[:, d:2 * d]).astype(BF16)
        o_ref[...] = jnp.zeros_like(o_ref)
        os_ref[...] = jnp.zeros_like(os_ref)

    u = jnp.dot(h_ref[...], wu_ref[...].astype(BF16), preferred_element_type=F32)
    u = jnp.square(jnp.maximum(u, 0.0)).astype(BF16)
    z = jnp.dot(u, wd_ref[...].astype(BF16), preferred_element_type=F32)
    o_ref[...] += z[:tm]
    os_ref[...] += z[tm:]

    @pl.when(f == pl.num_programs(1) - 1)
    def _():
        for x_r, mod_r, o_r in ((x_ref, mod_ref, o_ref), (xs_ref, mods_ref, os_ref)):
            y = x_r[...] + mod_r[:, 2 * d:] * o_r[...]
            if final_norm:
                y = y * lax.rsqrt(jnp.mean(y * y, axis=-1, keepdims=True) + EPS) * fg_ref[...]
            o_r[...] = y


def _mlp(x, xs, g, mods_p, mods_s, tiles_per_batch, w_up, w_down, layer, final_g, final_norm, tm):
    m_rows, d = x.shape
    srows = xs.shape[0]
    d_ff = w_up.shape[2]
    tf = min(COL_TILE, d_ff)
    kernel = functools.partial(_mlp_kernel, tm=tm, final_norm=final_norm)
    x_buffers = 1 if w_up.dtype == F32 else 2
    x_mode = {"pipeline_mode": pl.Buffered(1)} if x_buffers == 1 else {}
    vmem = ((2 + x_buffers) * _nbytes((tm, d), F32) + _nbytes((tm + srows, d), BF16)
            + 4 * _nbytes((d, tf), w_up.dtype) + 2 * _nbytes((d, tf), BF16) + 4 * _nbytes((tm, tf), F32)
            + (4 << 20))
    full = lambda m, f: (0, 0)
    vec = pl.BlockSpec((1, d), full)
    return pl.pallas_call(
        kernel,
        out_shape=(jax.ShapeDtypeStruct((m_rows, d), F32), jax.ShapeDtypeStruct((srows, d), F32)),
        grid=(m_rows // tm, d_ff // tf),
        in_specs=[pl.BlockSpec((tm, d), lambda m, f: (m, 0), **x_mode),
                  pl.BlockSpec((srows, d), full),
                  vec,
                  _mod_spec(1, d, 3, tiles_per_batch, 3),
                  _mod_spec(srows, d, 3, m_rows, 3),
                  pl.BlockSpec((None, d, tf), lambda m, f: (layer, 0, f)),
                  pl.BlockSpec((None, tf, d), lambda m, f: (layer, f, 0)),
                  vec],
        out_specs=(pl.BlockSpec((tm, d), lambda m, f: (m, 0)), pl.BlockSpec((srows, d), full)),
        scratch_shapes=[pltpu.VMEM((tm + srows, d), BF16)],
        compiler_params=_params(("arbitrary", "arbitrary"), vmem),
        name="relu2_mlp",
    )(x, xs, g, mods_p, mods_s, w_up, w_down, final_g)


def _ret_proj_kernel(x_ref, xs_ref, g_ref, mod_ref, mods_ref, w_ref, cs_ref, css_ref,
                     q_ref, k_ref, v_ref, gt_ref, zs_ref, h_ref, *, nq, tm, key_dim, k_scale):
    n = pl.program_id(1)
    d = x_ref.shape[1]

    @pl.when(n == 0)
    def _():
        h_ref[:tm, :] = _norm_mod(x_ref[...], g_ref[...], mod_ref[:, :d], mod_ref[:, d:]).astype(BF16)
        h_ref[tm:, :] = _norm_mod(xs_ref[...], g_ref[...], mods_ref[:, :d], mods_ref[:, d:]).astype(BF16)

    z = jnp.dot(h_ref[...], w_ref[...].astype(BF16), preferred_element_type=F32)
    zp = z[:tm]
    zs = z[tm:]
    half = key_dim // 2

    def rotate(zz, table_ref):
        cos = table_ref[:, :half]
        sin = table_ref[:, half:]
        parts = []
        for hh in range(zz.shape[1] // key_dim):
            x1 = zz[:, hh * key_dim:hh * key_dim + half]
            x2 = zz[:, hh * key_dim + half:(hh + 1) * key_dim]
            parts += [x1 * cos - x2 * sin, x1 * sin + x2 * cos]
        return jnp.concatenate(parts, axis=1)

    last_tile = pl.program_id(0) == pl.num_programs(0) - 1
    in_q = n < nq
    in_k = jnp.logical_and(n >= nq, n < 2 * nq)

    @pl.when(in_q)
    def _():
        q_ref[...] = rotate(zp, cs_ref).astype(BF16)

    @pl.when(in_k)
    def _():
        k_ref[...] = rotate(zp, cs_ref) * k_scale

    @pl.when(jnp.logical_and(n >= 2 * nq, n < 4 * nq))
    def _():
        v_ref[...] = zp.astype(BF16)

    @pl.when(n >= 4 * nq)
    def _():
        gt_ref[...] = zp

    @pl.when(jnp.logical_and(last_tile, n < 2 * nq))
    def _():
        zs_ref[...] = rotate(zs, css_ref) * jnp.where(in_k, k_scale, 1.0)

    @pl.when(jnp.logical_and(last_tile, n >= 2 * nq))
    def _():
        zs_ref[...] = zs


def _ret_proj(x, xs, g, mods_p, mods_s, tiles_per_batch, w_in, layer, rot, rot_s, key_dim, tm):
    m_rows, d = x.shape
    srows = xs.shape[0]
    tn = min(COL_TILE, d)
    nq = d // tn
    kernel = functools.partial(_ret_proj_kernel, nq=nq, tm=tm, key_dim=key_dim, k_scale=key_dim ** -0.5)
    last_tile = m_rows // tm - 1

    def col(lo, width):
        return lambda m, n: (m, jnp.clip(n - lo, 0, width - 1))

    vmem = (2 * _nbytes((tm, d), F32) + _nbytes((tm + srows, d), BF16) + 2 * _nbytes((d, tn), F32)
            + _nbytes((d, tn), BF16) + 2 * (2 * _nbytes((tm, tn), BF16) + 2 * _nbytes((tm, tn), F32))
            + 2 * _nbytes((tm, key_dim), F32) + 4 * _nbytes((tm, tn), F32) + (4 << 20))
    full = lambda m, n: (0, 0)
    outs = pl.pallas_call(
        kernel,
        out_shape=(jax.ShapeDtypeStruct((m_rows, d), BF16), jax.ShapeDtypeStruct((m_rows, d), F32),
                   jax.ShapeDtypeStruct((m_rows, 2 * d), BF16), jax.ShapeDtypeStruct((m_rows, 2 * d), F32),
                   jax.ShapeDtypeStruct((srows, 6 * d), F32)),
        grid=(m_rows // tm, 6 * nq),
        in_specs=[pl.BlockSpec((tm, d), lambda m, n: (m, 0)),
                  pl.BlockSpec((srows, d), full),
                  pl.BlockSpec((1, d), full),
                  _mod_spec(1, d, 0, tiles_per_batch, 2),
                  _mod_spec(srows, d, 0, m_rows, 2),
                  pl.BlockSpec((None, d, tn), lambda m, n: (layer, 0, n)),
                  pl.BlockSpec((tm, key_dim), lambda m, n: (m % tiles_per_batch, 0)),
                  pl.BlockSpec((srows, key_dim), full)],
        out_specs=(pl.BlockSpec((tm, tn), col(0, nq)),
                   pl.BlockSpec((tm, tn), col(nq, nq)),
                   pl.BlockSpec((tm, tn), col(2 * nq, 2 * nq)),
                   pl.BlockSpec((tm, tn), col(4 * nq, 2 * nq)),
                   pl.BlockSpec((srows, tn), lambda m, n: (0, jnp.where(m == last_tile, n, 0)))),
        scratch_shapes=[pltpu.VMEM((tm + srows, d), BF16)],
        compiler_params=_params(("arbitrary", "arbitrary"), vmem),
        name="ret_in_proj",
    )(x, xs, g, mods_p, mods_s, w_in, rot, rot_s)
    return outs[:4], outs[4]


def _group_norm_gate(o, gate):
    mu = jnp.mean(o, axis=-1, keepdims=True)
    cen = o - mu
    var = jnp.mean(cen * cen, axis=-1, keepdims=True)
    return _silu(gate) * (cen * lax.rsqrt(var + EPS))


def _ret_chunk_kernel(lg_ref, q_ref, k_ref, v_ref, gt_ref, *rest, n_round):
    y_ref, s_ref = rest[n_round:n_round + 2]
    dec_ref, xi_ref, zeta_ref = rest[-3:]
    _WeightRounding.run(rest[:n_round], rest[n_round + 2:-3])
    lg = jnp.full((1, 1), lg_ref[pl.program_id(1)], F32)
    ln = q_ref.shape[0]

    @pl.when(pl.program_id(2) == 0)
    def _():
        s_ref[...] = jnp.zeros_like(s_ref)
        diff = (lax.broadcasted_iota(jnp.int32, (ln, ln), 0)
                - lax.broadcasted_iota(jnp.int32, (ln, ln), 1)).astype(F32)
        dec_ref[...] = jnp.where(diff >= 0, jnp.exp(lg * jnp.maximum(diff, 0.0)), 0.0)
        n = lax.broadcasted_iota(jnp.int32, (ln, 1), 0).astype(F32)
        xi_ref[...] = jnp.exp(lg * (n + 1.0))
        zeta_ref[...] = jnp.exp(lg * (ln - 1.0 - n))

    q = q_ref[...]
    k = k_ref[...]
    v = v_ref[...]
    s0 = s_ref[...]
    qk = lax.dot_general(q, k.astype(BF16), (((1,), (1,)), ((), ())), preferred_element_type=F32)
    a = (qk * dec_ref[...]).astype(BF16)
    o = (jnp.dot(a, v, preferred_element_type=F32)
         + jnp.dot(q, s0.astype(BF16), preferred_element_type=F32) * xi_ref[...])
    kz = (k * zeta_ref[...]).astype(BF16)
    s_ref[...] = jnp.exp(lg * ln) * s0 + lax.dot_general(
        kz, v, (((0,), (0,)), ((), ())), preferred_element_type=F32)
    y_ref[...] = _group_norm_gate(o, gt_ref[...]).astype(y_ref.dtype)


def _ret_prompt(lg, q, k, v, gate, batch, seq, heads, key_dim, val_dim, weights_to_round):
    ln = min(RET_CHUNK, seq)
    nc = seq // ln
    rounding = _WeightRounding(weights_to_round, batch * heads * nc,
                               lambda b, h, c, *_: (b * heads + h) * nc + c)
    vmem = (2 * (_nbytes((ln, key_dim), BF16) + _nbytes((ln, key_dim), F32) + _nbytes((ln, val_dim), BF16)
                 + _nbytes((ln, val_dim), F32) + _nbytes((ln, val_dim), BF16))
            + 4 * _nbytes((key_dim, val_dim), F32) + 8 * _nbytes((ln, val_dim), F32)
            + rounding.vmem_bytes() + (8 << 20))
    grid_spec = pltpu.PrefetchScalarGridSpec(
        num_scalar_prefetch=1,
        grid=(batch, heads, nc),
        in_specs=[pl.BlockSpec((ln, key_dim), lambda b, h, c, lg: (b * nc + c, h)),
                  pl.BlockSpec((ln, key_dim), lambda b, h, c, lg: (b * nc + c, h)),
                  pl.BlockSpec((ln, val_dim), lambda b, h, c, lg: (b * nc + c, h)),
                  pl.BlockSpec((ln, val_dim), lambda b, h, c, lg: (b * nc + c, h))] + rounding.in_specs(),
        out_specs=[pl.BlockSpec((ln, val_dim), lambda b, h, c, lg: (b * nc + c, h)),
                   pl.BlockSpec((None, None, key_dim, val_dim), lambda b, h, c, lg: (b, h, 0, 0))]
        + rounding.out_specs(),
        scratch_shapes=[pltpu.VMEM((ln, ln), F32), pltpu.VMEM((ln, 1), F32), pltpu.VMEM((ln, 1), F32)])
    outs = pl.pallas_call(
        functools.partial(_ret_chunk_kernel, n_round=len(rounding)),
        out_shape=[jax.ShapeDtypeStruct((batch * seq, heads * val_dim), BF16),
                   jax.ShapeDtypeStruct((batch, heads, key_dim, val_dim), F32)] + rounding.out_shapes(),
        grid_spec=grid_spec,
        compiler_params=_params(("arbitrary", "arbitrary", "arbitrary"), vmem),
        name="ret_prompt_chunks",
    )(lg, q, k, v, gate, *rounding.operands())
    return outs[0], outs[1], outs[2:]


def _ret_decode_kernel(lg_ref, q_ref, k_ref, v_ref, gt_ref, s0_ref, y_ref, s_ref):
    q = q_ref[...]
    k = k_ref[...]
    q_cols = q.T
    k_cols = k.T
    a_all = jnp.sum(q * k, axis=-1, keepdims=True)
    for h in range(s0_ref.shape[0]):
        gamma = jnp.exp(jnp.full((1, 1), lg_ref[h], F32))
        s0 = s0_ref[h]
        v = v_ref[h:h + 1, :]
        qs = jnp.sum(q_cols[:, h:h + 1] * s0, axis=0, keepdims=True)
        o = a_all[h:h + 1, :] * v + qs * gamma
        s_ref[h] = gamma * s0 + k_cols[:, h:h + 1] * v
        y_ref[h:h + 1, :] = _group_norm_gate(o, gt_ref[h:h + 1, :])


def _ret_decode(lg, q, k, v, gate, state):
    bs, heads, key_dim, val_dim = state.shape

    def vec(width):
        return pl.BlockSpec((None, heads, width), lambda b, lg: (b, 0, 0))

    state_spec = pl.BlockSpec((None, heads, key_dim, val_dim), lambda b, lg: (b, 0, 0, 0))
    grid_spec = pltpu.PrefetchScalarGridSpec(
        num_scalar_prefetch=1,
        grid=(bs,),
        in_specs=[vec(key_dim), vec(key_dim), vec(val_dim), vec(val_dim), state_spec],
        out_specs=(vec(val_dim), state_spec))
    y, s = pl.pallas_call(
        _ret_decode_kernel,
        out_shape=(jax.ShapeDtypeStruct((bs, heads, val_dim), F32),
                   jax.ShapeDtypeStruct(state.shape, F32)),
        grid_spec=grid_spec,
        compiler_params=_params(("arbitrary",), 4 * _nbytes(state.shape[1:], F32) + (16 << 20)),
        name="ret_decode_step",
    )(lg, q.reshape(bs, heads, key_dim), k.reshape(bs, heads, key_dim),
      v.reshape(bs, heads, val_dim), gate.reshape(bs, heads, val_dim), state)
    return y.reshape(bs, heads * val_dim), s


def _rotary_table(pos, key_dim):
    half = key_dim // 2
    inv = (np.float32(1.0) / np.float32(ROPE_BASE) ** np.linspace(0.0, 1.0, half, dtype=np.float32))
    ang = (np.asarray(pos, np.float32)[:, None] * inv.astype(np.float32)[None, :]).astype(np.float64)
    return jnp.asarray(np.concatenate([np.cos(ang), np.sin(ang)], axis=1), F32)


def _pad_rows(a, rows):
    return jnp.pad(a, ((0, rows - a.shape[0]), (0, 0)))


def kernel(x_prompt, x_sample, c_prompt, c_sample, cache_fox_k, cache_fox_v, cache_fox_logf, state_ret, page_table, norm1_g, norm2_g, ada_w, ada_b, fox_w_in, fox_b_f, fox_w_out, ret_w_in, ret_w_out, mlp_w_up, mlp_w_down, final_g):
    bp, tp, d = x_prompt.shape
    bs, ts, _ = x_sample.shape
    assert ts == 1, "the decode kernels handle one new token per sequence"
    depth = ada_w.shape[0]
    _, n_pool, page, fox_heads, fox_hd = cache_fox_k.shape
    _, _, ret_heads, key_dim, val_dim = state_ret.shape
    past_len = page_table.shape[1] * page
    srows = -(-bs // BF16_SUBLANES) * BF16_SUBLANES
    tm = min(ROW_TILE, tp)
    tiles_p = tp // tm

    mod_rows = -(-(bs + bp) // 8) * 8
    c_all = _pad_rows(jnp.concatenate([c_sample, c_prompt], axis=0), max(mod_rows, srows))
    mods = _ada(c_all, ada_w, ada_b)

    lg = jnp.log1p(-jnp.exp2(-5.0 - jnp.arange(ret_heads, dtype=F32)))
    rot_p = _rotary_table(np.arange(tp), key_dim)
    rot_s = _rotary_table(np.full((srows,), past_len), key_dim)

    yp = x_prompt.reshape(bp * tp, d)
    ys = _pad_rows(x_sample.reshape(bs, d), srows)
    fg = final_g.reshape(1, d)
    outs = {name: [] for name in ("kp", "vp", "lfp", "ks", "vs", "lfs", "sp", "ss")}

    ret_rounded = None
    for i in range(depth):
        mods_p = mods[i, bs:bs + bp].reshape(bp, 1, 6 * d)
        mods_s = mods[i, :srows].reshape(1, srows, 6 * d)
        g1 = norm1_g[i].reshape(1, d)
        g2 = norm2_g[i].reshape(1, d)
        j = i // 2
        if i % 2 == 0:
            b_f = fox_b_f[j].reshape(1, fox_heads)
            (q, kf, vf, kb, vb, lf), sample = _fox_proj(
                yp, ys, g1, mods_p, mods_s, tiles_p, fox_w_in, j, b_f, tm,
                fox_hd ** -0.5 * LOG2E, fox_hd ** -0.5)
            aug = _cumsum(lf, bp, tp, fox_hd)
            to_round = [(fox_w_out, j), (mlp_w_up, i), (mlp_w_down, i)]
            if i + 1 < depth:
                jr = (i + 1) // 2
                to_round += [(ret_w_in, jr), (ret_w_out, jr), (mlp_w_up, i + 1), (mlp_w_down, i + 1)]
            o, rounded = _fox_attn(q, kb, vb, aug, bp, tp, fox_heads, fox_hd, to_round)
            w_out, w_up, w_down = rounded[:3]
            l_mlp = 0
            ret_rounded = rounded[3:] if i + 1 < depth else None
            yp = _proj_residual(o, w_out, 0, yp, mods_p, 1, tiles_p, 2, tm)
            outs["kp"].append(kf.reshape(bp, tp, fox_heads, fox_hd))
            outs["vp"].append(vf.reshape(bp, tp, fox_heads, fox_hd))
            outs["lfp"].append(lf.reshape(bp, tp, fox_heads))
            zs, lf = sample
            kf, vf = zs[:, d:2 * d], zs[:, 2 * d:]
            bf16_rounded = lambda a: a.astype(BF16).astype(F32)
            o = _fox_decode(
                page_table, j * n_pool,
                zs[:bs, :d].astype(BF16).reshape(bs, fox_heads, fox_hd),
                bf16_rounded(kf[:bs]).reshape(bs, fox_heads, fox_hd),
                bf16_rounded(vf[:bs]).reshape(bs, fox_heads, fox_hd),
                jnp.tile(lf[:bs], (1, page)).reshape(bs, 1, page * fox_heads),
                cache_fox_k.reshape(-1, page * fox_heads, fox_hd),
                cache_fox_v.reshape(-1, page * fox_heads, fox_hd),
                cache_fox_logf.reshape(-1, page * fox_heads))
            o = _pad_rows(o.reshape(bs, d), srows).astype(BF16)
            ys = _proj_residual(o, w_out, 0, ys, mods_s, srows, 1, 2, srows)
            outs["ks"].append(kf[:bs].reshape(bs, ts, fox_heads, fox_hd))
            outs["vs"].append(vf[:bs].reshape(bs, ts, fox_heads, fox_hd))
            outs["lfs"].append(lf[:bs].reshape(bs, ts, fox_heads))
        else:
            if ret_rounded is None:
                w_in, w_out, w_up, w_down = ret_w_in, ret_w_out, mlp_w_up, mlp_w_down
                l_mix, l_mlp = j, i
            else:
                w_in, w_out, w_up, w_down = ret_rounded
                l_mix = l_mlp = 0
            (q, k, v, gate), zs = _ret_proj(yp, ys, g1, mods_p, mods_s, tiles_p, w_in, l_mix,
                                            rot_p, rot_s, key_dim, tm)
            y, s, _ = _ret_prompt(lg, q, k, v, gate, bp, tp, ret_heads, key_dim, val_dim, [])
            yp = _proj_residual(y, w_out, l_mix, yp, mods_p, 1, tiles_p, 2, tm)
            outs["sp"].append(s)
            zs = zs[:bs]
            bf16_rounded = lambda a: a.astype(BF16).astype(F32)
            y, s = _ret_decode(lg, bf16_rounded(zs[:, :d]), zs[:, d:2 * d], bf16_rounded(zs[:, 2 * d:4 * d]),
                               zs[:, 4 * d:], state_ret[j].astype(F32))
            ys = _proj_residual(_pad_rows(y, srows).astype(BF16), w_out, l_mix, ys, mods_s, srows, 1, 2, srows)
            outs["ss"].append(s)
        yp, ys = _mlp(yp, ys, g2, mods_p, mods_s, tiles_p, w_up, w_down, l_mlp, fg, i == depth - 1, tm)

    return (yp.reshape(bp, tp, d), ys[:bs].reshape(bs, ts, d),
            jnp.stack(outs["kp"]), jnp.stack(outs["vp"]), jnp.stack(outs["lfp"]),
            jnp.stack(outs["ks"]), jnp.stack(outs["vs"]), jnp.stack(outs["lfs"]),
            jnp.stack(outs["sp"]), jnp.stack(outs["ss"]))
```

```python
import functools

import jax
import jax.numpy as jnp
import numpy as np
from jax import lax
from jax.experimental import pallas as pl
from jax.experimental.pallas import tpu as pltpu

F32 = jnp.float32
BF16 = jnp.bfloat16

EPS = 1e-6
ROPE_BASE = 10000.0
LOG2E = 1.4426950408889634
NEG_BIG = -1e30

V7X_VMEM_LIMIT_BYTES = 60000 * 1024
F32_SUBLANES = 8
BF16_SUBLANES = 16

ROW_TILE = 1024
COL_TILE = 512
ATTN_Q_TILE = 1024
ATTN_K_TILE = 1024
ATTN_K_SUB = 512
ATTN_HEADS_PER_STEP = 2
CUMSUM_TILE = 512
RET_CHUNK = 512
PAGES_PER_STEP = 8


def _params(semantics, vmem_bytes):
    return pltpu.CompilerParams(dimension_semantics=semantics,
                                vmem_limit_bytes=int(min(vmem_bytes, V7X_VMEM_LIMIT_BYTES)))


def _nbytes(shape, dtype):
    n = 1
    for s in shape:
        n *= s
    return n * jnp.dtype(dtype).itemsize


def _norm_mod(x, g, shift, scale):
    y = x * lax.rsqrt(jnp.mean(x * x, axis=-1, keepdims=True) + EPS)
    return y * (g * (1.0 + scale)) + shift


def _silu(x):
    return x * jax.nn.sigmoid(x)


def _log_sigmoid(x):
    return jnp.minimum(x, 0.0) - jnp.log1p(jnp.exp(-jnp.abs(x)))


def _split3(x):
    hi = x.astype(BF16)
    r1 = x - hi.astype(F32)
    mid = r1.astype(BF16)
    lo = (r1 - mid.astype(F32)).astype(BF16)
    return hi, mid, lo


class _WeightRounding:
    def __init__(self, weights, n_steps, step_index):
        self.items = []
        for w, layer in weights:
            slab = w.shape[1] // n_steps
            assert w.shape[1] % n_steps == 0 and slab % BF16_SUBLANES == 0, (w.shape, n_steps)
            self.items.append((w, layer, slab))
        self.step_index = step_index

    def __len__(self):
        return len(self.items)

    def in_specs(self):
        return [pl.BlockSpec((None, slab, w.shape[2]),
                             lambda *ids, layer=layer: (layer, self.step_index(*ids), 0))
                for w, layer, slab in self.items]

    def out_specs(self):
        return [pl.BlockSpec((None, slab, w.shape[2]), lambda *ids: (0, self.step_index(*ids), 0))
                for w, _, slab in self.items]

    def out_shapes(self):
        return [jax.ShapeDtypeStruct((1,) + w.shape[1:], BF16) for w, _, _ in self.items]

    def operands(self):
        return [w for w, _, _ in self.items]

    def vmem_bytes(self):
        return sum(2 * _nbytes((slab, w.shape[2]), F32) + 2 * _nbytes((slab, w.shape[2]), BF16)
                   for w, _, slab in self.items)

    @staticmethod
    def run(in_refs, out_refs):
        for i_ref, o_ref in zip(in_refs, out_refs):
            o_ref[...] = i_ref[...].astype(BF16)


def _mod_spec(rows, width, chunk, tiles_per_group, n_chunks=1):
    assert chunk % n_chunks == 0
    return pl.BlockSpec((None, rows, n_chunks * width),
                        lambda m, n: (m // tiles_per_group, 0, chunk // n_chunks))


def _ada_kernel(c_ref, w_ref, b_ref, o_ref):
    o_ref[...] = jnp.dot(_silu(c_ref[...]), w_ref[...], preferred_element_type=F32) + b_ref[...]


def _ada(c_all, ada_w, ada_b):
    depth, d, n6 = ada_w.shape
    rows = c_all.shape[0]
    tn = min(1024, n6)
    vmem = 2 * (_nbytes((d, tn), F32) + _nbytes((rows, tn), F32)) + _nbytes((rows, d), F32) * 2 + (4 << 20)
    return pl.pallas_call(
        _ada_kernel,
        out_shape=jax.ShapeDtypeStruct((depth, rows, n6), F32),
        grid=(depth, n6 // tn),
        in_specs=[pl.BlockSpec((rows, d), lambda i, n: (0, 0)),
                  pl.BlockSpec((None, d, tn), lambda i, n: (i, 0, n)),
                  pl.BlockSpec((None, 1, tn), lambda i, n: (i, 0, n))],
        out_specs=pl.BlockSpec((None, rows, tn), lambda i, n: (i, 0, n)),
        compiler_params=_params(("arbitrary", "arbitrary"), vmem),
        name="ada_modulation",
    )(c_all, ada_w, ada_b.reshape(depth, 1, n6))


def _fox_proj_kernel(x_ref, xs_ref, g_ref, mod_ref, mods_ref, w_ref, wf_ref, bf_ref,
                     q_ref, kf_ref, vf_ref, kb_ref, vb_ref, lf_ref, zs_ref, lfs_ref, h_ref, *,
                     nq, tm, q_scale, q_scale_s):
    n = pl.program_id(1)
    d = x_ref.shape[1]

    @pl.when(n == 0)
    def _():
        w_f = wf_ref[...].astype(BF16)
        for x_r, mod_r, lf_r, rows in ((x_ref, mod_ref, lf_ref, slice(0, tm)),
                                       (xs_ref, mods_ref, lfs_ref, slice(tm, None))):
            h = _norm_mod(x_r[...], g_ref[...], mod_r[:, :d], mod_r[:, d:]).astype(BF16)
            h_ref[rows, :] = h
            fl = lax.dot_general(h, w_f, (((1,), (1,)), ((), ())), preferred_element_type=F32)
            lf_r[...] = _log_sigmoid(fl + bf_ref[...])

    z = lax.dot_general(h_ref[...], w_ref[...].astype(BF16), (((1,), (1,)), ((), ())),
                        preferred_element_type=F32)
    zp = z[:tm]
    zs = z[tm:]
    in_q = n < nq

    @pl.when(in_q)
    def _():
        q_ref[...] = (zp * q_scale).astype(BF16)

    @pl.when(jnp.logical_and(n >= nq, n < 2 * nq))
    def _():
        kf_ref[...] = zp
        kb_ref[...] = zp.astype(BF16)

    @pl.when(n >= 2 * nq)
    def _():
        vf_ref[...] = zp
        vb_ref[...] = zp.astype(BF16)

    @pl.when(pl.program_id(0) == pl.num_programs(0) - 1)
    def _():
        zs_ref[...] = zs * jnp.where(in_q, q_scale_s, 1.0)


def _fox_proj(x, xs, g, mods_p, mods_s, tiles_per_batch, w_in, layer, b_f, tm, q_scale, q_scale_s):
    m_rows, d = x.shape
    srows = xs.shape[0]
    heads = b_f.shape[1]
    assert w_in.shape[2] - 3 * d == heads and (3 * d) % heads == 0
    w_t = jnp.swapaxes(w_in, 1, 2)
    tn = min(COL_TILE, d)
    nq = d // tn
    kernel = functools.partial(_fox_proj_kernel, nq=nq, tm=tm, q_scale=q_scale, q_scale_s=q_scale_s)

    last_tile = m_rows // tm - 1

    def col(lo):
        return lambda m, n: (m, jnp.clip(n - lo, 0, nq - 1))

    vmem = (2 * _nbytes((tm, d), F32) + _nbytes((tm + srows, d), BF16) + 2 * _nbytes((d, tn), F32)
            + _nbytes((d, tn), BF16) + 2 * (3 * _nbytes((tm, tn), BF16) + 2 * _nbytes((tm, tn), F32))
            + 3 * _nbytes((tm, tn), F32) + (4 << 20))
    out_f = jax.ShapeDtypeStruct((m_rows, d), F32)
    out_b = jax.ShapeDtypeStruct((m_rows, d), BF16)
    full = lambda m, n: (0, 0)
    outs = pl.pallas_call(
        kernel,
        out_shape=(out_b, out_f, out_f, out_b, out_b, jax.ShapeDtypeStruct((m_rows, heads), F32),
                   jax.ShapeDtypeStruct((srows, 3 * d), F32), jax.ShapeDtypeStruct((srows, heads), F32)),
        grid=(m_rows // tm, 3 * nq),
        in_specs=[pl.BlockSpec((tm, d), lambda m, n: (m, 0)),
                  pl.BlockSpec((srows, d), full),
                  pl.BlockSpec((1, d), full),
                  _mod_spec(1, d, 0, tiles_per_batch, 2),
                  _mod_spec(srows, d, 0, m_rows, 2),
                  pl.BlockSpec((None, tn, d), lambda m, n: (layer, n, 0)),
                  pl.BlockSpec((None, heads, d), lambda m, n: (layer, 3 * d // heads, 0)),
                  pl.BlockSpec((1, heads), full)],
        out_specs=(pl.BlockSpec((tm, tn), col(0)),
                   pl.BlockSpec((tm, tn), col(nq)),
                   pl.BlockSpec((tm, tn), col(2 * nq)),
                   pl.BlockSpec((tm, tn), col(nq)),
                   pl.BlockSpec((tm, tn), col(2 * nq)),
                   pl.BlockSpec((tm, heads), lambda m, n: (m, 0)),
                   pl.BlockSpec((srows, tn), lambda m, n: (0, jnp.where(m == last_tile, n, 0))),
                   pl.BlockSpec((srows, heads), full)),
        scratch_shapes=[pltpu.VMEM((tm + srows, d), BF16)],
        compiler_params=_params(("arbitrary", "arbitrary"), vmem),
        name="fox_in_proj",
    )(x, xs, g, mods_p, mods_s, w_t, w_t, b_f)
    return outs[:6], outs[6:]


def _cumsum_kernel(lf_ref, aug_ref, carry_ref, *, hd):
    @pl.when(pl.program_id(1) == 0)
    def _():
        carry_ref[...] = jnp.zeros_like(carry_ref)

    tc, heads = lf_ref.shape
    row = lax.broadcasted_iota(jnp.int32, (tc, tc), 0)
    col = lax.broadcasted_iota(jnp.int32, (tc, tc), 1)
    tri = jnp.where(col <= row, 1.0, 0.0).astype(BF16)
    c = carry_ref[...]
    for piece in _split3(lf_ref[...]):
        c = c + jnp.dot(tri, piece, preferred_element_type=F32)
    carry_ref[...] = c[tc - 1:tc, :]

    row = lax.broadcasted_iota(jnp.int32, (3 * heads, heads * hd), 0)
    lane = lax.broadcasted_iota(jnp.int32, (3 * heads, heads * hd), 1)
    place = jnp.where(lane == (row % heads) * hd + row // heads, 1.0, 0.0).astype(BF16)
    pieces = jnp.concatenate(_split3(c * (-LOG2E)), axis=1)
    aug_ref[...] = jnp.dot(pieces, place, preferred_element_type=F32).astype(BF16)


def _cumsum(logf, batch, seq, hd):
    rows, heads = logf.shape
    tc = min(CUMSUM_TILE, seq)
    per = seq // tc
    return pl.pallas_call(
        functools.partial(_cumsum_kernel, hd=hd),
        out_shape=jax.ShapeDtypeStruct((rows, heads * hd), BF16),
        grid=(batch, per),
        in_specs=[pl.BlockSpec((tc, heads), lambda b, t: (b * per + t, 0))],
        out_specs=pl.BlockSpec((tc, heads * hd), lambda b, t: (b * per + t, 0)),
        scratch_shapes=[pltpu.VMEM((1, heads), F32)],
        compiler_params=_params(("arbitrary", "arbitrary"), 24 << 20),
        name="fox_logf_cumsum",
    )(logf)


def _fox_attn_kernel(q_ref, k_ref, v_ref, aug_ref, *rest, tq, tk, ts, hd, heads_per_step, n_round):
    o_ref, vt_ref = rest[n_round], rest[-1]
    _WeightRounding.run(rest[:n_round], rest[n_round + 1:-1])
    i = pl.program_id(2)
    seq = k_ref.shape[0]
    q0 = pl.multiple_of(i * tq, tq)
    lane_tile = 128

    @pl.when(i == 0)
    def _():
        for t in range(seq // tk):
            vt_ref[:, t * tk:(t + 1) * tk] = v_ref[t * tk:(t + 1) * tk, :].astype(F32).T.astype(BF16)

    ones = jnp.where(lax.broadcasted_iota(jnp.int32, (tq, hd), 1) < 3, 1.0, 0.0).astype(BF16)
    q_ext = [jnp.concatenate([q_ref[:, hh * hd:(hh + 1) * hd], ones], axis=1)
             for hh in range(heads_per_step)]

    def step(j, carry, masked):
        carry = list(carry)
        items = [(sub, hh) for sub in range(tk // ts) for hh in range(heads_per_step)]
        n_tiles = tq // lane_tile

        def tile_range(sub):
            if masked and tq == tk:
                return sub * ts // lane_tile, min((sub + 1) * ts // lane_tile, n_tiles)
            return 0, (n_tiles if masked else 0)

        def scores(sub, hh):
            k0 = pl.multiple_of(j * tk + sub * ts, ts)
            lanes = slice(hh * hd, (hh + 1) * hd)
            k_ext = jnp.concatenate([k_ref[pl.ds(k0, ts), lanes], aug_ref[pl.ds(k0, ts), lanes]], axis=1)
            return lax.dot_general(k_ext, q_ext[hh][tile_range(sub)[0] * lane_tile:],
                                   (((1,), (1,)), ((), ())), preferred_element_type=F32)

        s_next = scores(*items[0])
        for n, (sub, hh) in enumerate(items):
            s = s_next
            if n + 1 < len(items):
                s_next = scores(*items[n + 1])
            m, l, acc = carry[3 * hh:3 * hh + 3]
            k0 = pl.multiple_of(j * tk + sub * ts, ts)
            first, full = tile_range(sub)
            lo = first * lane_tile
            m_t, l_t, p_t = ([m[:, :lo]], [l[:, :lo]], []) if lo else ([], [], [])
            for qt in range(first, n_tiles):
                ql = slice(qt * lane_tile, (qt + 1) * lane_tile)
                sq = s[:, (qt - first) * lane_tile:(qt - first + 1) * lane_tile]
                if qt < full:
                    keep = (k0 + lax.broadcasted_iota(jnp.int32, (ts, lane_tile), 0)
                            <= q0 + qt * lane_tile + lax.broadcasted_iota(jnp.int32, (ts, lane_tile), 1))
                    sq = jnp.where(keep, sq, NEG_BIG)
                mq = jnp.maximum(m[:, ql], jnp.max(sq, axis=0, keepdims=True))
                pq = jnp.exp2(sq - mq)
                l_t.append(jnp.exp2(m[:, ql] - mq) * l[:, ql] + jnp.sum(pq, axis=0, keepdims=True))
                m_t.append(mq)
                p_t.append(pq.astype(BF16))
            m_new = jnp.concatenate(m_t, axis=1)
            alpha = jnp.exp2(m[:, lo:] - m_new[:, lo:])
            vt = vt_ref[hh * hd:(hh + 1) * hd, pl.ds(k0, ts)]
            acc_hi = alpha * acc[:, lo:] + jnp.dot(vt, jnp.concatenate(p_t, axis=1),
                                                   preferred_element_type=F32)
            carry[3 * hh:3 * hh + 3] = [m_new, jnp.concatenate(l_t, axis=1),
                                        jnp.concatenate([acc[:, :lo], acc_hi], axis=1) if lo else acc_hi]
        return tuple(carry)

    init = (jnp.full((1, tq), NEG_BIG, F32), jnp.zeros((1, tq), F32),
            jnp.zeros((hd, tq), F32)) * heads_per_step
    n_full = q0 // tk
    carry = lax.fori_loop(0, n_full, lambda j, c: step(j, c, False), init)
    carry = step(n_full, carry, True)
    for hh in range(heads_per_step):
        _, l, acc = carry[3 * hh:3 * hh + 3]
        o_ref[:, hh * hd:(hh + 1) * hd] = (acc / l).T.astype(o_ref.dtype)


def _fox_attn(q, k, v, aug, batch, seq, heads, head_dim, weights_to_round):
    tq = min(ATTN_Q_TILE, seq)
    tk = min(ATTN_K_TILE, seq)
    nq = seq // tq
    hps = ATTN_HEADS_PER_STEP if heads % ATTN_HEADS_PER_STEP == 0 else 1
    groups = heads // hps
    width = hps * head_dim
    rounding = _WeightRounding(weights_to_round, batch * groups * nq,
                               lambda b, h, i: (b * groups + h) * nq + i)
    kernel = functools.partial(_fox_attn_kernel, tq=tq, tk=tk, ts=min(ATTN_K_SUB, tk), hd=head_dim,
                               heads_per_step=hps, n_round=len(rounding))
    vmem = (7 * _nbytes((seq, width), BF16) + 6 * hps * _nbytes((tk, tq), F32) + rounding.vmem_bytes()
            + (8 << 20))
    outs = pl.pallas_call(
        kernel,
        out_shape=[jax.ShapeDtypeStruct(q.shape, BF16)] + rounding.out_shapes(),
        grid=(batch, groups, nq),
        in_specs=[pl.BlockSpec((tq, width), lambda b, h, i: (b * nq + i, h)),
                  pl.BlockSpec((seq, width), lambda b, h, i: (b, h)),
                  pl.BlockSpec((seq, width), lambda b, h, i: (b, h)),
                  pl.BlockSpec((seq, width), lambda b, h, i: (b, h))] + rounding.in_specs(),
        out_specs=[pl.BlockSpec((tq, width), lambda b, h, i: (b * nq + i, h))] + rounding.out_specs(),
        scratch_shapes=[pltpu.VMEM((width, seq), BF16)],
        compiler_params=_params(("arbitrary", "arbitrary", "arbitrary"), vmem),
        name="fox_prompt_attention",
    )(q, k, v, aug, *rounding.operands())
    return outs[0], outs[1:]


def _page_suffix_kernel(lf_ref, out_ref, *, heads):
    x = lf_ref[...]
    cols = x.shape[1]
    col = lax.broadcasted_iota(jnp.int32, x.shape, 1)
    incl = x
    tot = x
    sh = heads
    while sh < cols:
        incl = incl + jnp.where(col + sh < cols, pltpu.roll(incl, cols - sh, 1), 0.0)
        tot = tot + pltpu.roll(tot, sh, 1)
        sh *= 2
    out_ref[:, :cols] = incl - x
    out_ref[:, cols:] = tot


def _page_suffix(cache_lf, heads):
    n_rows, cols = cache_lf.shape
    tr = 256 if n_rows % 256 == 0 else n_rows
    return pl.pallas_call(
        functools.partial(_page_suffix_kernel, heads=heads),
        out_shape=jax.ShapeDtypeStruct((n_rows, 2 * cols), F32),
        grid=(n_rows // tr,),
        in_specs=[pl.BlockSpec((tr, cols), lambda r: (r, 0))],
        out_specs=pl.BlockSpec((tr, 2 * cols), lambda r: (r, 0)),
        compiler_params=_params(("arbitrary",), 12 * _nbytes((tr, cols), F32) + (4 << 20)),
        name="fox_page_logf_suffix",
    )(cache_lf)


def _fox_decode_kernel(pt_ref, q_ref, kn_ref, vn_ref, cn_ref, *refs, pages, page_id):
    k_refs = refs[:pages]
    v_refs = refs[pages:2 * pages]
    bias_refs = refs[2 * pages:3 * pages]
    o_ref, m_ref, l_ref, acc_ref, run_ref = refs[3 * pages:]
    s_idx = pl.program_id(1)
    heads = q_ref.shape[0]
    cols = k_refs[0].shape[0]
    head_mask = (lax.broadcasted_iota(jnp.int32, (heads, cols), 1) % heads
                 == lax.broadcasted_iota(jnp.int32, (heads, cols), 0))

    @pl.when(s_idx == 0)
    def _():
        m_ref[...] = jnp.full_like(m_ref, NEG_BIG)
        l_ref[...] = jnp.zeros_like(l_ref)
        acc_ref[...] = jnp.zeros_like(acc_ref)
        run_ref[...] = jnp.zeros_like(run_ref)

    q = q_ref[...]
    cn = cn_ref[...]
    run = run_ref[...]
    scores = []
    for g in range(pages):
        s = lax.dot_general(q, k_refs[g][...].astype(BF16), (((1,), (1,)), ((), ())),
                            preferred_element_type=F32)
        row = page_id(pt_ref, pl.program_id(0), s_idx, g) % F32_SUBLANES
        page_bias = bias_refs[g][pl.ds(row, 1), :]
        scores.append(jnp.where(head_mask, s + (page_bias[:, :cols] + (run + cn)), NEG_BIG))
        run = run + page_bias[:, cols:]
    run_ref[...] = run
    m = m_ref[...]
    m_new = m
    for s in scores:
        m_new = jnp.maximum(m_new, jnp.max(s, axis=-1, keepdims=True))
    alpha = jnp.exp(m - m_new)
    l = alpha * l_ref[...]
    acc = alpha * acc_ref[...]
    for g in range(pages):
        p = jnp.exp(scores[g] - m_new)
        l = l + jnp.sum(p, axis=-1, keepdims=True)
        acc = acc + jnp.dot(p.astype(BF16), v_refs[g][...].astype(BF16), preferred_element_type=F32)
    m_ref[...] = m_new
    l_ref[...] = l
    acc_ref[...] = acc

    @pl.when(s_idx == pl.num_programs(1) - 1)
    def _():
        s_new = jnp.sum(q.astype(F32) * kn_ref[...], axis=-1, keepdims=True)
        m = m_ref[...]
        m_new = jnp.maximum(m, s_new)
        alpha = jnp.exp(m - m_new)
        p_new = jnp.exp(s_new - m_new)
        l = alpha * l_ref[...] + p_new
        acc = alpha * acc_ref[...] + p_new.astype(BF16).astype(F32) * vn_ref[...]
        o_ref[...] = acc / l


def _fox_decode(page_table, pool_base, q, k_new, v_new, logf_new, cache_k, cache_v, cache_lf):
    bs, n_pages = page_table.shape
    _, heads, hd = q.shape
    n_pool, cols, _ = cache_k.shape
    assert n_pool % F32_SUBLANES == 0 and pool_base % F32_SUBLANES == 0
    page_bias = _page_suffix(cache_lf, heads)
    g_pages = PAGES_PER_STEP if n_pages % PAGES_PER_STEP == 0 else 1
    steps = n_pages // g_pages

    def page_id(pt, b, s, g):
        return pool_base + pt[b * n_pages + (n_pages - 1 - (s * g_pages + g))]

    kernel = functools.partial(_fox_decode_kernel, pages=g_pages, page_id=page_id)

    def page_idx(g):
        return lambda b, s, pt: (page_id(pt, b, s, g), 0, 0)

    def row_block_idx(g):
        return lambda b, s, pt: (page_id(pt, b, s, g) // F32_SUBLANES, 0)

    tok = pl.BlockSpec((None, heads, hd), lambda b, s, pt: (b, 0, 0))
    in_specs = [tok, tok, tok, pl.BlockSpec((None, 1, cols), lambda b, s, pt: (b, 0, 0))]
    in_specs += [pl.BlockSpec((None, cols, hd), page_idx(g)) for g in range(g_pages)]
    in_specs += [pl.BlockSpec((None, cols, hd), page_idx(g)) for g in range(g_pages)]
    in_specs += [pl.BlockSpec((F32_SUBLANES, 2 * cols), row_block_idx(g)) for g in range(g_pages)]
    vmem = (4 * g_pages * _nbytes((cols, hd), F32) + 2 * g_pages * _nbytes((cols, hd), BF16)
            + 4 * g_pages * _nbytes((heads, cols), F32) + (4 << 20))
    grid_spec = pltpu.PrefetchScalarGridSpec(
        num_scalar_prefetch=1,
        grid=(bs, steps),
        in_specs=in_specs,
        out_specs=pl.BlockSpec((None, heads, hd), lambda b, s, pt: (b, 0, 0)),
        scratch_shapes=[pltpu.VMEM((heads, 1), F32), pltpu.VMEM((heads, 1), F32),
                        pltpu.VMEM((heads, hd), F32), pltpu.VMEM((1, cols), F32)])
    return pl.pallas_call(
        kernel,
        out_shape=jax.ShapeDtypeStruct((bs, heads, hd), F32),
        grid_spec=grid_spec,
        compiler_params=_params(("arbitrary", "arbitrary"), vmem),
        name="fox_decode_attention",
    )(page_table.reshape(-1), q, k_new, v_new, logf_new,
      *([cache_k] * g_pages), *([cache_v] * g_pages), *([page_bias] * g_pages))


def _proj_residual_kernel(a_ref, w_ref, x_ref, gate_ref, o_ref):
    z = jnp.dot(a_ref[...], w_ref[...].astype(BF16), preferred_element_type=F32)
    o_ref[...] = x_ref[...] + gate_ref[...] * z


def _proj_residual(a, w, layer, x, mods, mod_rows, tiles_per_group, gate_chunk, tm):
    m_rows, k = a.shape
    d = w.shape[2]
    tn = min(COL_TILE, d)
    vmem = (2 * _nbytes((tm, k), BF16) + 2 * _nbytes((k, tn), F32) + _nbytes((k, tn), BF16)
            + 5 * _nbytes((tm, tn), F32) + (4 << 20))
    return pl.pallas_call(
        _proj_residual_kernel,
        out_shape=jax.ShapeDtypeStruct((m_rows, d), F32),
        grid=(m_rows // tm, d // tn),
        in_specs=[pl.BlockSpec((tm, k), lambda m, n: (m, 0)),
                  pl.BlockSpec((None, k, tn), lambda m, n: (layer, 0, n)),
                  pl.BlockSpec((tm, tn), lambda m, n: (m, n)),
                  pl.BlockSpec((None, mod_rows, tn),
                               lambda m, n: (m // tiles_per_group, 0, gate_chunk * (d // tn) + n))],
        out_specs=pl.BlockSpec((tm, tn), lambda m, n: (m, n)),
        compiler_params=_params(("arbitrary", "arbitrary"), vmem),
        name="mixer_out_proj",
    )(a, w, x, mods)


def _mlp_kernel(x_ref, xs_ref, g_ref, mod_ref, mods_ref, wu_ref, wd_ref, fg_ref, o_ref, os_ref, h_ref, *,
                tm, final_norm):
    f = pl.program_id(1)
    d = x_ref.shape[1]

    @pl.when(f == 0)
    def _():
        h_ref[:tm, :] = _norm_mod(x_ref[...], g_ref[...], mod_ref[:, :d], mod_ref[:, d:2 * d]).astype(BF16)
        h_ref[tm:, :] = _norm_mod(xs_ref[...], g_ref[...], mods_ref[:, :d],
                                  mods_ref[:, d:2 * d]).astype(BF16)
        o_ref[...] = jnp.zeros_like(o_ref)
        os_ref[...] = jnp.zeros_like(os_ref)

    u = jnp.dot(h_ref[...], wu_ref[...].astype(BF16), preferred_element_type=F32)
    u = jnp.square(jnp.maximum(u, 0.0)).astype(BF16)
    z = jnp.dot(u, wd_ref[...].astype(BF16), preferred_element_type=F32)
    o_ref[...] += z[:tm]
    os_ref[...] += z[tm:]

    @pl.when(f == pl.num_programs(1) - 1)
    def _():
        for x_r, mod_r, o_r in ((x_ref, mod_ref, o_ref), (xs_ref, mods_ref, os_ref)):
            y = x_r[...] + mod_r[:, 2 * d:] * o_r[...]
            if final_norm:
                y = y * lax.rsqrt(jnp.mean(y * y, axis=-1, keepdims=True) + EPS) * fg_ref[...]
            o_r[...] = y


def _mlp(x, xs, g, mods_p, mods_s, tiles_per_batch, w_up, w_down, layer, final_g, final_norm, tm):
    m_rows, d = x.shape
    srows = xs.shape[0]
    d_ff = w_up.shape[2]
    tf = min(COL_TILE, d_ff)
    kernel = functools.partial(_mlp_kernel, tm=tm, final_norm=final_norm)
    x_buffers = 1 if w_up.dtype == F32 else 2
    x_mode = {"pipeline_mode": pl.Buffered(1)} if x_buffers == 1 else {}
    vmem = ((2 + x_buffers) * _nbytes((tm, d), F32) + _nbytes((tm + srows, d), BF16)
            + 4 * _nbytes((d, tf), w_up.dtype) + 2 * _nbytes((d, tf), BF16) + 4 * _nbytes((tm, tf), F32)
            + (4 << 20))
    full = lambda m, f: (0, 0)
    vec = pl.BlockSpec((1, d), full)
    return pl.pallas_call(
        kernel,
        out_shape=(jax.ShapeDtypeStruct((m_rows, d), F32), jax.ShapeDtypeStruct((srows, d), F32)),
        grid=(m_rows // tm, d_ff // tf),
        in_specs=[pl.BlockSpec((tm, d), lambda m, f: (m, 0), **x_mode),
                  pl.BlockSpec((srows, d), full),
                  vec,
                  _mod_spec(1, d, 3, tiles_per_batch, 3),
                  _mod_spec(srows, d, 3, m_rows, 3),
                  pl.BlockSpec((None, d, tf), lambda m, f: (layer, 0, f)),
                  pl.BlockSpec((None, tf, d), lambda m, f: (layer, f, 0)),
                  vec],
        out_specs=(pl.BlockSpec((tm, d), lambda m, f: (m, 0)), pl.BlockSpec((srows, d), full)),
        scratch_shapes=[pltpu.VMEM((tm + srows, d), BF16)],
        compiler_params=_params(("arbitrary", "arbitrary"), vmem),
        name="relu2_mlp",
    )(x, xs, g, mods_p, mods_s, w_up, w_down, final_g)


def _ret_proj_kernel(x_ref, xs_ref, g_ref, mod_ref, mods_ref, w_ref, cs_ref, css_ref,
                     q_ref, k_ref, v_ref, gt_ref, zs_ref, h_ref, *, nq, tm, key_dim, k_scale):
    n = pl.program_id(1)
    d = x_ref.shape[1]

    @pl.when(n == 0)
    def _():
        h_ref[:tm, :] = _norm_mod(x_ref[...], g_ref[...], mod_ref[:, :d], mod_ref[:, d:]).astype(BF16)
        h_ref[tm:, :] = _norm_mod(xs_ref[...], g_ref[...], mods_ref[:, :d], mods_ref[:, d:]).astype(BF16)

    z = jnp.dot(h_ref[...], w_ref[...].astype(BF16), preferred_element_type=F32)
    zp = z[:tm]
    zs = z[tm:]
    half = key_dim // 2

    def rotate(zz, table_ref):
        cos = table_ref[:, :half]
        sin = table_ref[:, half:]
        parts = []
        for hh in range(zz.shape[1] // key_dim):
            x1 = zz[:, hh * key_dim:hh * key_dim + half]
            x2 = zz[:, hh * key_dim + half:(hh + 1) * key_dim]
            parts += [x1 * cos - x2 * sin, x1 * sin + x2 * cos]
        return jnp.concatenate(parts, axis=1)

    last_tile = pl.program_id(0) == pl.num_programs(0) - 1
    in_q = n < nq
    in_k = jnp.logical_and(n >= nq, n < 2 * nq)

    @pl.when(in_q)
    def _():
        q_ref[...] = rotate(zp, cs_ref).astype(BF16)

    @pl.when(in_k)
    def _():
        k_ref[...] = rotate(zp, cs_ref) * k_scale

    @pl.when(jnp.logical_and(n >= 2 * nq, n < 4 * nq))
    def _():
        v_ref[...] = zp.astype(BF16)

    @pl.when(n >= 4 * nq)
    def _():
        gt_ref[...] = zp

    @pl.when(jnp.logical_and(last_tile, n < 2 * nq))
    def _():
        zs_ref[...] = rotate(zs, css_ref) * jnp.where(in_k, k_scale, 1.0)

    @pl.when(jnp.logical_and(last_tile, n >= 2 * nq))
    def _():
        zs_ref[...] = zs


def _ret_proj(x, xs, g, mods_p, mods_s, tiles_per_batch, w_in, layer, rot, rot_s, key_dim, tm):
    m_rows, d = x.shape
    srows = xs.shape[0]
    tn = min(COL_TILE, d)
    nq = d // tn
    kernel = functools.partial(_ret_proj_kernel, nq=nq, tm=tm, key_dim=key_dim, k_scale=key_dim ** -0.5)
    last_tile = m_rows // tm - 1

    def col(lo, width):
        return lambda m, n: (m, jnp.clip(n - lo, 0, width - 1))

    vmem = (2 * _nbytes((tm, d), F32) + _nbytes((tm + srows, d), BF16) + 2 * _nbytes((d, tn), F32)
            + _nbytes((d, tn), BF16) + 2 * (2 * _nbytes((tm, tn), BF16) + 2 * _nbytes((tm, tn), F32))
            + 2 * _nbytes((tm, key_dim), F32) + 4 * _nbytes((tm, tn), F32) + (4 << 20))
    full = lambda m, n: (0, 0)
    outs = pl.pallas_call(
        kernel,
        out_shape=(jax.ShapeDtypeStruct((m_rows, d), BF16), jax.ShapeDtypeStruct((m_rows, d), F32),
                   jax.ShapeDtypeStruct((m_rows, 2 * d), BF16), jax.ShapeDtypeStruct((m_rows, 2 * d), F32),
                   jax.ShapeDtypeStruct((srows, 6 * d), F32)),
        grid=(m_rows // tm, 6 * nq),
        in_specs=[pl.BlockSpec((tm, d), lambda m, n: (m, 0)),
                  pl.BlockSpec((srows, d), full),
                  pl.BlockSpec((1, d), full),
                  _mod_spec(1, d, 0, tiles_per_batch, 2),
                  _mod_spec(srows, d, 0, m_rows, 2),
                  pl.BlockSpec((None, d, tn), lambda m, n: (layer, 0, n)),
                  pl.BlockSpec((tm, key_dim), lambda m, n: (m % tiles_per_batch, 0)),
                  pl.BlockSpec((srows, key_dim), full)],
        out_specs=(pl.BlockSpec((tm, tn), col(0, nq)),
                   pl.BlockSpec((tm, tn), col(nq, nq)),
                   pl.BlockSpec((tm, tn), col(2 * nq, 2 * nq)),
                   pl.BlockSpec((tm, tn), col(4 * nq, 2 * nq)),
                   pl.BlockSpec((srows, tn), lambda m, n: (0, jnp.where(m == last_tile, n, 0)))),
        scratch_shapes=[pltpu.VMEM((tm + srows, d), BF16)],
        compiler_params=_params(("arbitrary", "arbitrary"), vmem),
        name="ret_in_proj",
    )(x, xs, g, mods_p, mods_s, w_in, rot, rot_s)
    return outs[:4], outs[4]


def _group_norm_gate(o, gate):
    mu = jnp.mean(o, axis=-1, keepdims=True)
    cen = o - mu
    var = jnp.mean(cen * cen, axis=-1, keepdims=True)
    return _silu(gate) * (cen * lax.rsqrt(var + EPS))


def _ret_chunk_kernel(lg_ref, q_ref, k_ref, v_ref, gt_ref, *rest, n_round):
    y_ref, s_ref = rest[n_round:n_round + 2]
    dec_ref, xi_ref, zeta_ref = rest[-3:]
    _WeightRounding.run(rest[:n_round], rest[n_round + 2:-3])
    lg = jnp.full((1, 1), lg_ref[pl.program_id(1)], F32)
    ln = q_ref.shape[0]

    @pl.when(pl.program_id(2) == 0)
    def _():
        s_ref[...] = jnp.zeros_like(s_ref)
        diff = (lax.broadcasted_iota(jnp.int32, (ln, ln), 0)
                - lax.broadcasted_iota(jnp.int32, (ln, ln), 1)).astype(F32)
        dec_ref[...] = jnp.where(diff >= 0, jnp.exp(lg * jnp.maximum(diff, 0.0)), 0.0)
        n = lax.broadcasted_iota(jnp.int32, (ln, 1), 0).astype(F32)
        xi_ref[...] = jnp.exp(lg * (n + 1.0))
        zeta_ref[...] = jnp.exp(lg * (ln - 1.0 - n))

    q = q_ref[...]
    k = k_ref[...]
    v = v_ref[...]
    s0 = s_ref[...]
    qk = lax.dot_general(q, k.astype(BF16), (((1,), (1,)), ((), ())), preferred_element_type=F32)
    a = (qk * dec_ref[...]).astype(BF16)
    o = (jnp.dot(a, v, preferred_element_type=F32)
         + jnp.dot(q, s0.astype(BF16), preferred_element_type=F32) * xi_ref[...])
    kz = (k * zeta_ref[...]).astype(BF16)
    s_ref[...] = jnp.exp(lg * ln) * s0 + lax.dot_general(
        kz, v, (((0,), (0,)), ((), ())), preferred_element_type=F32)
    y_ref[...] = _group_norm_gate(o, gt_ref[...]).astype(y_ref.dtype)


def _ret_prompt(lg, q, k, v, gate, batch, seq, heads, key_dim, val_dim, weights_to_round):
    ln = min(RET_CHUNK, seq)
    nc = seq // ln
    rounding = _WeightRounding(weights_to_round, batch * heads * nc,
                               lambda b, h, c, *_: (b * heads + h) * nc + c)
    vmem = (2 * (_nbytes((ln, key_dim), BF16) + _nbytes((ln, key_dim), F32) + _nbytes((ln, val_dim), BF16)
                 + _nbytes((ln, val_dim), F32) + _nbytes((ln, val_dim), BF16))
            + 4 * _nbytes((key_dim, val_dim), F32) + 8 * _nbytes((ln, val_dim), F32)
            + rounding.vmem_bytes() + (8 << 20))
    grid_spec = pltpu.PrefetchScalarGridSpec(
        num_scalar_prefetch=1,
        grid=(batch, heads, nc),
        in_specs=[pl.BlockSpec((ln, key_dim), lambda b, h, c, lg: (b * nc + c, h)),
                  pl.BlockSpec((ln, key_dim), lambda b, h, c, lg: (b * nc + c, h)),
                  pl.BlockSpec((ln, val_dim), lambda b, h, c, lg: (b * nc + c, h)),
                  pl.BlockSpec((ln, val_dim), lambda b, h, c, lg: (b * nc + c, h))] + rounding.in_specs(),
        out_specs=[pl.BlockSpec((ln, val_dim), lambda b, h, c, lg: (b * nc + c, h)),
                   pl.BlockSpec((None, None, key_dim, val_dim), lambda b, h, c, lg: (b, h, 0, 0))]
        + rounding.out_specs(),
        scratch_shapes=[pltpu.VMEM((ln, ln), F32), pltpu.VMEM((ln, 1), F32), pltpu.VMEM((ln, 1), F32)])
    outs = pl.pallas_call(
        functools.partial(_ret_chunk_kernel, n_round=len(rounding)),
        out_shape=[jax.ShapeDtypeStruct((batch * seq, heads * val_dim), BF16),
                   jax.ShapeDtypeStruct((batch, heads, key_dim, val_dim), F32)] + rounding.out_shapes(),
        grid_spec=grid_spec,
        compiler_params=_params(("arbitrary", "arbitrary", "arbitrary"), vmem),
        name="ret_prompt_chunks",
    )(lg, q, k, v, gate, *rounding.operands())
    return outs[0], outs[1], outs[2:]


def _ret_decode_kernel(lg_ref, q_ref, k_ref, v_ref, gt_ref, s0_ref, y_ref, s_ref):
    q = q_ref[...]
    k = k_ref[...]
    q_cols = q.T
    k_cols = k.T
    a_all = jnp.sum(q * k, axis=-1, keepdims=True)
    for h in range(s0_ref.shape[0]):
        gamma = jnp.exp(jnp.full((1, 1), lg_ref[h], F32))
        s0 = s0_ref[h]
        v = v_ref[h:h + 1, :]
        qs = jnp.sum(q_cols[:, h:h + 1] * s0, axis=0, keepdims=True)
        o = a_all[h:h + 1, :] * v + qs * gamma
        s_ref[h] = gamma * s0 + k_cols[:, h:h + 1] * v
        y_ref[h:h + 1, :] = _group_norm_gate(o, gt_ref[h:h + 1, :])


def _ret_decode(lg, q, k, v, gate, state):
    bs, heads, key_dim, val_dim = state.shape

    def vec(width):
        return pl.BlockSpec((None, heads, width), lambda b, lg: (b, 0, 0))

    state_spec = pl.BlockSpec((None, heads, key_dim, val_dim), lambda b, lg: (b, 0, 0, 0))
    grid_spec = pltpu.PrefetchScalarGridSpec(
        num_scalar_prefetch=1,
        grid=(bs,),
        in_specs=[vec(key_dim), vec(key_dim), vec(val_dim), vec(val_dim), state_spec],
        out_specs=(vec(val_dim), state_spec))
    y, s = pl.pallas_call(
        _ret_decode_kernel,
        out_shape=(jax.ShapeDtypeStruct((bs, heads, val_dim), F32),
                   jax.ShapeDtypeStruct(state.shape, F32)),
        grid_spec=grid_spec,
        compiler_params=_params(("arbitrary",), 4 * _nbytes(state.shape[1:], F32) + (16 << 20)),
        name="ret_decode_step",
    )(lg, q.reshape(bs, heads, key_dim), k.reshape(bs, heads, key_dim),
      v.reshape(bs, heads, val_dim), gate.reshape(bs, heads, val_dim), state)
    return y.reshape(bs, heads * val_dim), s


def _rotary_table(pos, key_dim):
    half = key_dim // 2
    inv = (np.float32(1.0) / np.float32(ROPE_BASE) ** np.linspace(0.0, 1.0, half, dtype=np.float32))
    ang = (np.asarray(pos, np.float32)[:, None] * inv.astype(np.float32)[None, :]).astype(np.float64)
    return jnp.asarray(np.concatenate([np.cos(ang), np.sin(ang)], axis=1), F32)


def _pad_rows(a, rows):
    return jnp.pad(a, ((0, rows - a.shape[0]), (0, 0)))


def kernel(x_prompt, x_sample, c_prompt, c_sample, cache_fox_k, cache_fox_v, cache_fox_logf, state_ret, page_table, norm1_g, norm2_g, ada_w, ada_b, fox_w_in, fox_b_f, fox_w_out, ret_w_in, ret_w_out, mlp_w_up, mlp_w_down, final_g):
    bp, tp, d = x_prompt.shape
    bs, ts, _ = x_sample.shape
    assert ts == 1, "the decode kernels handle one new token per sequence"
    depth = ada_w.shape[0]
    _, n_pool, page, fox_heads, fox_hd = cache_fox_k.shape
    _, _, ret_heads, key_dim, val_dim = state_ret.shape
    past_len = page_table.shape[1] * page
    srows = -(-bs // BF16_SUBLANES) * BF16_SUBLANES
    tm = min(ROW_TILE, tp)
    tiles_p = tp // tm

    mod_rows = -(-(bs + bp) // 8) * 8
    c_all = _pad_rows(jnp.concatenate([c_sample, c_prompt], axis=0), max(mod_rows, srows))
    mods = _ada(c_all, ada_w, ada_b)

    lg = jnp.log1p(-jnp.exp2(-5.0 - jnp.arange(ret_heads, dtype=F32)))
    rot_p = _rotary_table(np.arange(tp), key_dim)
    rot_s = _rotary_table(np.full((srows,), past_len), key_dim)

    yp = x_prompt.reshape(bp * tp, d)
    ys = _pad_rows(x_sample.reshape(bs, d), srows)
    fg = final_g.reshape(1, d)
    outs = {name: [] for name in ("kp", "vp", "lfp", "ks", "vs", "lfs", "sp", "ss")}

    ret_rounded = None
    for i in range(depth):
        mods_p = mods[i, bs:bs + bp].reshape(bp, 1, 6 * d)
        mods_s = mods[i, :srows].reshape(1, srows, 6 * d)
        g1 = norm1_g[i].reshape(1, d)
        g2 = norm2_g[i].reshape(1, d)
        j = i // 2
        if i % 2 == 0:
            b_f = fox_b_f[j].reshape(1, fox_heads)
            (q, kf, vf, kb, vb, lf), sample = _fox_proj(
                yp, ys, g1, mods_p, mods_s, tiles_p, fox_w_in, j, b_f, tm,
                fox_hd ** -0.5 * LOG2E, fox_hd ** -0.5)
            aug = _cumsum(lf, bp, tp, fox_hd)
            to_round = [(fox_w_out, j), (mlp_w_up, i), (mlp_w_down, i)]
            if i + 1 < depth:
                jr = (i + 1) // 2
                to_round += [(ret_w_in, jr), (ret_w_out, jr), (mlp_w_up, i + 1), (mlp_w_down, i + 1)]
            o, rounded = _fox_attn(q, kb, vb, aug, bp, tp, fox_heads, fox_hd, to_round)
            w_out, w_up, w_down = rounded[:3]
            l_mlp = 0
            ret_rounded = rounded[3:] if i + 1 < depth else None
            yp = _proj_residual(o, w_out, 0, yp, mods_p, 1, tiles_p, 2, tm)
            outs["kp"].append(kf.reshape(bp, tp, fox_heads, fox_hd))
            outs["vp"].append(vf.reshape(bp, tp, fox_heads, fox_hd))
            outs["lfp"].append(lf.reshape(bp, tp, fox_heads))
            zs, lf = sample
            kf, vf = zs[:, d:2 * d], zs[:, 2 * d:]
            bf16_rounded = lambda a: a.astype(BF16).astype(F32)
            o = _fox_decode(
                page_table, j * n_pool,
                zs[:bs, :d].astype(BF16).reshape(bs, fox_heads, fox_hd),
                bf16_rounded(kf[:bs]).reshape(bs, fox_heads, fox_hd),
                bf16_rounded(vf[:bs]).reshape(bs, fox_heads, fox_hd),
                jnp.tile(lf[:bs], (1, page)).reshape(bs, 1, page * fox_heads),
                cache_fox_k.reshape(-1, page * fox_heads, fox_hd),
                cache_fox_v.reshape(-1, page * fox_heads, fox_hd),
                cache_fox_logf.reshape(-1, page * fox_heads))
            o = _pad_rows(o.reshape(bs, d), srows).astype(BF16)
            ys = _proj_residual(o, w_out, 0, ys, mods_s, srows, 1, 2, srows)
            outs["ks"].append(kf[:bs].reshape(bs, ts, fox_heads, fox_hd))
            outs["vs"].append(vf[:bs].reshape(bs, ts, fox_heads, fox_hd))
            outs["lfs"].append(lf[:bs].reshape(bs, ts, fox_heads))
        else:
            if ret_rounded is None:
                w_in, w_out, w_up, w_down = ret_w_in, ret_w_out, mlp_w_up, mlp_w_down
                l_mix, l_mlp = j, i
            else:
                w_in, w_out, w_up, w_down = ret_rounded
                l_mix = l_mlp = 0
            (q, k, v, gate), zs = _ret_proj(yp, ys, g1, mods_p, mods_s, tiles_p, w_in, l_mix,
                                            rot_p, rot_s, key_dim, tm)
            y, s, _ = _ret_prompt(lg, q, k, v, gate, bp, tp, ret_heads, key_dim, val_dim, [])
            yp = _proj_residual(y, w_out, l_mix, yp, mods_p, 1, tiles_p, 2, tm)
            outs["sp"].append(s)
            zs = zs[:bs]
            bf16_rounded = lambda a: a.astype(BF16).astype(F32)
            y, s = _ret_decode(lg, bf16_rounded(zs[:, :d]), zs[:, d:2 * d], bf16_rounded(zs[:, 2 * d:4 * d]),
                               zs[:, 4 * d:], state_ret[j].astype(F32))
            ys = _proj_residual(_pad_rows(y, srows).astype(BF16), w_out, l_mix, ys, mods_s, srows, 1, 2, srows)
            outs["ss"].append(s)
        yp, ys = _mlp(yp, ys, g2, mods_p, mods_s, tiles_p, w_up, w_down, l_mlp, fg, i == depth - 1, tm)

    return (yp.reshape(bp, tp, d), ys[:bs].reshape(bs, ts, d),
            jnp.stack(outs["kp"]), jnp.stack(outs["vp"]), jnp.stack(outs["lfp"]),
            jnp.stack(outs["ks"]), jnp.stack(outs["vs"]), jnp.stack(outs["lfs"]),
            jnp.stack(outs["sp"]), jnp.stack(outs["ss"]))
```

```python
import functools

import jax
import jax.numpy as jnp
import numpy as np
from jax import lax
from jax.experimental import pallas as pl
from jax.experimental.pallas import tpu as pltpu

F32 = jnp.float32
BF16 = jnp.bfloat16

EPS = 1e-6
ROPE_BASE = 10000.0
LOG2E = 1.4426950408889634
NEG_BIG = -1e30

V7X_VMEM_LIMIT_BYTES = 60000 * 1024
F32_SUBLANES = 8
BF16_SUBLANES = 16

ROW_TILE = 1024
COL_TILE = 512
ATTN_Q_TILE = 1024
ATTN_K_TILE = 1024
ATTN_K_SUB = 512
ATTN_HEADS_PER_STEP = 2
CUMSUM_TILE = 512
RET_HEADS_PER_STEP = 2
RET_CHUNK = 512
PAGES_PER_STEP = 8


def _params(semantics, vmem_bytes):
    return pltpu.CompilerParams(dimension_semantics=semantics,
                                vmem_limit_bytes=int(min(vmem_bytes, V7X_VMEM_LIMIT_BYTES)))


def _nbytes(shape, dtype):
    n = 1
    for s in shape:
        n *= s
    return n * jnp.dtype(dtype).itemsize


def _norm_mod(x, g, shift, scale):
    y = x * lax.rsqrt(jnp.mean(x * x, axis=-1, keepdims=True) + EPS)
    return y * (g * (1.0 + scale)) + shift


def _silu(x):
    return x * jax.nn.sigmoid(x)


def _log_sigmoid(x):
    return jnp.minimum(x, 0.0) - jnp.log1p(jnp.exp(-jnp.abs(x)))


def _split3(x):
    hi = x.astype(BF16)
    r1 = x - hi.astype(F32)
    mid = r1.astype(BF16)
    lo = (r1 - mid.astype(F32)).astype(BF16)
    return hi, mid, lo


class _WeightRounding:
    def __init__(self, weights, n_steps, step_index):
        self.items = []
        for w, layer in weights:
            slab = w.shape[1] // n_steps
            assert w.shape[1] % n_steps == 0 and slab % BF16_SUBLANES == 0, (w.shape, n_steps)
            self.items.append((w, layer, slab))
        self.step_index = step_index

    def __len__(self):
        return len(self.items)

    def in_specs(self):
        return [pl.BlockSpec((None, slab, w.shape[2]),
                             lambda *ids, layer=layer: (layer, self.step_index(*ids), 0))
                for w, layer, slab in self.items]

    def out_specs(self):
        return [pl.BlockSpec((None, slab, w.shape[2]), lambda *ids: (0, self.step_index(*ids), 0))
                for w, _, slab in self.items]

    def out_shapes(self):
        return [jax.ShapeDtypeStruct((1,) + w.shape[1:], BF16) for w, _, _ in self.items]

    def operands(self):
        return [w for w, _, _ in self.items]

    def vmem_bytes(self):
        return sum(2 * _nbytes((slab, w.shape[2]), F32) + 2 * _nbytes((slab, w.shape[2]), BF16)
                   for w, _, slab in self.items)

    @staticmethod
    def run(in_refs, out_refs):
        for i_ref, o_ref in zip(in_refs, out_refs):
            o_ref[...] = i_ref[...].astype(BF16)


def _mod_spec(rows, width, chunk, tiles_per_group, n_chunks=1):
    assert chunk % n_chunks == 0
    return pl.BlockSpec((None, rows, n_chunks * width),
                        lambda m, n: (m // tiles_per_group, 0, chunk // n_chunks))


def _ada_kernel(c_ref, w_ref, b_ref, o_ref):
    o_ref[...] = jnp.dot(_silu(c_ref[...]), w_ref[...], preferred_element_type=F32) + b_ref[...]


def _ada(c_all, ada_w, ada_b):
    depth, d, n6 = ada_w.shape
    rows = c_all.shape[0]
    tn = min(1024, n6)
    vmem = 2 * (_nbytes((d, tn), F32) + _nbytes((rows, tn), F32)) + _nbytes((rows, d), F32) * 2 + (4 << 20)
    return pl.pallas_call(
        _ada_kernel,
        out_shape=jax.ShapeDtypeStruct((depth, rows, n6), F32),
        grid=(depth, n6 // tn),
        in_specs=[pl.BlockSpec((rows, d), lambda i, n: (0, 0)),
                  pl.BlockSpec((None, d, tn), lambda i, n: (i, 0, n)),
                  pl.BlockSpec((None, 1, tn), lambda i, n: (i, 0, n))],
        out_specs=pl.BlockSpec((None, rows, tn), lambda i, n: (i, 0, n)),
        compiler_params=_params(("arbitrary", "arbitrary"), vmem),
        name="ada_modulation",
    )(c_all, ada_w, ada_b.reshape(depth, 1, n6))


def _fox_proj_kernel(x_ref, xs_ref, g_ref, mod_ref, mods_ref, w_ref, wf_ref, bf_ref,
                     q_ref, kf_ref, vf_ref, kb_ref, vb_ref, lf_ref, zs_ref, lfs_ref, h_ref, *,
                     nq, tm, q_scale, q_scale_s):
    n = pl.program_id(1)
    d = x_ref.shape[1]

    @pl.when(n == 0)
    def _():
        w_f = wf_ref[...].astype(BF16)
        for x_r, mod_r, lf_r, rows in ((x_ref, mod_ref, lf_ref, slice(0, tm)),
                                       (xs_ref, mods_ref, lfs_ref, slice(tm, None))):
            h = _norm_mod(x_r[...], g_ref[...], mod_r[:, :d], mod_r[:, d:]).astype(BF16)
            h_ref[rows, :] = h
            fl = lax.dot_general(h, w_f, (((1,), (1,)), ((), ())), preferred_element_type=F32)
            lf_r[...] = _log_sigmoid(fl + bf_ref[...])

    z = lax.dot_general(h_ref[...], w_ref[...].astype(BF16), (((1,), (1,)), ((), ())),
                        preferred_element_type=F32)
    zp = z[:tm]
    zs = z[tm:]
    in_q = n < nq

    @pl.when(in_q)
    def _():
        q_ref[...] = (zp * q_scale).astype(BF16)

    @pl.when(jnp.logical_and(n >= nq, n < 2 * nq))
    def _():
        kf_ref[...] = zp
        kb_ref[...] = zp.astype(BF16)

    @pl.when(n >= 2 * nq)
    def _():
        vf_ref[...] = zp
        vb_ref[...] = zp.astype(BF16)

    @pl.when(pl.program_id(0) == pl.num_programs(0) - 1)
    def _():
        zs_ref[...] = zs * jnp.where(in_q, q_scale_s, 1.0)


def _fox_proj(x, xs, g, mods_p, mods_s, tiles_per_batch, w_in, layer, b_f, tm, q_scale, q_scale_s):
    m_rows, d = x.shape
    srows = xs.shape[0]
    heads = b_f.shape[1]
    assert w_in.shape[2] - 3 * d == heads and (3 * d) % heads == 0
    w_t = jnp.swapaxes(w_in, 1, 2)
    tn = min(COL_TILE, d)
    nq = d // tn
    kernel = functools.partial(_fox_proj_kernel, nq=nq, tm=tm, q_scale=q_scale, q_scale_s=q_scale_s)

    last_tile = m_rows // tm - 1

    def col(lo):
        return lambda m, n: (m, jnp.clip(n - lo, 0, nq - 1))

    vmem = (2 * _nbytes((tm, d), F32) + _nbytes((tm + srows, d), BF16) + 2 * _nbytes((d, tn), F32)
            + _nbytes((d, tn), BF16) + 2 * (3 * _nbytes((tm, tn), BF16) + 2 * _nbytes((tm, tn), F32))
            + 3 * _nbytes((tm, tn), F32) + (4 << 20))
    out_f = jax.ShapeDtypeStruct((m_rows, d), F32)
    out_b = jax.ShapeDtypeStruct((m_rows, d), BF16)
    full = lambda m, n: (0, 0)
    outs = pl.pallas_call(
        kernel,
        out_shape=(out_b, out_f, out_f, out_b, out_b, jax.ShapeDtypeStruct((m_rows, heads), F32),
                   jax.ShapeDtypeStruct((srows, 3 * d), F32), jax.ShapeDtypeStruct((srows, heads), F32)),
        grid=(m_rows // tm, 3 * nq),
        in_specs=[pl.BlockSpec((tm, d), lambda m, n: (m, 0)),
                  pl.BlockSpec((srows, d), full),
                  pl.BlockSpec((1, d), full),
                  _mod_spec(1, d, 0, tiles_per_batch, 2),
                  _mod_spec(srows, d, 0, m_rows, 2),
                  pl.BlockSpec((None, tn, d), lambda m, n: (layer, n, 0)),
                  pl.BlockSpec((None, heads, d), lambda m, n: (layer, 3 * d // heads, 0)),
                  pl.BlockSpec((1, heads), full)],
        out_specs=(pl.BlockSpec((tm, tn), col(0)),
                   pl.BlockSpec((tm, tn), col(nq)),
                   pl.BlockSpec((tm, tn), col(2 * nq)),
                   pl.BlockSpec((tm, tn), col(nq)),
                   pl.BlockSpec((tm, tn), col(2 * nq)),
                   pl.BlockSpec((tm, heads), lambda m, n: (m, 0)),
                   pl.BlockSpec((srows, tn), lambda m, n: (0, jnp.where(m == last_tile, n, 0))),
                   pl.BlockSpec((srows, heads), full)),
        scratch_shapes=[pltpu.VMEM((tm + srows, d), BF16)],
        compiler_params=_params(("arbitrary", "arbitrary"), vmem),
        name="fox_in_proj",
    )(x, xs, g, mods_p, mods_s, w_t, w_t, b_f)
    return outs[:6], outs[6:]


def _cumsum_kernel(lf_ref, aug_ref, carry_ref, *, hd):
    @pl.when(pl.program_id(1) == 0)
    def _():
        carry_ref[...] = jnp.zeros_like(carry_ref)

    tc, heads = lf_ref.shape
    row = lax.broadcasted_iota(jnp.int32, (tc, tc), 0)
    col = lax.broadcasted_iota(jnp.int32, (tc, tc), 1)
    tri = jnp.where(col <= row, 1.0, 0.0).astype(BF16)
    c = carry_ref[...]
    for piece in _split3(lf_ref[...]):
        c = c + jnp.dot(tri, piece, preferred_element_type=F32)
    carry_ref[...] = c[tc - 1:tc, :]

    row = lax.broadcasted_iota(jnp.int32, (3 * heads, heads * hd), 0)
    lane = lax.broadcasted_iota(jnp.int32, (3 * heads, heads * hd), 1)
    place = jnp.where(lane == (row % heads) * hd + row // heads, 1.0, 0.0).astype(BF16)
    pieces = jnp.concatenate(_split3(c * (-LOG2E)), axis=1)
    aug_ref[...] = jnp.dot(pieces, place, preferred_element_type=F32).astype(BF16)


def _cumsum(logf, batch, seq, hd):
    rows, heads = logf.shape
    tc = min(CUMSUM_TILE, seq)
    per = seq // tc
    return pl.pallas_call(
        functools.partial(_cumsum_kernel, hd=hd),
        out_shape=jax.ShapeDtypeStruct((rows, heads * hd), BF16),
        grid=(batch, per),
        in_specs=[pl.BlockSpec((tc, heads), lambda b, t: (b * per + t, 0))],
        out_specs=pl.BlockSpec((tc, heads * hd), lambda b, t: (b * per + t, 0)),
        scratch_shapes=[pltpu.VMEM((1, heads), F32)],
        compiler_params=_params(("arbitrary", "arbitrary"), 24 << 20),
        name="fox_logf_cumsum",
    )(logf)


def _fox_attn_kernel(q_ref, k_ref, v_ref, aug_ref, *rest, tq, tk, ts, hd, heads_per_step, n_round):
    o_ref, vt_ref = rest[n_round], rest[-1]
    _WeightRounding.run(rest[:n_round], rest[n_round + 1:-1])
    i = pl.program_id(2)
    seq = k_ref.shape[0]
    q0 = pl.multiple_of(i * tq, tq)
    lane_tile = 128

    @pl.when(i == 0)
    def _():
        for t in range(seq // tk):
            vt_ref[:, t * tk:(t + 1) * tk] = v_ref[t * tk:(t + 1) * tk, :].astype(F32).T.astype(BF16)

    ones = jnp.where(lax.broadcasted_iota(jnp.int32, (tq, hd), 1) < 3, 1.0, 0.0).astype(BF16)
    q_ext = [jnp.concatenate([q_ref[:, hh * hd:(hh + 1) * hd], ones], axis=1)
             for hh in range(heads_per_step)]

    def step(j, carry, masked):
        carry = list(carry)
        items = [(sub, hh) for sub in range(tk // ts) for hh in range(heads_per_step)]
        n_tiles = tq // lane_tile

        def tile_range(sub):
            if masked and tq == tk:
                return sub * ts // lane_tile, min((sub + 1) * ts // lane_tile, n_tiles)
            return 0, (n_tiles if masked else 0)

        def scores(sub, hh):
            k0 = pl.multiple_of(j * tk + sub * ts, ts)
            lanes = slice(hh * hd, (hh + 1) * hd)
            k_ext = jnp.concatenate([k_ref[pl.ds(k0, ts), lanes], aug_ref[pl.ds(k0, ts), lanes]], axis=1)
            return lax.dot_general(k_ext, q_ext[hh][tile_range(sub)[0] * lane_tile:],
                                   (((1,), (1,)), ((), ())), preferred_element_type=F32)

        s_next = scores(*items[0])
        for n, (sub, hh) in enumerate(items):
            s = s_next
            if n + 1 < len(items):
                s_next = scores(*items[n + 1])
            m, l, acc = carry[3 * hh:3 * hh + 3]
            k0 = pl.multiple_of(j * tk + sub * ts, ts)
            first, full = tile_range(sub)
            lo = first * lane_tile
            m_t, l_t, p_t = ([m[:, :lo]], [l[:, :lo]], []) if lo else ([], [], [])
            for qt in range(first, n_tiles):
                ql = slice(qt * lane_tile, (qt + 1) * lane_tile)
                sq = s[:, (qt - first) * lane_tile:(qt - first + 1) * lane_tile]
                if qt < full:
                    keep = (k0 + lax.broadcasted_iota(jnp.int32, (ts, lane_tile), 0)
                            <= q0 + qt * lane_tile + lax.broadcasted_iota(jnp.int32, (ts, lane_tile), 1))
                    sq = jnp.where(keep, sq, NEG_BIG)
                mq = jnp.maximum(m[:, ql], jnp.max(sq, axis=0, keepdims=True))
                pq = jnp.exp2(sq - mq)
                l_t.append(jnp.exp2(m[:, ql] - mq) * l[:, ql] + jnp.sum(pq, axis=0, keepdims=True))
                m_t.append(mq)
                p_t.append(pq.astype(BF16))
            m_new = jnp.concatenate(m_t, axis=1)
            alpha = jnp.exp2(m[:, lo:] - m_new[:, lo:])
            vt = vt_ref[hh * hd:(hh + 1) * hd, pl.ds(k0, ts)]
            acc_hi = alpha * acc[:, lo:] + jnp.dot(vt, jnp.concatenate(p_t, axis=1),
                                                   preferred_element_type=F32)
            carry[3 * hh:3 * hh + 3] = [m_new, jnp.concatenate(l_t, axis=1),
                                        jnp.concatenate([acc[:, :lo], acc_hi], axis=1) if lo else acc_hi]
        return tuple(carry)

    init = (jnp.full((1, tq), NEG_BIG, F32), jnp.zeros((1, tq), F32),
            jnp.zeros((hd, tq), F32)) * heads_per_step
    n_full = q0 // tk
    carry = lax.fori_loop(0, n_full, lambda j, c: step(j, c, False), init)
    carry = step(n_full, carry, True)
    for hh in range(heads_per_step):
        _, l, acc = carry[3 * hh:3 * hh + 3]
        o_ref[:, hh * hd:(hh + 1) * hd] = (acc / l).T.astype(o_ref.dtype)


def _fox_attn(q, k, v, aug, batch, seq, heads, head_dim, weights_to_round):
    tq = min(ATTN_Q_TILE, seq)
    tk = min(ATTN_K_TILE, seq)
    nq = seq // tq
    hps = ATTN_HEADS_PER_STEP if heads % ATTN_HEADS_PER_STEP == 0 else 1
    groups = heads // hps
    width = hps * head_dim
    rounding = _WeightRounding(weights_to_round, batch * groups * nq,
                               lambda b, h, i: (b * groups + h) * nq + i)
    kernel = functools.partial(_fox_attn_kernel, tq=tq, tk=tk, ts=min(ATTN_K_SUB, tk), hd=head_dim,
                               heads_per_step=hps, n_round=len(rounding))
    vmem = (7 * _nbytes((seq, width), BF16) + 6 * hps * _nbytes((tk, tq), F32) + rounding.vmem_bytes()
            + (8 << 20))
    outs = pl.pallas_call(
        kernel,
        out_shape=[jax.ShapeDtypeStruct(q.shape, BF16)] + rounding.out_shapes(),
        grid=(batch, groups, nq),
        in_specs=[pl.BlockSpec((tq, width), lambda b, h, i: (b * nq + i, h)),
                  pl.BlockSpec((seq, width), lambda b, h, i: (b, h)),
                  pl.BlockSpec((seq, width), lambda b, h, i: (b, h)),
                  pl.BlockSpec((seq, width), lambda b, h, i: (b, h))] + rounding.in_specs(),
        out_specs=[pl.BlockSpec((tq, width), lambda b, h, i: (b * nq + i, h))] + rounding.out_specs(),
        scratch_shapes=[pltpu.VMEM((width, seq), BF16)],
        compiler_params=_params(("arbitrary", "arbitrary", "arbitrary"), vmem),
        name="fox_prompt_attention",
    )(q, k, v, aug, *rounding.operands())
    return outs[0], outs[1:]


def _page_suffix_kernel(lf_ref, out_ref, *, heads):
    x = lf_ref[...]
    cols = x.shape[1]
    col = lax.broadcasted_iota(jnp.int32, x.shape, 1)
    incl = x
    tot = x
    sh = heads
    while sh < cols:
        incl = incl + jnp.where(col + sh < cols, pltpu.roll(incl, cols - sh, 1), 0.0)
        tot = tot + pltpu.roll(tot, sh, 1)
        sh *= 2
    out_ref[:, :cols] = incl - x
    out_ref[:, cols:] = tot


def _page_suffix(cache_lf, heads):
    n_rows, cols = cache_lf.shape
    tr = 256 if n_rows % 256 == 0 else n_rows
    return pl.pallas_call(
        functools.partial(_page_suffix_kernel, heads=heads),
        out_shape=jax.ShapeDtypeStruct((n_rows, 2 * cols), F32),
        grid=(n_rows // tr,),
        in_specs=[pl.BlockSpec((tr, cols), lambda r: (r, 0))],
        out_specs=pl.BlockSpec((tr, 2 * cols), lambda r: (r, 0)),
        compiler_params=_params(("arbitrary",), 12 * _nbytes((tr, cols), F32) + (4 << 20)),
        name="fox_page_logf_suffix",
    )(cache_lf)


def _fox_decode_kernel(pt_ref, q_ref, kn_ref, vn_ref, cn_ref, *refs, pages, page_id):
    k_refs = refs[:pages]
    v_refs = refs[pages:2 * pages]
    bias_refs = refs[2 * pages:3 * pages]
    o_ref, m_ref, l_ref, acc_ref, run_ref = refs[3 * pages:]
    s_idx = pl.program_id(1)
    heads = q_ref.shape[0]
    cols = k_refs[0].shape[0]
    head_mask = (lax.broadcasted_iota(jnp.int32, (heads, cols), 1) % heads
                 == lax.broadcasted_iota(jnp.int32, (heads, cols), 0))

    @pl.when(s_idx == 0)
    def _():
        m_ref[...] = jnp.full_like(m_ref, NEG_BIG)
        l_ref[...] = jnp.zeros_like(l_ref)
        acc_ref[...] = jnp.zeros_like(acc_ref)
        run_ref[...] = jnp.zeros_like(run_ref)

    q = q_ref[...]
    cn = cn_ref[...]
    run = run_ref[...]
    scores = []
    for g in range(pages):
        s = lax.dot_general(q, k_refs[g][...].astype(BF16), (((1,), (1,)), ((), ())),
                            preferred_element_type=F32)
        row = page_id(pt_ref, pl.program_id(0), s_idx, g) % F32_SUBLANES
        page_bias = bias_refs[g][pl.ds(row, 1), :]
        scores.append(jnp.where(head_mask, s + (page_bias[:, :cols] + (run + cn)), NEG_BIG))
        run = run + page_bias[:, cols:]
    run_ref[...] = run
    m = m_ref[...]
    m_new = m
    for s in scores:
        m_new = jnp.maximum(m_new, jnp.max(s, axis=-1, keepdims=True))
    alpha = jnp.exp(m - m_new)
    l = alpha * l_ref[...]
    acc = alpha * acc_ref[...]
    for g in range(pages):
        p = jnp.exp(scores[g] - m_new)
        l = l + jnp.sum(p, axis=-1, keepdims=True)
        acc = acc + jnp.dot(p.astype(BF16), v_refs[g][...].astype(BF16), preferred_element_type=F32)
    m_ref[...] = m_new
    l_ref[...] = l
    acc_ref[...] = acc

    @pl.when(s_idx == pl.num_programs(1) - 1)
    def _():
        s_new = jnp.sum(q.astype(F32) * kn_ref[...], axis=-1, keepdims=True)
        m = m_ref[...]
        m_new = jnp.maximum(m, s_new)
        alpha = jnp.exp(m - m_new)
        p_new = jnp.exp(s_new - m_new)
        l = alpha * l_ref[...] + p_new
        acc = alpha * acc_ref[...] + p_new.astype(BF16).astype(F32) * vn_ref[...]
        o_ref[...] = acc / l


def _fox_decode(page_table, pool_base, q, k_new, v_new, logf_new, cache_k, cache_v, cache_lf):
    bs, n_pages = page_table.shape
    _, heads, hd = q.shape
    n_pool, cols, _ = cache_k.shape
    assert n_pool % F32_SUBLANES == 0 and pool_base % F32_SUBLANES == 0
    page_bias = _page_suffix(cache_lf, heads)
    g_pages = PAGES_PER_STEP if n_pages % PAGES_PER_STEP == 0 else 1
    steps = n_pages // g_pages

    def page_id(pt, b, s, g):
        return pool_base + pt[b * n_pages + (n_pages - 1 - (s * g_pages + g))]

    kernel = functools.partial(_fox_decode_kernel, pages=g_pages, page_id=page_id)

    def page_idx(g):
        return lambda b, s, pt: (page_id(pt, b, s, g), 0, 0)

    def row_block_idx(g):
        return lambda b, s, pt: (page_id(pt, b, s, g) // F32_SUBLANES, 0)

    tok = pl.BlockSpec((None, heads, hd), lambda b, s, pt: (b, 0, 0))
    in_specs = [tok, tok, tok, pl.BlockSpec((None, 1, cols), lambda b, s, pt: (b, 0, 0))]
    in_specs += [pl.BlockSpec((None, cols, hd), page_idx(g)) for g in range(g_pages)]
    in_specs += [pl.BlockSpec((None, cols, hd), page_idx(g)) for g in range(g_pages)]
    in_specs += [pl.BlockSpec((F32_SUBLANES, 2 * cols), row_block_idx(g)) for g in range(g_pages)]
    vmem = (4 * g_pages * _nbytes((cols, hd), F32) + 2 * g_pages * _nbytes((cols, hd), BF16)
            + 4 * g_pages * _nbytes((heads, cols), F32) + (4 << 20))
    grid_spec = pltpu.PrefetchScalarGridSpec(
        num_scalar_prefetch=1,
        grid=(bs, steps),
        in_specs=in_specs,
        out_specs=pl.BlockSpec((None, heads, hd), lambda b, s, pt: (b, 0, 0)),
        scratch_shapes=[pltpu.VMEM((heads, 1), F32), pltpu.VMEM((heads, 1), F32),
                        pltpu.VMEM((heads, hd), F32), pltpu.VMEM((1, cols), F32)])
    return pl.pallas_call(
        kernel,
        out_shape=jax.ShapeDtypeStruct((bs, heads, hd), F32),
        grid_spec=grid_spec,
        compiler_params=_params(("arbitrary", "arbitrary"), vmem),
        name="fox_decode_attention",
    )(page_table.reshape(-1), q, k_new, v_new, logf_new,
      *([cache_k] * g_pages), *([cache_v] * g_pages), *([page_bias] * g_pages))


def _proj_residual_kernel(a_ref, w_ref, x_ref, gate_ref, o_ref):
    z = jnp.dot(a_ref[...], w_ref[...].astype(BF16), preferred_element_type=F32)
    o_ref[...] = x_ref[...] + gate_ref[...] * z


def _proj_residual(a, w, layer, x, mods, mod_rows, tiles_per_group, gate_chunk, tm):
    m_rows, k = a.shape
    d = w.shape[2]
    tn = min(COL_TILE, d)
    vmem = (2 * _nbytes((tm, k), BF16) + 2 * _nbytes((k, tn), F32) + _nbytes((k, tn), BF16)
            + 5 * _nbytes((tm, tn), F32) + (4 << 20))
    return pl.pallas_call(
        _proj_residual_kernel,
        out_shape=jax.ShapeDtypeStruct((m_rows, d), F32),
        grid=(m_rows // tm, d // tn),
        in_specs=[pl.BlockSpec((tm, k), lambda m, n: (m, 0)),
                  pl.BlockSpec((None, k, tn), lambda m, n: (layer, 0, n)),
                  pl.BlockSpec((tm, tn), lambda m, n: (m, n)),
                  pl.BlockSpec((None, mod_rows, tn),
                               lambda m, n: (m // tiles_per_group, 0, gate_chunk * (d // tn) + n))],
        out_specs=pl.BlockSpec((tm, tn), lambda m, n: (m, n)),
        compiler_params=_params(("arbitrary", "arbitrary"), vmem),
        name="mixer_out_proj",
    )(a, w, x, mods)


def _mlp_kernel(x_ref, xs_ref, g_ref, mod_ref, mods_ref, wu_ref, wd_ref, fg_ref, o_ref, os_ref, h_ref, *,
                tm, final_norm):
    f = pl.program_id(1)
    d = x_ref.shape[1]

    @pl.when(f == 0)
    def _():
        h_ref[:tm, :] = _norm_mod(x_ref[...], g_ref[...], mod_ref[:, :d], mod_ref[:, d:2 * d]).astype(BF16)
        h_ref[tm:, :] = _norm_mod(xs_ref[...], g_ref[...], mods_ref[:, :d],
                                  mods_ref[:, d:2 * d]).astype(BF16)
        o_ref[...] = jnp.zeros_like(o_ref)
        os_ref[...] = jnp.zeros_like(os_ref)

    u = jnp.dot(h_ref[...], wu_ref[...].astype(BF16), preferred_element_type=F32)
    u = jnp.square(jnp.maximum(u, 0.0)).astype(BF16)
    z = jnp.dot(u, wd_ref[...].astype(BF16), preferred_element_type=F32)
    o_ref[...] += z[:tm]
    os_ref[...] += z[tm:]

    @pl.when(f == pl.num_programs(1) - 1)
    def _():
        for x_r, mod_r, o_r in ((x_ref, mod_ref, o_ref), (xs_ref, mods_ref, os_ref)):
            y = x_r[...] + mod_r[:, 2 * d:] * o_r[...]
            if final_norm:
                y = y * lax.rsqrt(jnp.mean(y * y, axis=-1, keepdims=True) + EPS) * fg_ref[...]
            o_r[...] = y


def _mlp(x, xs, g, mods_p, mods_s, tiles_per_batch, w_up, w_down, layer, final_g, final_norm, tm):
    m_rows, d = x.shape
    srows = xs.shape[0]
    d_ff = w_up.shape[2]
    tf = min(COL_TILE, d_ff)
    kernel = functools.partial(_mlp_kernel, tm=tm, final_norm=final_norm)
    x_buffers = 1 if w_up.dtype == F32 else 2
    x_mode = {"pipeline_mode": pl.Buffered(1)} if x_buffers == 1 else {}
    vmem = ((2 + x_buffers) * _nbytes((tm, d), F32) + _nbytes((tm + srows, d), BF16)
            + 4 * _nbytes((d, tf), w_up.dtype) + 2 * _nbytes((d, tf), BF16) + 4 * _nbytes((tm, tf), F32)
            + (4 << 20))
    full = lambda m, f: (0, 0)
    vec = pl.BlockSpec((1, d), full)
    return pl.pallas_call(
        kernel,
        out_shape=(jax.ShapeDtypeStruct((m_rows, d), F32), jax.ShapeDtypeStruct((srows, d), F32)),
        grid=(m_rows // tm, d_ff // tf),
        in_specs=[pl.BlockSpec((tm, d), lambda m, f: (m, 0), **x_mode),
                  pl.BlockSpec((srows, d), full),
                  vec,
                  _mod_spec(1, d, 3, tiles_per_batch, 3),
                  _mod_spec(srows, d, 3, m_rows, 3),
                  pl.BlockSpec((None, d, tf), lambda m, f: (layer, 0, f)),
                  pl.BlockSpec((None, tf, d), lambda m, f: (layer, f, 0)),
                  vec],
        out_specs=(pl.BlockSpec((tm, d), lambda m, f: (m, 0)), pl.BlockSpec((srows, d), full)),
        scratch_shapes=[pltpu.VMEM((tm + srows, d), BF16)],
        compiler_params=_params(("arbitrary", "arbitrary"), vmem),
        name="relu2_mlp",
    )(x, xs, g, mods_p, mods_s, w_up, w_down, final_g)


def _ret_proj_kernel(x_ref, xs_ref, g_ref, mod_ref, mods_ref, w_ref, cs_ref, css_ref,
                     q_ref, k_ref, v_ref, gt_ref, zs_ref, h_ref, *, nq, tm, key_dim, k_scale):
    n = pl.program_id(1)
    d = x_ref.shape[1]

    @pl.when(n == 0)
    def _():
        h_ref[:tm, :] = _norm_mod(x_ref[...], g_ref[...], mod_ref[:, :d], mod_ref[:, d:]).astype(BF16)
        h_ref[tm:, :] = _norm_mod(xs_ref[...], g_ref[...], mods_ref[:, :d], mods_ref[:, d:]).astype(BF16)

    z = jnp.dot(h_ref[...], w_ref[...].astype(BF16), preferred_element_type=F32)
    zp = z[:tm]
    zs = z[tm:]
    half = key_dim // 2

    def rotate(zz, table_ref):
        cos = table_ref[:, :half]
        sin = table_ref[:, half:]
        parts = []
        for hh in range(zz.shape[1] // key_dim):
            x1 = zz[:, hh * key_dim:hh * key_dim + half]
            x2 = zz[:, hh * key_dim + half:(hh + 1) * key_dim]
            parts += [x1 * cos - x2 * sin, x1 * sin + x2 * cos]
        return jnp.concatenate(parts, axis=1)

    last_tile = pl.program_id(0) == pl.num_programs(0) - 1
    in_q = n < nq
    in_k = jnp.logical_and(n >= nq, n < 2 * nq)

    @pl.when(in_q)
    def _():
        q_ref[...] = rotate(zp, cs_ref).astype(BF16)

    @pl.when(in_k)
    def _():
        k_ref[...] = rotate(zp, cs_ref) * k_scale

    @pl.when(jnp.logical_and(n >= 2 * nq, n < 4 * nq))
    def _():
        v_ref[...] = zp.astype(BF16)

    @pl.when(n >= 4 * nq)
    def _():
        gt_ref[...] = zp

    @pl.when(jnp.logical_and(last_tile, n < 2 * nq))
    def _():
        zs_ref[...] = rotate(zs, css_ref) * jnp.where(in_k, k_scale, 1.0)

    @pl.when(jnp.logical_and(last_tile, n >= 2 * nq))
    def _():
        zs_ref[...] = zs


def _ret_proj(x, xs, g, mods_p, mods_s, tiles_per_batch, w_in, layer, rot, rot_s, key_dim, tm):
    m_rows, d = x.shape
    srows = xs.shape[0]
    tn = min(COL_TILE, d)
    nq = d // tn
    kernel = functools.partial(_ret_proj_kernel, nq=nq, tm=tm, key_dim=key_dim, k_scale=key_dim ** -0.5)
    last_tile = m_rows // tm - 1

    def col(lo, width):
        return lambda m, n: (m, jnp.clip(n - lo, 0, width - 1))

    vmem = (2 * _nbytes((tm, d), F32) + _nbytes((tm + srows, d), BF16) + 2 * _nbytes((d, tn), F32)
            + _nbytes((d, tn), BF16) + 2 * (2 * _nbytes((tm, tn), BF16) + 2 * _nbytes((tm, tn), F32))
            + 2 * _nbytes((tm, key_dim), F32) + 4 * _nbytes((tm, tn), F32) + (4 << 20))
    full = lambda m, n: (0, 0)
    outs = pl.pallas_call(
        kernel,
        out_shape=(jax.ShapeDtypeStruct((m_rows, d), BF16), jax.ShapeDtypeStruct((m_rows, d), F32),
                   jax.ShapeDtypeStruct((m_rows, 2 * d), BF16), jax.ShapeDtypeStruct((m_rows, 2 * d), F32),
                   jax.ShapeDtypeStruct((srows, 6 * d), F32)),
        grid=(m_rows // tm, 6 * nq),
        in_specs=[pl.BlockSpec((tm, d), lambda m, n: (m, 0)),
                  pl.BlockSpec((srows, d), full),
                  pl.BlockSpec((1, d), full),
                  _mod_spec(1, d, 0, tiles_per_batch, 2),
                  _mod_spec(srows, d, 0, m_rows, 2),
                  pl.BlockSpec((None, d, tn), lambda m, n: (layer, 0, n)),
                  pl.BlockSpec((tm, key_dim), lambda m, n: (m % tiles_per_batch, 0)),
                  pl.BlockSpec((srows, key_dim), full)],
        out_specs=(pl.BlockSpec((tm, tn), col(0, nq)),
                   pl.BlockSpec((tm, tn), col(nq, nq)),
                   pl.BlockSpec((tm, tn), col(2 * nq, 2 * nq)),
                   pl.BlockSpec((tm, tn), col(4 * nq, 2 * nq)),
                   pl.BlockSpec((srows, tn), lambda m, n: (0, jnp.where(m == last_tile, n, 0)))),
        scratch_shapes=[pltpu.VMEM((tm + srows, d), BF16)],
        compiler_params=_params(("arbitrary", "arbitrary"), vmem),
        name="ret_in_proj",
    )(x, xs, g, mods_p, mods_s, w_in, rot, rot_s)
    return outs[:4], outs[4]


def _group_norm_gate(o, gate):
    mu = jnp.mean(o, axis=-1, keepdims=True)
    cen = o - mu
    var = jnp.mean(cen * cen, axis=-1, keepdims=True)
    return _silu(gate) * (cen * lax.rsqrt(var + EPS))


def _ret_chunk_kernel(lg_ref, q_ref, k_ref, v_ref, gt_ref, *rest, n_round, heads_per_step):
    y_ref, s_ref = rest[n_round:n_round + 2]
    dec_ref, xi_ref, zeta_ref = rest[-3:]
    _WeightRounding.run(rest[:n_round], rest[n_round + 2:-3])
    ln = q_ref.shape[0]
    kd = s_ref.shape[1]
    vd = s_ref.shape[2]
    lgs = [jnp.full((1, 1), lg_ref[pl.program_id(1) * heads_per_step + hh], F32)
           for hh in range(heads_per_step)]

    @pl.when(pl.program_id(2) == 0)
    def _():
        s_ref[...] = jnp.zeros_like(s_ref)
        diff = (lax.broadcasted_iota(jnp.int32, (ln, ln), 0)
                - lax.broadcasted_iota(jnp.int32, (ln, ln), 1)).astype(F32)
        n = lax.broadcasted_iota(jnp.int32, (ln, 1), 0).astype(F32)
        for hh, lg in enumerate(lgs):
            dec_ref[hh] = jnp.where(diff >= 0, jnp.exp(lg * jnp.maximum(diff, 0.0)), 0.0)
            xi_ref[hh] = jnp.exp(lg * (n + 1.0))
            zeta_ref[hh] = jnp.exp(lg * (ln - 1.0 - n))

    for hh, lg in enumerate(lgs):
        q = q_ref[:, hh * kd:(hh + 1) * kd]
        k = k_ref[:, hh * kd:(hh + 1) * kd]
        v = v_ref[:, hh * vd:(hh + 1) * vd]
        s0 = s_ref[hh]
        qk = lax.dot_general(q, k.astype(BF16), (((1,), (1,)), ((), ())), preferred_element_type=F32)
        a = (qk * dec_ref[hh]).astype(BF16)
        o = (jnp.dot(a, v, preferred_element_type=F32)
             + jnp.dot(q, s0.astype(BF16), preferred_element_type=F32) * xi_ref[hh])
        kz = (k * zeta_ref[hh]).astype(BF16)
        s_ref[hh] = jnp.exp(lg * ln) * s0 + lax.dot_general(
            kz, v, (((0,), (0,)), ((), ())), preferred_element_type=F32)
        y_ref[:, hh * vd:(hh + 1) * vd] = _group_norm_gate(
            o, gt_ref[:, hh * vd:(hh + 1) * vd]).astype(y_ref.dtype)


def _ret_prompt(lg, q, k, v, gate, batch, seq, heads, key_dim, val_dim, weights_to_round):
    ln = min(RET_CHUNK, seq)
    nc = seq // ln
    hps = RET_HEADS_PER_STEP if heads % RET_HEADS_PER_STEP == 0 else 1
    groups = heads // hps
    rounding = _WeightRounding(weights_to_round, batch * groups * nc,
                               lambda b, h, c, *_: (b * groups + h) * nc + c)
    vmem = (hps * (2 * (_nbytes((ln, key_dim), BF16) + _nbytes((ln, key_dim), F32)
                        + _nbytes((ln, val_dim), BF16) + _nbytes((ln, val_dim), F32)
                        + _nbytes((ln, val_dim), BF16))
                   + 4 * _nbytes((key_dim, val_dim), F32) + 8 * _nbytes((ln, val_dim), F32)
                   + _nbytes((ln, ln), F32))
            + rounding.vmem_bytes() + (8 << 20))
    grid_spec = pltpu.PrefetchScalarGridSpec(
        num_scalar_prefetch=1,
        grid=(batch, groups, nc),
        in_specs=[pl.BlockSpec((ln, hps * key_dim), lambda b, h, c, lg: (b * nc + c, h)),
                  pl.BlockSpec((ln, hps * key_dim), lambda b, h, c, lg: (b * nc + c, h)),
                  pl.BlockSpec((ln, hps * val_dim), lambda b, h, c, lg: (b * nc + c, h)),
                  pl.BlockSpec((ln, hps * val_dim), lambda b, h, c, lg: (b * nc + c, h))]
        + rounding.in_specs(),
        out_specs=[pl.BlockSpec((ln, hps * val_dim), lambda b, h, c, lg: (b * nc + c, h)),
                   pl.BlockSpec((None, hps, key_dim, val_dim), lambda b, h, c, lg: (b, h, 0, 0))]
        + rounding.out_specs(),
        scratch_shapes=[pltpu.VMEM((hps, ln, ln), F32), pltpu.VMEM((hps, ln, 1), F32),
                        pltpu.VMEM((hps, ln, 1), F32)])
    outs = pl.pallas_call(
        functools.partial(_ret_chunk_kernel, n_round=len(rounding), heads_per_step=hps),
        out_shape=[jax.ShapeDtypeStruct((batch * seq, heads * val_dim), BF16),
                   jax.ShapeDtypeStruct((batch, heads, key_dim, val_dim), F32)] + rounding.out_shapes(),
        grid_spec=grid_spec,
        compiler_params=_params(("arbitrary", "arbitrary", "arbitrary"), vmem),
        name="ret_prompt_chunks",
    )(lg, q, k, v, gate, *rounding.operands())
    return outs[0], outs[1], outs[2:]


def _ret_decode_kernel(lg_ref, q_ref, k_ref, v_ref, gt_ref, s0_ref, y_ref, s_ref):
    q = q_ref[...]
    k = k_ref[...]
    q_cols = q.T
    k_cols = k.T
    a_all = jnp.sum(q * k, axis=-1, keepdims=True)
    for h in range(s0_ref.shape[0]):
        gamma = jnp.exp(jnp.full((1, 1), lg_ref[h], F32))
        s0 = s0_ref[h]
        v = v_ref[h:h + 1, :]
        qs = jnp.sum(q_cols[:, h:h + 1] * s0, axis=0, keepdims=True)
        o = a_all[h:h + 1, :] * v + qs * gamma
        s_ref[h] = gamma * s0 + k_cols[:, h:h + 1] * v
        y_ref[h:h + 1, :] = _group_norm_gate(o, gt_ref[h:h + 1, :])


def _ret_decode(lg, q, k, v, gate, state):
    bs, heads, key_dim, val_dim = state.shape

    def vec(width):
        return pl.BlockSpec((None, heads, width), lambda b, lg: (b, 0, 0))

    state_spec = pl.BlockSpec((None, heads, key_dim, val_dim), lambda b, lg: (b, 0, 0, 0))
    grid_spec = pltpu.PrefetchScalarGridSpec(
        num_scalar_prefetch=1,
        grid=(bs,),
        in_specs=[vec(key_dim), vec(key_dim), vec(val_dim), vec(val_dim), state_spec],
        out_specs=(vec(val_dim), state_spec))
    y, s = pl.pallas_call(
        _ret_decode_kernel,
        out_shape=(jax.ShapeDtypeStruct((bs, heads, val_dim), F32),
                   jax.ShapeDtypeStruct(state.shape, F32)),
        grid_spec=grid_spec,
        compiler_params=_params(("arbitrary",), 4 * _nbytes(state.shape[1:], F32) + (16 << 20)),
        name="ret_decode_step",
    )(lg, q.reshape(bs, heads, key_dim), k.reshape(bs, heads, key_dim),
      v.reshape(bs, heads, val_dim), gate.reshape(bs, heads, val_dim), state)
    return y.reshape(bs, heads * val_dim), s


def _rotary_table(pos, key_dim):
    half = key_dim // 2
    inv = (np.float32(1.0) / np.float32(ROPE_BASE) ** np.linspace(0.0, 1.0, half, dtype=np.float32))
    ang = (np.asarray(pos, np.float32)[:, None] * inv.astype(np.float32)[None, :]).astype(np.float64)
    return jnp.asarray(np.concatenate([np.cos(ang), np.sin(ang)], axis=1), F32)


def _pad_rows(a, rows):
    return jnp.pad(a, ((0, rows - a.shape[0]), (0, 0)))


def kernel(x_prompt, x_sample, c_prompt, c_sample, cache_fox_k, cache_fox_v, cache_fox_logf, state_ret, page_table, norm1_g, norm2_g, ada_w, ada_b, fox_w_in, fox_b_f, fox_w_out, ret_w_in, ret_w_out, mlp_w_up, mlp_w_down, final_g):
    bp, tp, d = x_prompt.shape
    bs, ts, _ = x_sample.shape
    assert ts == 1, "the decode kernels handle one new token per sequence"
    depth = ada_w.shape[0]
    _, n_pool, page, fox_heads, fox_hd = cache_fox_k.shape
    _, _, ret_heads, key_dim, val_dim = state_ret.shape
    past_len = page_table.shape[1] * page
    srows = -(-bs // BF16_SUBLANES) * BF16_SUBLANES
    tm = min(ROW_TILE, tp)
    tiles_p = tp // tm

    mod_rows = -(-(bs + bp) // 8) * 8
    c_all = _pad_rows(jnp.concatenate([c_sample, c_prompt], axis=0), max(mod_rows, srows))
    mods = _ada(c_all, ada_w, ada_b)

    lg = jnp.log1p(-jnp.exp2(-5.0 - jnp.arange(ret_heads, dtype=F32)))
    rot_p = _rotary_table(np.arange(tp), key_dim)
    rot_s = _rotary_table(np.full((srows,), past_len), key_dim)

    yp = x_prompt.reshape(bp * tp, d)
    ys = _pad_rows(x_sample.reshape(bs, d), srows)
    fg = final_g.reshape(1, d)
    outs = {name: [] for name in ("kp", "vp", "lfp", "ks", "vs", "lfs", "sp", "ss")}

    ret_rounded = None
    for i in range(depth):
        mods_p = mods[i, bs:bs + bp].reshape(bp, 1, 6 * d)
        mods_s = mods[i, :srows].reshape(1, srows, 6 * d)
        g1 = norm1_g[i].reshape(1, d)
        g2 = norm2_g[i].reshape(1, d)
        j = i // 2
        if i % 2 == 0:
            b_f = fox_b_f[j].reshape(1, fox_heads)
            (q, kf, vf, kb, vb, lf), sample = _fox_proj(
                yp, ys, g1, mods_p, mods_s, tiles_p, fox_w_in, j, b_f, tm,
                fox_hd ** -0.5 * LOG2E, fox_hd ** -0.5)
            aug = _cumsum(lf, bp, tp, fox_hd)
            to_round = [(fox_w_out, j), (mlp_w_up, i), (mlp_w_down, i)]
            if i + 1 < depth:
                jr = (i + 1) // 2
                to_round += [(ret_w_in, jr), (ret_w_out, jr), (mlp_w_up, i + 1), (mlp_w_down, i + 1)]
            o, rounded = _fox_attn(q, kb, vb, aug, bp, tp, fox_heads, fox_hd, to_round)
            w_out, w_up, w_down = rounded[:3]
            l_mlp = 0
            ret_rounded = rounded[3:] if i + 1 < depth else None
            yp = _proj_residual(o, w_out, 0, yp, mods_p, 1, tiles_p, 2, tm)
            outs["kp"].append(kf.reshape(bp, tp, fox_heads, fox_hd))
            outs["vp"].append(vf.reshape(bp, tp, fox_heads, fox_hd))
            outs["lfp"].append(lf.reshape(bp, tp, fox_heads))
            zs, lf = sample
            kf, vf = zs[:, d:2 * d], zs[:, 2 * d:]
            bf16_rounded = lambda a: a.astype(BF16).astype(F32)
            o = _fox_decode(
                page_table, j * n_pool,
                zs[:bs, :d].astype(BF16).reshape(bs, fox_heads, fox_hd),
                bf16_rounded(kf[:bs]).reshape(bs, fox_heads, fox_hd),
                bf16_rounded(vf[:bs]).reshape(bs, fox_heads, fox_hd),
                jnp.tile(lf[:bs], (1, page)).reshape(bs, 1, page * fox_heads),
                cache_fox_k.reshape(-1, page * fox_heads, fox_hd),
                cache_fox_v.reshape(-1, page * fox_heads, fox_hd),
                cache_fox_logf.reshape(-1, page * fox_heads))
            o = _pad_rows(o.reshape(bs, d), srows).astype(BF16)
            ys = _proj_residual(o, w_out, 0, ys, mods_s, srows, 1, 2, srows)
            outs["ks"].append(kf[:bs].reshape(bs, ts, fox_heads, fox_hd))
            outs["vs"].append(vf[:bs].reshape(bs, ts, fox_heads, fox_hd))
            outs["lfs"].append(lf[:bs].reshape(bs, ts, fox_heads))
        else:
            if ret_rounded is None:
                w_in, w_out, w_up, w_down = ret_w_in, ret_w_out, mlp_w_up, mlp_w_down
                l_mix, l_mlp = j, i
            else:
                w_in, w_out, w_up, w_down = ret_rounded
                l_mix = l_mlp = 0
            (q, k, v, gate), zs = _ret_proj(yp, ys, g1, mods_p, mods_s, tiles_p, w_in, l_mix,
                                            rot_p, rot_s, key_dim, tm)
            y, s, _ = _ret_prompt(lg, q, k, v, gate, bp, tp, ret_heads, key_dim, val_dim, [])
            yp = _proj_residual(y, w_out, l_mix, yp, mods_p, 1, tiles_p, 2, tm)
            outs["sp"].append(s)
            zs = zs[:bs]
            bf16_rounded = lambda a: a.astype(BF16).astype(F32)
            y, s = _ret_decode(lg, bf16_rounded(zs[:, :d]), zs[:, d:2 * d], bf16_rounded(zs[:, 2 * d:4 * d]),
                               zs[:, 4 * d:], state_ret[j].astype(F32))
            ys = _proj_residual(_pad_rows(y, srows).astype(BF16), w_out, l_mix, ys, mods_s, srows, 1, 2, srows)
            outs["ss"].append(s)
        yp, ys = _mlp(yp, ys, g2, mods_p, mods_s, tiles_p, w_up, w_down, l_mlp, fg, i == depth - 1, tm)

    return (yp.reshape(bp, tp, d), ys[:bs].reshape(bs, ts, d),
            jnp.stack(outs["kp"]), jnp.stack(outs["vp"]), jnp.stack(outs["lfp"]),
            jnp.stack(outs["ks"]), jnp.stack(outs["vs"]), jnp.stack(outs["lfs"]),
            jnp.stack(outs["sp"]), jnp.stack(outs["ss"]))
```

```python
import functools

import jax
import jax.numpy as jnp
import numpy as np
from jax import lax
from jax.experimental import pallas as pl
from jax.experimental.pallas import tpu as pltpu

F32 = jnp.float32
BF16 = jnp.bfloat16

EPS = 1e-6
ROPE_BASE = 10000.0
LOG2E = 1.4426950408889634
NEG_BIG = -1e30

V7X_VMEM_LIMIT_BYTES = 60000 * 1024
F32_SUBLANES = 8
BF16_SUBLANES = 16

ROW_TILE = 1024
COL_TILE = 512
ATTN_Q_TILE = 1024
ATTN_K_TILE = 1024
ATTN_K_SUB = 512
ATTN_HEADS_PER_STEP = 2
CUMSUM_TILE = 512
RET_HEADS_PER_STEP = 4
RET_CHUNK = 512
PAGES_PER_STEP = 8


def _params(semantics, vmem_bytes):
    return pltpu.CompilerParams(dimension_semantics=semantics,
                                vmem_limit_bytes=int(min(vmem_bytes, V7X_VMEM_LIMIT_BYTES)))


def _nbytes(shape, dtype):
    n = 1
    for s in shape:
        n *= s
    return n * jnp.dtype(dtype).itemsize


def _norm_mod(x, g, shift, scale):
    y = x * lax.rsqrt(jnp.mean(x * x, axis=-1, keepdims=True) + EPS)
    return y * (g * (1.0 + scale)) + shift


def _silu(x):
    return x * jax.nn.sigmoid(x)


def _log_sigmoid(x):
    return jnp.minimum(x, 0.0) - jnp.log1p(jnp.exp(-jnp.abs(x)))


def _split3(x):
    hi = x.astype(BF16)
    r1 = x - hi.astype(F32)
    mid = r1.astype(BF16)
    lo = (r1 - mid.astype(F32)).astype(BF16)
    return hi, mid, lo


class _WeightRounding:
    def __init__(self, weights, n_steps, step_index):
        self.items = []
        for w, layer in weights:
            slab = w.shape[1] // n_steps
            assert w.shape[1] % n_steps == 0 and slab % BF16_SUBLANES == 0, (w.shape, n_steps)
            self.items.append((w, layer, slab))
        self.step_index = step_index

    def __len__(self):
        return len(self.items)

    def in_specs(self):
        return [pl.BlockSpec((None, slab, w.shape[2]),
                             lambda *ids, layer=layer: (layer, self.step_index(*ids), 0))
                for w, layer, slab in self.items]

    def out_specs(self):
        return [pl.BlockSpec((None, slab, w.shape[2]), lambda *ids: (0, self.step_index(*ids), 0))
                for w, _, slab in self.items]

    def out_shapes(self):
        return [jax.ShapeDtypeStruct((1,) + w.shape[1:], BF16) for w, _, _ in self.items]

    def operands(self):
        return [w for w, _, _ in self.items]

    def vmem_bytes(self):
        return sum(2 * _nbytes((slab, w.shape[2]), F32) + 2 * _nbytes((slab, w.shape[2]), BF16)
                   for w, _, slab in self.items)

    @staticmethod
    def run(in_refs, out_refs):
        for i_ref, o_ref in zip(in_refs, out_refs):
            o_ref[...] = i_ref[...].astype(BF16)


def _mod_spec(rows, width, chunk, tiles_per_group, n_chunks=1):
    assert chunk % n_chunks == 0
    return pl.BlockSpec((None, rows, n_chunks * width),
                        lambda m, n: (m // tiles_per_group, 0, chunk // n_chunks))


def _ada_kernel(c_ref, w_ref, b_ref, o_ref):
    o_ref[...] = jnp.dot(_silu(c_ref[...]), w_ref[...], preferred_element_type=F32) + b_ref[...]


def _ada(c_all, ada_w, ada_b):
    depth, d, n6 = ada_w.shape
    rows = c_all.shape[0]
    tn = min(1024, n6)
    vmem = 2 * (_nbytes((d, tn), F32) + _nbytes((rows, tn), F32)) + _nbytes((rows, d), F32) * 2 + (4 << 20)
    return pl.pallas_call(
        _ada_kernel,
        out_shape=jax.ShapeDtypeStruct((depth, rows, n6), F32),
        grid=(depth, n6 // tn),
        in_specs=[pl.BlockSpec((rows, d), lambda i, n: (0, 0)),
                  pl.BlockSpec((None, d, tn), lambda i, n: (i, 0, n)),
                  pl.BlockSpec((None, 1, tn), lambda i, n: (i, 0, n))],
        out_specs=pl.BlockSpec((None, rows, tn), lambda i, n: (i, 0, n)),
        compiler_params=_params(("arbitrary", "arbitrary"), vmem),
        name="ada_modulation",
    )(c_all, ada_w, ada_b.reshape(depth, 1, n6))


def _fox_proj_kernel(x_ref, xs_ref, g_ref, mod_ref, mods_ref, w_ref, wf_ref, bf_ref,
                     q_ref, kf_ref, vf_ref, kb_ref, vb_ref, lf_ref, zs_ref, lfs_ref, h_ref, *,
                     nq, tm, q_scale, q_scale_s):
    n = pl.program_id(1)
    d = x_ref.shape[1]

    @pl.when(n == 0)
    def _():
        w_f = wf_ref[...].astype(BF16)
        for x_r, mod_r, lf_r, rows in ((x_ref, mod_ref, lf_ref, slice(0, tm)),
                                       (xs_ref, mods_ref, lfs_ref, slice(tm, None))):
            h = _norm_mod(x_r[...], g_ref[...], mod_r[:, :d], mod_r[:, d:]).astype(BF16)
            h_ref[rows, :] = h
            fl = lax.dot_general(h, w_f, (((1,), (1,)), ((), ())), preferred_element_type=F32)
            lf_r[...] = _log_sigmoid(fl + bf_ref[...])

    z = lax.dot_general(h_ref[...], w_ref[...].astype(BF16), (((1,), (1,)), ((), ())),
                        preferred_element_type=F32)
    zp = z[:tm]
    zs = z[tm:]
    in_q = n < nq

    @pl.when(in_q)
    def _():
        q_ref[...] = (zp * q_scale).astype(BF16)

    @pl.when(jnp.logical_and(n >= nq, n < 2 * nq))
    def _():
        kf_ref[...] = zp
        kb_ref[...] = zp.astype(BF16)

    @pl.when(n >= 2 * nq)
    def _():
        vf_ref[...] = zp
        vb_ref[...] = zp.astype(BF16)

    @pl.when(pl.program_id(0) == pl.num_programs(0) - 1)
    def _():
        zs_ref[...] = zs * jnp.where(in_q, q_scale_s, 1.0)


def _fox_proj(x, xs, g, mods_p, mods_s, tiles_per_batch, w_in, layer, b_f, tm, q_scale, q_scale_s):
    m_rows, d = x.shape
    srows = xs.shape[0]
    heads = b_f.shape[1]
    assert w_in.shape[2] - 3 * d == heads and (3 * d) % heads == 0
    w_t = jnp.swapaxes(w_in, 1, 2)
    tn = min(COL_TILE, d)
    nq = d // tn
    kernel = functools.partial(_fox_proj_kernel, nq=nq, tm=tm, q_scale=q_scale, q_scale_s=q_scale_s)

    last_tile = m_rows // tm - 1

    def col(lo):
        return lambda m, n: (m, jnp.clip(n - lo, 0, nq - 1))

    vmem = (2 * _nbytes((tm, d), F32) + _nbytes((tm + srows, d), BF16) + 2 * _nbytes((d, tn), F32)
            + _nbytes((d, tn), BF16) + 2 * (3 * _nbytes((tm, tn), BF16) + 2 * _nbytes((tm, tn), F32))
            + 3 * _nbytes((tm, tn), F32) + (4 << 20))
    out_f = jax.ShapeDtypeStruct((m_rows, d), F32)
    out_b = jax.ShapeDtypeStruct((m_rows, d), BF16)
    full = lambda m, n: (0, 0)
    outs = pl.pallas_call(
        kernel,
        out_shape=(out_b, out_f, out_f, out_b, out_b, jax.ShapeDtypeStruct((m_rows, heads), F32),
                   jax.ShapeDtypeStruct((srows, 3 * d), F32), jax.ShapeDtypeStruct((srows, heads), F32)),
        grid=(m_rows // tm, 3 * nq),
        in_specs=[pl.BlockSpec((tm, d), lambda m, n: (m, 0)),
                  pl.BlockSpec((srows, d), full),
                  pl.BlockSpec((1, d), full),
                  _mod_spec(1, d, 0, tiles_per_batch, 2),
                  _mod_spec(srows, d, 0, m_rows, 2),
                  pl.BlockSpec((None, tn, d), lambda m, n: (layer, n, 0)),
                  pl.BlockSpec((None, heads, d), lambda m, n: (layer, 3 * d // heads, 0)),
                  pl.BlockSpec((1, heads), full)],
        out_specs=(pl.BlockSpec((tm, tn), col(0)),
                   pl.BlockSpec((tm, tn), col(nq)),
                   pl.BlockSpec((tm, tn), col(2 * nq)),
                   pl.BlockSpec((tm, tn), col(nq)),
                   pl.BlockSpec((tm, tn), col(2 * nq)),
                   pl.BlockSpec((tm, heads), lambda m, n: (m, 0)),
                   pl.BlockSpec((srows, tn), lambda m, n: (0, jnp.where(m == last_tile, n, 0))),
                   pl.BlockSpec((srows, heads), full)),
        scratch_shapes=[pltpu.VMEM((tm + srows, d), BF16)],
        compiler_params=_params(("arbitrary", "arbitrary"), vmem),
        name="fox_in_proj",
    )(x, xs, g, mods_p, mods_s, w_t, w_t, b_f)
    return outs[:6], outs[6:]


def _cumsum_kernel(lf_ref, aug_ref, carry_ref, *, hd):
    @pl.when(pl.program_id(1) == 0)
    def _():
        carry_ref[...] = jnp.zeros_like(carry_ref)

    tc, heads = lf_ref.shape
    row = lax.broadcasted_iota(jnp.int32, (tc, tc), 0)
    col = lax.broadcasted_iota(jnp.int32, (tc, tc), 1)
    tri = jnp.where(col <= row, 1.0, 0.0).astype(BF16)
    c = carry_ref[...]
    for piece in _split3(lf_ref[...]):
        c = c + jnp.dot(tri, piece, preferred_element_type=F32)
    carry_ref[...] = c[tc - 1:tc, :]

    row = lax.broadcasted_iota(jnp.int32, (3 * heads, heads * hd), 0)
    lane = lax.broadcasted_iota(jnp.int32, (3 * heads, heads * hd), 1)
    place = jnp.where(lane == (row % heads) * hd + row // heads, 1.0, 0.0).astype(BF16)
    pieces = jnp.concatenate(_split3(c * (-LOG2E)), axis=1)
    aug_ref[...] = jnp.dot(pieces, place, preferred_element_type=F32).astype(BF16)


def _cumsum(logf, batch, seq, hd):
    rows, heads = logf.shape
    tc = min(CUMSUM_TILE, seq)
    per = seq // tc
    return pl.pallas_call(
        functools.partial(_cumsum_kernel, hd=hd),
        out_shape=jax.ShapeDtypeStruct((rows, heads * hd), BF16),
        grid=(batch, per),
        in_specs=[pl.BlockSpec((tc, heads), lambda b, t: (b * per + t, 0))],
        out_specs=pl.BlockSpec((tc, heads * hd), lambda b, t: (b * per + t, 0)),
        scratch_shapes=[pltpu.VMEM((1, heads), F32)],
        compiler_params=_params(("arbitrary", "arbitrary"), 24 << 20),
        name="fox_logf_cumsum",
    )(logf)


def _fox_attn_kernel(q_ref, k_ref, v_ref, aug_ref, *rest, tq, tk, ts, hd, heads_per_step, n_round):
    o_ref, vt_ref = rest[n_round], rest[-1]
    _WeightRounding.run(rest[:n_round], rest[n_round + 1:-1])
    i = pl.program_id(2)
    seq = k_ref.shape[0]
    q0 = pl.multiple_of(i * tq, tq)
    lane_tile = 128

    @pl.when(i == 0)
    def _():
        for t in range(seq // tk):
            vt_ref[:, t * tk:(t + 1) * tk] = v_ref[t * tk:(t + 1) * tk, :].astype(F32).T.astype(BF16)

    ones = jnp.where(lax.broadcasted_iota(jnp.int32, (tq, hd), 1) < 3, 1.0, 0.0).astype(BF16)
    q_ext = [jnp.concatenate([q_ref[:, hh * hd:(hh + 1) * hd], ones], axis=1)
             for hh in range(heads_per_step)]

    def step(j, carry, masked):
        carry = list(carry)
        items = [(sub, hh) for sub in range(tk // ts) for hh in range(heads_per_step)]
        n_tiles = tq // lane_tile

        def tile_range(sub):
            if masked and tq == tk:
                return sub * ts // lane_tile, min((sub + 1) * ts // lane_tile, n_tiles)
            return 0, (n_tiles if masked else 0)

        def scores(sub, hh):
            k0 = pl.multiple_of(j * tk + sub * ts, ts)
            lanes = slice(hh * hd, (hh + 1) * hd)
            k_ext = jnp.concatenate([k_ref[pl.ds(k0, ts), lanes], aug_ref[pl.ds(k0, ts), lanes]], axis=1)
            return lax.dot_general(k_ext, q_ext[hh][tile_range(sub)[0] * lane_tile:],
                                   (((1,), (1,)), ((), ())), preferred_element_type=F32)

        s_next = scores(*items[0])
        for n, (sub, hh) in enumerate(items):
            s = s_next
            if n + 1 < len(items):
                s_next = scores(*items[n + 1])
            m, l, acc = carry[3 * hh:3 * hh + 3]
            k0 = pl.multiple_of(j * tk + sub * ts, ts)
            first, full = tile_range(sub)
            lo = first * lane_tile
            m_t, l_t, p_t = ([m[:, :lo]], [l[:, :lo]], []) if lo else ([], [], [])
            for qt in range(first, n_tiles):
                ql = slice(qt * lane_tile, (qt + 1) * lane_tile)
                sq = s[:, (qt - first) * lane_tile:(qt - first + 1) * lane_tile]
                if qt < full:
                    keep = (k0 + lax.broadcasted_iota(jnp.int32, (ts, lane_tile), 0)
                            <= q0 + qt * lane_tile + lax.broadcasted_iota(jnp.int32, (ts, lane_tile), 1))
                    sq = jnp.where(keep, sq, NEG_BIG)
                mq = jnp.maximum(m[:, ql], jnp.max(sq, axis=0, keepdims=True))
                pq = jnp.exp2(sq - mq)
                l_t.append(jnp.exp2(m[:, ql] - mq) * l[:, ql] + jnp.sum(pq, axis=0, keepdims=True))
                m_t.append(mq)
                p_t.append(pq.astype(BF16))
            m_new = jnp.concatenate(m_t, axis=1)
            alpha = jnp.exp2(m[:, lo:] - m_new[:, lo:])
            vt = vt_ref[hh * hd:(hh + 1) * hd, pl.ds(k0, ts)]
            acc_hi = alpha * acc[:, lo:] + jnp.dot(vt, jnp.concatenate(p_t, axis=1),
                                                   preferred_element_type=F32)
            carry[3 * hh:3 * hh + 3] = [m_new, jnp.concatenate(l_t, axis=1),
                                        jnp.concatenate([acc[:, :lo], acc_hi], axis=1) if lo else acc_hi]
        return tuple(carry)

    init = (jnp.full((1, tq), NEG_BIG, F32), jnp.zeros((1, tq), F32),
            jnp.zeros((hd, tq), F32)) * heads_per_step
    n_full = q0 // tk
    carry = lax.fori_loop(0, n_full, lambda j, c: step(j, c, False), init)
    carry = step(n_full, carry, True)
    for hh in range(heads_per_step):
        _, l, acc = carry[3 * hh:3 * hh + 3]
        o_ref[:, hh * hd:(hh + 1) * hd] = (acc / l).T.astype(o_ref.dtype)


def _fox_attn(q, k, v, aug, batch, seq, heads, head_dim, weights_to_round):
    tq = min(ATTN_Q_TILE, seq)
    tk = min(ATTN_K_TILE, seq)
    nq = seq // tq
    hps = ATTN_HEADS_PER_STEP if heads % ATTN_HEADS_PER_STEP == 0 else 1
    groups = heads // hps
    width = hps * head_dim
    rounding = _WeightRounding(weights_to_round, batch * groups * nq,
                               lambda b, h, i: (b * groups + h) * nq + i)
    kernel = functools.partial(_fox_attn_kernel, tq=tq, tk=tk, ts=min(ATTN_K_SUB, tk), hd=head_dim,
                               heads_per_step=hps, n_round=len(rounding))
    vmem = (7 * _nbytes((seq, width), BF16) + 6 * hps * _nbytes((tk, tq), F32) + rounding.vmem_bytes()
            + (8 << 20))
    outs = pl.pallas_call(
        kernel,
        out_shape=[jax.ShapeDtypeStruct(q.shape, BF16)] + rounding.out_shapes(),
        grid=(batch, groups, nq),
        in_specs=[pl.BlockSpec((tq, width), lambda b, h, i: (b * nq + i, h)),
                  pl.BlockSpec((seq, width), lambda b, h, i: (b, h)),
                  pl.BlockSpec((seq, width), lambda b, h, i: (b, h)),
                  pl.BlockSpec((seq, width), lambda b, h, i: (b, h))] + rounding.in_specs(),
        out_specs=[pl.BlockSpec((tq, width), lambda b, h, i: (b * nq + i, h))] + rounding.out_specs(),
        scratch_shapes=[pltpu.VMEM((width, seq), BF16)],
        compiler_params=_params(("arbitrary", "arbitrary", "arbitrary"), vmem),
        name="fox_prompt_attention",
    )(q, k, v, aug, *rounding.operands())
    return outs[0], outs[1:]


def _page_suffix_kernel(lf_ref, out_ref, *, heads):
    x = lf_ref[...]
    cols = x.shape[1]
    col = lax.broadcasted_iota(jnp.int32, x.shape, 1)
    incl = x
    tot = x
    sh = heads
    while sh < cols:
        incl = incl + jnp.where(col + sh < cols, pltpu.roll(incl, cols - sh, 1), 0.0)
        tot = tot + pltpu.roll(tot, sh, 1)
        sh *= 2
    out_ref[:, :cols] = incl - x
    out_ref[:, cols:] = tot


def _page_suffix(cache_lf, heads):
    n_rows, cols = cache_lf.shape
    tr = 256 if n_rows % 256 == 0 else n_rows
    return pl.pallas_call(
        functools.partial(_page_suffix_kernel, heads=heads),
        out_shape=jax.ShapeDtypeStruct((n_rows, 2 * cols), F32),
        grid=(n_rows // tr,),
        in_specs=[pl.BlockSpec((tr, cols), lambda r: (r, 0))],
        out_specs=pl.BlockSpec((tr, 2 * cols), lambda r: (r, 0)),
        compiler_params=_params(("arbitrary",), 12 * _nbytes((tr, cols), F32) + (4 << 20)),
        name="fox_page_logf_suffix",
    )(cache_lf)


def _fox_decode_kernel(pt_ref, q_ref, kn_ref, vn_ref, cn_ref, *refs, pages, page_id):
    k_refs = refs[:pages]
    v_refs = refs[pages:2 * pages]
    bias_refs = refs[2 * pages:3 * pages]
    o_ref, m_ref, l_ref, acc_ref, run_ref = refs[3 * pages:]
    s_idx = pl.program_id(1)
    heads = q_ref.shape[0]
    cols = k_refs[0].shape[0]
    head_mask = (lax.broadcasted_iota(jnp.int32, (heads, cols), 1) % heads
                 == lax.broadcasted_iota(jnp.int32, (heads, cols), 0))

    @pl.when(s_idx == 0)
    def _():
        m_ref[...] = jnp.full_like(m_ref, NEG_BIG)
        l_ref[...] = jnp.zeros_like(l_ref)
        acc_ref[...] = jnp.zeros_like(acc_ref)
        run_ref[...] = jnp.zeros_like(run_ref)

    q = q_ref[...]
    cn = cn_ref[...]
    run = run_ref[...]
    scores = []
    for g in range(pages):
        s = lax.dot_general(q, k_refs[g][...].astype(BF16), (((1,), (1,)), ((), ())),
                            preferred_element_type=F32)
        row = page_id(pt_ref, pl.program_id(0), s_idx, g) % F32_SUBLANES
        page_bias = bias_refs[g][pl.ds(row, 1), :]
        scores.append(jnp.where(head_mask, s + (page_bias[:, :cols] + (run + cn)), NEG_BIG))
        run = run + page_bias[:, cols:]
    run_ref[...] = run
    m = m_ref[...]
    m_new = m
    for s in scores:
        m_new = jnp.maximum(m_new, jnp.max(s, axis=-1, keepdims=True))
    alpha = jnp.exp(m - m_new)
    l = alpha * l_ref[...]
    acc = alpha * acc_ref[...]
    for g in range(pages):
        p = jnp.exp(scores[g] - m_new)
        l = l + jnp.sum(p, axis=-1, keepdims=True)
        acc = acc + jnp.dot(p.astype(BF16), v_refs[g][...].astype(BF16), preferred_element_type=F32)
    m_ref[...] = m_new
    l_ref[...] = l
    acc_ref[...] = acc

    @pl.when(s_idx == pl.num_programs(1) - 1)
    def _():
        s_new = jnp.sum(q.astype(F32) * kn_ref[...], axis=-1, keepdims=True)
        m = m_ref[...]
        m_new = jnp.maximum(m, s_new)
        alpha = jnp.exp(m - m_new)
        p_new = jnp.exp(s_new - m_new)
        l = alpha * l_ref[...] + p_new
        acc = alpha * acc_ref[...] + p_new.astype(BF16).astype(F32) * vn_ref[...]
        o_ref[...] = acc / l


def _fox_decode(page_table, pool_base, q, k_new, v_new, logf_new, cache_k, cache_v, cache_lf):
    bs, n_pages = page_table.shape
    _, heads, hd = q.shape
    n_pool, cols, _ = cache_k.shape
    assert n_pool % F32_SUBLANES == 0 and pool_base % F32_SUBLANES == 0
    page_bias = _page_suffix(cache_lf, heads)
    g_pages = PAGES_PER_STEP if n_pages % PAGES_PER_STEP == 0 else 1
    steps = n_pages // g_pages

    def page_id(pt, b, s, g):
        return pool_base + pt[b * n_pages + (n_pages - 1 - (s * g_pages + g))]

    kernel = functools.partial(_fox_decode_kernel, pages=g_pages, page_id=page_id)

    def page_idx(g):
        return lambda b, s, pt: (page_id(pt, b, s, g), 0, 0)

    def row_block_idx(g):
        return lambda b, s, pt: (page_id(pt, b, s, g) // F32_SUBLANES, 0)

    tok = pl.BlockSpec((None, heads, hd), lambda b, s, pt: (b, 0, 0))
    in_specs = [tok, tok, tok, pl.BlockSpec((None, 1, cols), lambda b, s, pt: (b, 0, 0))]
    in_specs += [pl.BlockSpec((None, cols, hd), page_idx(g)) for g in range(g_pages)]
    in_specs += [pl.BlockSpec((None, cols, hd), page_idx(g)) for g in range(g_pages)]
    in_specs += [pl.BlockSpec((F32_SUBLANES, 2 * cols), row_block_idx(g)) for g in range(g_pages)]
    vmem = (4 * g_pages * _nbytes((cols, hd), F32) + 2 * g_pages * _nbytes((cols, hd), BF16)
            + 4 * g_pages * _nbytes((heads, cols), F32) + (4 << 20))
    grid_spec = pltpu.PrefetchScalarGridSpec(
        num_scalar_prefetch=1,
        grid=(bs, steps),
        in_specs=in_specs,
        out_specs=pl.BlockSpec((None, heads, hd), lambda b, s, pt: (b, 0, 0)),
        scratch_shapes=[pltpu.VMEM((heads, 1), F32), pltpu.VMEM((heads, 1), F32),
                        pltpu.VMEM((heads, hd), F32), pltpu.VMEM((1, cols), F32)])
    return pl.pallas_call(
        kernel,
        out_shape=jax.ShapeDtypeStruct((bs, heads, hd), F32),
        grid_spec=grid_spec,
        compiler_params=_params(("arbitrary", "arbitrary"), vmem),
        name="fox_decode_attention",
    )(page_table.reshape(-1), q, k_new, v_new, logf_new,
      *([cache_k] * g_pages), *([cache_v] * g_pages), *([page_bias] * g_pages))


def _proj_residual_kernel(a_ref, w_ref, x_ref, gate_ref, o_ref):
    z = jnp.dot(a_ref[...], w_ref[...].astype(BF16), preferred_element_type=F32)
    o_ref[...] = x_ref[...] + gate_ref[...] * z


def _proj_residual(a, w, layer, x, mods, mod_rows, tiles_per_group, gate_chunk, tm):
    m_rows, k = a.shape
    d = w.shape[2]
    tn = min(COL_TILE, d)
    vmem = (2 * _nbytes((tm, k), BF16) + 2 * _nbytes((k, tn), F32) + _nbytes((k, tn), BF16)
            + 5 * _nbytes((tm, tn), F32) + (4 << 20))
    return pl.pallas_call(
        _proj_residual_kernel,
        out_shape=jax.ShapeDtypeStruct((m_rows, d), F32),
        grid=(m_rows // tm, d // tn),
        in_specs=[pl.BlockSpec((tm, k), lambda m, n: (m, 0)),
                  pl.BlockSpec((None, k, tn), lambda m, n: (layer, 0, n)),
                  pl.BlockSpec((tm, tn), lambda m, n: (m, n)),
                  pl.BlockSpec((None, mod_rows, tn),
                               lambda m, n: (m // tiles_per_group, 0, gate_chunk * (d // tn) + n))],
        out_specs=pl.BlockSpec((tm, tn), lambda m, n: (m, n)),
        compiler_params=_params(("arbitrary", "arbitrary"), vmem),
        name="mixer_out_proj",
    )(a, w, x, mods)


def _mlp_kernel(x_ref, xs_ref, g_ref, mod_ref, mods_ref, wu_ref, wd_ref, fg_ref, o_ref, os_ref, h_ref, *,
                tm, final_norm):
    f = pl.program_id(1)
    d = x_ref.shape[1]

    @pl.when(f == 0)
    def _():
        h_ref[:tm, :] = _norm_mod(x_ref[...], g_ref[...], mod_ref[:, :d], mod_ref[:, d:2 * d]).astype(BF16)
        h_ref[tm:, :] = _norm_mod(xs_ref[...], g_ref[...], mods_ref[:, :d],
                                  mods_ref[:, d:2 * d]).astype(BF16)
        o_ref[...] = jnp.zeros_like(o_ref)
        os_ref[...] = jnp.zeros_like(os_ref)

    u = jnp.dot(h_ref[...], wu_ref[...].astype(BF16), preferred_element_type=F32)
    u = jnp.square(jnp.maximum(u, 0.0)).astype(BF16)
    z = jnp.dot(u, wd_ref[...].astype(BF16), preferred_element_type=F32)
    o_ref[...] += z[:tm]
    os_ref[...] += z[tm:]

    @pl.when(f == pl.num_programs(1) - 1)
    def _():
        for x_r, mod_r, o_r in ((x_ref, mod_ref, o_ref), (xs_ref, mods_ref, os_ref)):
            y = x_r[...] + mod_r[:, 2 * d:] * o_r[...]
            if final_norm:
                y = y * lax.rsqrt(jnp.mean(y * y, axis=-1, keepdims=True) + EPS) * fg_ref[...]
            o_r[...] = y


def _mlp(x, xs, g, mods_p, mods_s, tiles_per_batch, w_up, w_down, layer, final_g, final_norm, tm):
    m_rows, d = x.shape
    srows = xs.shape[0]
    d_ff = w_up.shape[2]
    tf = min(COL_TILE, d_ff)
    kernel = functools.partial(_mlp_kernel, tm=tm, final_norm=final_norm)
    x_buffers = 1 if w_up.dtype == F32 else 2
    x_mode = {"pipeline_mode": pl.Buffered(1)} if x_buffers == 1 else {}
    vmem = ((2 + x_buffers) * _nbytes((tm, d), F32) + _nbytes((tm + srows, d), BF16)
            + 4 * _nbytes((d, tf), w_up.dtype) + 2 * _nbytes((d, tf), BF16) + 4 * _nbytes((tm, tf), F32)
            + (4 << 20))
    full = lambda m, f: (0, 0)
    vec = pl.BlockSpec((1, d), full)
    return pl.pallas_call(
        kernel,
        out_shape=(jax.ShapeDtypeStruct((m_rows, d), F32), jax.ShapeDtypeStruct((srows, d), F32)),
        grid=(m_rows // tm, d_ff // tf),
        in_specs=[pl.BlockSpec((tm, d), lambda m, f: (m, 0), **x_mode),
                  pl.BlockSpec((srows, d), full),
                  vec,
                  _mod_spec(1, d, 3, tiles_per_batch, 3),
                  _mod_spec(srows, d, 3, m_rows, 3),
                  pl.BlockSpec((None, d, tf), lambda m, f: (layer, 0, f)),
                  pl.BlockSpec((None, tf, d), lambda m, f: (layer, f, 0)),
                  vec],
        out_specs=(pl.BlockSpec((tm, d), lambda m, f: (m, 0)), pl.BlockSpec((srows, d), full)),
        scratch_shapes=[pltpu.VMEM((tm + srows, d), BF16)],
        compiler_params=_params(("arbitrary", "arbitrary"), vmem),
        name="relu2_mlp",
    )(x, xs, g, mods_p, mods_s, w_up, w_down, final_g)


def _ret_proj_kernel(x_ref, xs_ref, g_ref, mod_ref, mods_ref, w_ref, cs_ref, css_ref,
                     q_ref, k_ref, v_ref, gt_ref, zs_ref, h_ref, *, nq, tm, key_dim, k_scale):
    n = pl.program_id(1)
    d = x_ref.shape[1]

    @pl.when(n == 0)
    def _():
        h_ref[:tm, :] = _norm_mod(x_ref[...], g_ref[...], mod_ref[:, :d], mod_ref[:, d:]).astype(BF16)
        h_ref[tm:, :] = _norm_mod(xs_ref[...], g_ref[...], mods_ref[:, :d], mods_ref[:, d:]).astype(BF16)

    z = jnp.dot(h_ref[...], w_ref[...].astype(BF16), preferred_element_type=F32)
    zp = z[:tm]
    zs = z[tm:]
    half = key_dim // 2

    def rotate(zz, table_ref):
        cos = table_ref[:, :half]
        sin = table_ref[:, half:]
        parts = []
        for hh in range(zz.shape[1] // key_dim):
            x1 = zz[:, hh * key_dim:hh * key_dim + half]
            x2 = zz[:, hh * key_dim + half:(hh + 1) * key_dim]
            parts += [x1 * cos - x2 * sin, x1 * sin + x2 * cos]
        return jnp.concatenate(parts, axis=1)

    last_tile = pl.program_id(0) == pl.num_programs(0) - 1
    in_q = n < nq
    in_k = jnp.logical_and(n >= nq, n < 2 * nq)

    @pl.when(in_q)
    def _():
        q_ref[...] = rotate(zp, cs_ref).astype(BF16)

    @pl.when(in_k)
    def _():
        k_ref[...] = rotate(zp, cs_ref) * k_scale

    @pl.when(jnp.logical_and(n >= 2 * nq, n < 4 * nq))
    def _():
        v_ref[...] = zp.astype(BF16)

    @pl.when(n >= 4 * nq)
    def _():
        gt_ref[...] = zp

    @pl.when(jnp.logical_and(last_tile, n < 2 * nq))
    def _():
        zs_ref[...] = rotate(zs, css_ref) * jnp.where(in_k, k_scale, 1.0)

    @pl.when(jnp.logical_and(last_tile, n >= 2 * nq))
    def _():
        zs_ref[...] = zs


def _ret_proj(x, xs, g, mods_p, mods_s, tiles_per_batch, w_in, layer, rot, rot_s, key_dim, tm):
    m_rows, d = x.shape
    srows = xs.shape[0]
    tn = min(COL_TILE, d)
    nq = d // tn
    kernel = functools.partial(_ret_proj_kernel, nq=nq, tm=tm, key_dim=key_dim, k_scale=key_dim ** -0.5)
    last_tile = m_rows // tm - 1

    def col(lo, width):
        return lambda m, n: (m, jnp.clip(n - lo, 0, width - 1))

    vmem = (2 * _nbytes((tm, d), F32) + _nbytes((tm + srows, d), BF16) + 2 * _nbytes((d, tn), F32)
            + _nbytes((d, tn), BF16) + 2 * (2 * _nbytes((tm, tn), BF16) + 2 * _nbytes((tm, tn), F32))
            + 2 * _nbytes((tm, key_dim), F32) + 4 * _nbytes((tm, tn), F32) + (4 << 20))
    full = lambda m, n: (0, 0)
    outs = pl.pallas_call(
        kernel,
        out_shape=(jax.ShapeDtypeStruct((m_rows, d), BF16), jax.ShapeDtypeStruct((m_rows, d), F32),
                   jax.ShapeDtypeStruct((m_rows, 2 * d), BF16), jax.ShapeDtypeStruct((m_rows, 2 * d), F32),
                   jax.ShapeDtypeStruct((srows, 6 * d), F32)),
        grid=(m_rows // tm, 6 * nq),
        in_specs=[pl.BlockSpec((tm, d), lambda m, n: (m, 0)),
                  pl.BlockSpec((srows, d), full),
                  pl.BlockSpec((1, d), full),
                  _mod_spec(1, d, 0, tiles_per_batch, 2),
                  _mod_spec(srows, d, 0, m_rows, 2),
                  pl.BlockSpec((None, d, tn), lambda m, n: (layer, 0, n)),
                  pl.BlockSpec((tm, key_dim), lambda m, n: (m % tiles_per_batch, 0)),
                  pl.BlockSpec((srows, key_dim), full)],
        out_specs=(pl.BlockSpec((tm, tn), col(0, nq)),
                   pl.BlockSpec((tm, tn), col(nq, nq)),
                   pl.BlockSpec((tm, tn), col(2 * nq, 2 * nq)),
                   pl.BlockSpec((tm, tn), col(4 * nq, 2 * nq)),
                   pl.BlockSpec((srows, tn), lambda m, n: (0, jnp.where(m == last_tile, n, 0)))),
        scratch_shapes=[pltpu.VMEM((tm + srows, d), BF16)],
        compiler_params=_params(("arbitrary", "arbitrary"), vmem),
        name="ret_in_proj",
    )(x, xs, g, mods_p, mods_s, w_in, rot, rot_s)
    return outs[:4], outs[4]


def _group_norm_gate(o, gate):
    mu = jnp.mean(o, axis=-1, keepdims=True)
    cen = o - mu
    var = jnp.mean(cen * cen, axis=-1, keepdims=True)
    return _silu(gate) * (cen * lax.rsqrt(var + EPS))


def _ret_chunk_kernel(lg_ref, q_ref, k_ref, v_ref, gt_ref, *rest, n_round, heads_per_step):
    y_ref, s_ref = rest[n_round:n_round + 2]
    dec_ref, xi_ref, zeta_ref = rest[-3:]
    _WeightRounding.run(rest[:n_round], rest[n_round + 2:-3])
    ln = q_ref.shape[0]
    kd = s_ref.shape[1]
    vd = s_ref.shape[2]
    lgs = [jnp.full((1, 1), lg_ref[pl.program_id(1) * heads_per_step + hh], F32)
           for hh in range(heads_per_step)]

    @pl.when(pl.program_id(2) == 0)
    def _():
        s_ref[...] = jnp.zeros_like(s_ref)
        diff = (lax.broadcasted_iota(jnp.int32, (ln, ln), 0)
                - lax.broadcasted_iota(jnp.int32, (ln, ln), 1)).astype(F32)
        n = lax.broadcasted_iota(jnp.int32, (ln, 1), 0).astype(F32)
        for hh, lg in enumerate(lgs):
            dec_ref[hh] = jnp.where(diff >= 0, jnp.exp(lg * jnp.maximum(diff, 0.0)), 0.0)
            xi_ref[hh] = jnp.exp(lg * (n + 1.0))
            zeta_ref[hh] = jnp.exp(lg * (ln - 1.0 - n))

    for hh, lg in enumerate(lgs):
        q = q_ref[:, hh * kd:(hh + 1) * kd]
        k = k_ref[:, hh * kd:(hh + 1) * kd]
        v = v_ref[:, hh * vd:(hh + 1) * vd]
        s0 = s_ref[hh]
        qk = lax.dot_general(q, k.astype(BF16), (((1,), (1,)), ((), ())), preferred_element_type=F32)
        a = (qk * dec_ref[hh]).astype(BF16)
        o = (jnp.dot(a, v, preferred_element_type=F32)
             + jnp.dot(q, s0.astype(BF16), preferred_element_type=F32) * xi_ref[hh])
        kz = (k * zeta_ref[hh]).astype(BF16)
        s_ref[hh] = jnp.exp(lg * ln) * s0 + lax.dot_general(
            kz, v, (((0,), (0,)), ((), ())), preferred_element_type=F32)
        y_ref[:, hh * vd:(hh + 1) * vd] = _group_norm_gate(
            o, gt_ref[:, hh * vd:(hh + 1) * vd]).astype(y_ref.dtype)


def _ret_prompt(lg, q, k, v, gate, batch, seq, heads, key_dim, val_dim, weights_to_round):
    ln = min(RET_CHUNK, seq)
    nc = seq // ln
    hps = RET_HEADS_PER_STEP if heads % RET_HEADS_PER_STEP == 0 else 1
    groups = heads // hps
    rounding = _WeightRounding(weights_to_round, batch * groups * nc,
                               lambda b, h, c, *_: (b * groups + h) * nc + c)
    vmem = (hps * (2 * (_nbytes((ln, key_dim), BF16) + _nbytes((ln, key_dim), F32)
                        + _nbytes((ln, val_dim), BF16) + _nbytes((ln, val_dim), F32)
                        + _nbytes((ln, val_dim), BF16))
                   + 4 * _nbytes((key_dim, val_dim), F32) + 8 * _nbytes((ln, val_dim), F32)
                   + _nbytes((ln, ln), F32))
            + rounding.vmem_bytes() + (8 << 20))
    grid_spec = pltpu.PrefetchScalarGridSpec(
        num_scalar_prefetch=1,
        grid=(batch, groups, nc),
        in_specs=[pl.BlockSpec((ln, hps * key_dim), lambda b, h, c, lg: (b * nc + c, h)),
                  pl.BlockSpec((ln, hps * key_dim), lambda b, h, c, lg: (b * nc + c, h)),
                  pl.BlockSpec((ln, hps * val_dim), lambda b, h, c, lg: (b * nc + c, h)),
                  pl.BlockSpec((ln, hps * val_dim), lambda b, h, c, lg: (b * nc + c, h))]
        + rounding.in_specs(),
        out_specs=[pl.BlockSpec((ln, hps * val_dim), lambda b, h, c, lg: (b * nc + c, h)),
                   pl.BlockSpec((None, hps, key_dim, val_dim), lambda b, h, c, lg: (b, h, 0, 0))]
        + rounding.out_specs(),
        scratch_shapes=[pltpu.VMEM((hps, ln, ln), F32), pltpu.VMEM((hps, ln, 1), F32),
                        pltpu.VMEM((hps, ln, 1), F32)])
    outs = pl.pallas_call(
        functools.partial(_ret_chunk_kernel, n_round=len(rounding), heads_per_step=hps),
        out_shape=[jax.ShapeDtypeStruct((batch * seq, heads * val_dim), BF16),
                   jax.ShapeDtypeStruct((batch, heads, key_dim, val_dim), F32)] + rounding.out_shapes(),
        grid_spec=grid_spec,
        compiler_params=_params(("arbitrary", "arbitrary", "arbitrary"), vmem),
        name="ret_prompt_chunks",
    )(lg, q, k, v, gate, *rounding.operands())
    return outs[0], outs[1], outs[2:]


def _ret_decode_kernel(lg_ref, q_ref, k_ref, v_ref, gt_ref, s0_ref, y_ref, s_ref):
    q = q_ref[...]
    k = k_ref[...]
    q_cols = q.T
    k_cols = k.T
    a_all = jnp.sum(q * k, axis=-1, keepdims=True)
    for h in range(s0_ref.shape[0]):
        gamma = jnp.exp(jnp.full((1, 1), lg_ref[h], F32))
        s0 = s0_ref[h]
        v = v_ref[h:h + 1, :]
        qs = jnp.sum(q_cols[:, h:h + 1] * s0, axis=0, keepdims=True)
        o = a_all[h:h + 1, :] * v + qs * gamma
        s_ref[h] = gamma * s0 + k_cols[:, h:h + 1] * v
        y_ref[h:h + 1, :] = _group_norm_gate(o, gt_ref[h:h + 1, :])


def _ret_decode(lg, q, k, v, gate, state):
    bs, heads, key_dim, val_dim = state.shape

    def vec(width):
        return pl.BlockSpec((None, heads, width), lambda b, lg: (b, 0, 0))

    state_spec = pl.BlockSpec((None, heads, key_dim, val_dim), lambda b, lg: (b, 0, 0, 0))
    grid_spec = pltpu.PrefetchScalarGridSpec(
        num_scalar_prefetch=1,
        grid=(bs,),
        in_specs=[vec(key_dim), vec(key_dim), vec(val_dim), vec(val_dim), state_spec],
        out_specs=(vec(val_dim), state_spec))
    y, s = pl.pallas_call(
        _ret_decode_kernel,
        out_shape=(jax.ShapeDtypeStruct((bs, heads, val_dim), F32),
                   jax.ShapeDtypeStruct(state.shape, F32)),
        grid_spec=grid_spec,
        compiler_params=_params(("arbitrary",), 4 * _nbytes(state.shape[1:], F32) + (16 << 20)),
        name="ret_decode_step",
    )(lg, q.reshape(bs, heads, key_dim), k.reshape(bs, heads, key_dim),
      v.reshape(bs, heads, val_dim), gate.reshape(bs, heads, val_dim), state)
    return y.reshape(bs, heads * val_dim), s


def _rotary_table(pos, key_dim):
    half = key_dim // 2
    inv = (np.float32(1.0) / np.float32(ROPE_BASE) ** np.linspace(0.0, 1.0, half, dtype=np.float32))
    ang = (np.asarray(pos, np.float32)[:, None] * inv.astype(np.float32)[None, :]).astype(np.float64)
    return jnp.asarray(np.concatenate([np.cos(ang), np.sin(ang)], axis=1), F32)


def _pad_rows(a, rows):
    return jnp.pad(a, ((0, rows - a.shape[0]), (0, 0)))


def kernel(x_prompt, x_sample, c_prompt, c_sample, cache_fox_k, cache_fox_v, cache_fox_logf, state_ret, page_table, norm1_g, norm2_g, ada_w, ada_b, fox_w_in, fox_b_f, fox_w_out, ret_w_in, ret_w_out, mlp_w_up, mlp_w_down, final_g):
    bp, tp, d = x_prompt.shape
    bs, ts, _ = x_sample.shape
    assert ts == 1, "the decode kernels handle one new token per sequence"
    depth = ada_w.shape[0]
    _, n_pool, page, fox_heads, fox_hd = cache_fox_k.shape
    _, _, ret_heads, key_dim, val_dim = state_ret.shape
    past_len = page_table.shape[1] * page
    srows = -(-bs // BF16_SUBLANES) * BF16_SUBLANES
    tm = min(ROW_TILE, tp)
    tiles_p = tp // tm

    mod_rows = -(-(bs + bp) // 8) * 8
    c_all = _pad_rows(jnp.concatenate([c_sample, c_prompt], axis=0), max(mod_rows, srows))
    mods = _ada(c_all, ada_w, ada_b)

    lg = jnp.log1p(-jnp.exp2(-5.0 - jnp.arange(ret_heads, dtype=F32)))
    rot_p = _rotary_table(np.arange(tp), key_dim)
    rot_s = _rotary_table(np.full((srows,), past_len), key_dim)

    yp = x_prompt.reshape(bp * tp, d)
    ys = _pad_rows(x_sample.reshape(bs, d), srows)
    fg = final_g.reshape(1, d)
    outs = {name: [] for name in ("kp", "vp", "lfp", "ks", "vs", "lfs", "sp", "ss")}

    ret_rounded = None
    for i in range(depth):
        mods_p = mods[i, bs:bs + bp].reshape(bp, 1, 6 * d)
        mods_s = mods[i, :srows].reshape(1, srows, 6 * d)
        g1 = norm1_g[i].reshape(1, d)
        g2 = norm2_g[i].reshape(1, d)
        j = i // 2
        if i % 2 == 0:
            b_f = fox_b_f[j].reshape(1, fox_heads)
            (q, kf, vf, kb, vb, lf), sample = _fox_proj(
                yp, ys, g1, mods_p, mods_s, tiles_p, fox_w_in, j, b_f, tm,
                fox_hd ** -0.5 * LOG2E, fox_hd ** -0.5)
            aug = _cumsum(lf, bp, tp, fox_hd)
            to_round = [(fox_w_out, j), (mlp_w_up, i), (mlp_w_down, i)]
            if i + 1 < depth:
                jr = (i + 1) // 2
                to_round += [(ret_w_in, jr), (ret_w_out, jr), (mlp_w_up, i + 1), (mlp_w_down, i + 1)]
            o, rounded = _fox_attn(q, kb, vb, aug, bp, tp, fox_heads, fox_hd, to_round)
            w_out, w_up, w_down = rounded[:3]
            l_mlp = 0
            ret_rounded = rounded[3:] if i + 1 < depth else None
            yp = _proj_residual(o, w_out, 0, yp, mods_p, 1, tiles_p, 2, tm)
            outs["kp"].append(kf.reshape(bp, tp, fox_heads, fox_hd))
            outs["vp"].append(vf.reshape(bp, tp, fox_heads, fox_hd))
            outs["lfp"].append(lf.reshape(bp, tp, fox_heads))
            zs, lf = sample
            kf, vf = zs[:, d:2 * d], zs[:, 2 * d:]
            bf16_rounded = lambda a: a.astype(BF16).astype(F32)
            o = _fox_decode(
                page_table, j * n_pool,
                zs[:bs, :d].astype(BF16).reshape(bs, fox_heads, fox_hd),
                bf16_rounded(kf[:bs]).reshape(bs, fox_heads, fox_hd),
                bf16_rounded(vf[:bs]).reshape(bs, fox_heads, fox_hd),
                jnp.tile(lf[:bs], (1, page)).reshape(bs, 1, page * fox_heads),
                cache_fox_k.reshape(-1, page * fox_heads, fox_hd),
                cache_fox_v.reshape(-1, page * fox_heads, fox_hd),
                cache_fox_logf.reshape(-1, page * fox_heads))
            o = _pad_rows(o.reshape(bs, d), srows).astype(BF16)
            ys = _proj_residual(o, w_out, 0, ys, mods_s, srows, 1, 2, srows)
            outs["ks"].append(kf[:bs].reshape(bs, ts, fox_heads, fox_hd))
            outs["vs"].append(vf[:bs].reshape(bs, ts, fox_heads, fox_hd))
            outs["lfs"].append(lf[:bs].reshape(bs, ts, fox_heads))
        else:
            if ret_rounded is None:
                w_in, w_out, w_up, w_down = ret_w_in, ret_w_out, mlp_w_up, mlp_w_down
                l_mix, l_mlp = j, i
            else:
                w_in, w_out, w_up, w_down = ret_rounded
                l_mix = l_mlp = 0
            (q, k, v, gate), zs = _ret_proj(yp, ys, g1, mods_p, mods_s, tiles_p, w_in, l_mix,
                                            rot_p, rot_s, key_dim, tm)
            y, s, _ = _ret_prompt(lg, q, k, v, gate, bp, tp, ret_heads, key_dim, val_dim, [])
            yp = _proj_residual(y, w_out, l_mix, yp, mods_p, 1, tiles_p, 2, tm)
            outs["sp"].append(s)
            zs = zs[:bs]
            bf16_rounded = lambda a: a.astype(BF16).astype(F32)
            y, s = _ret_decode(lg, bf16_rounded(zs[:, :d]), zs[:, d:2 * d], bf16_rounded(zs[:, 2 * d:4 * d]),
                               zs[:, 4 * d:], state_ret[j].astype(F32))
            ys = _proj_residual(_pad_rows(y, srows).astype(BF16), w_out, l_mix, ys, mods_s, srows, 1, 2, srows)
            outs["ss"].append(s)
        yp, ys = _mlp(yp, ys, g2, mods_p, mods_s, tiles_p, w_up, w_down, l_mlp, fg, i == depth - 1, tm)

    return (yp.reshape(bp, tp, d), ys[:bs].reshape(bs, ts, d),
            jnp.stack(outs["kp"]), jnp.stack(outs["vp"]), jnp.stack(outs["lfp"]),
            jnp.stack(outs["ks"]), jnp.stack(outs["vs"]), jnp.stack(outs["lfs"]),
            jnp.stack(outs["sp"]), jnp.stack(outs["ss"]))
```

```python
import functools

import jax
import jax.numpy as jnp
import numpy as np
from jax import lax
from jax.experimental import pallas as pl
from jax.experimental.pallas import tpu as pltpu

F32 = jnp.float32
BF16 = jnp.bfloat16

EPS = 1e-6
ROPE_BASE = 10000.0
LOG2E = 1.4426950408889634
NEG_BIG = -1e30

V7X_VMEM_LIMIT_BYTES = 60000 * 1024
F32_SUBLANES = 8
BF16_SUBLANES = 16

ROW_TILE = 1024
COL_TILE = 512
OUT_PROJ_COL_TILE = 1024
ATTN_Q_TILE = 1024
ATTN_K_TILE = 1024
ATTN_K_SUB = 512
ATTN_HEADS_PER_STEP = 2
CUMSUM_TILE = 512
RET_HEADS_PER_STEP = 4
RET_CHUNK = 512
PAGES_PER_STEP = 8


def _params(semantics, vmem_bytes):
    return pltpu.CompilerParams(dimension_semantics=semantics,
                                vmem_limit_bytes=int(min(vmem_bytes, V7X_VMEM_LIMIT_BYTES)))


def _nbytes(shape, dtype):
    n = 1
    for s in shape:
        n *= s
    return n * jnp.dtype(dtype).itemsize


def _norm_mod(x, g, shift, scale):
    y = x * lax.rsqrt(jnp.mean(x * x, axis=-1, keepdims=True) + EPS)
    return y * (g * (1.0 + scale)) + shift


def _silu(x):
    return x * jax.nn.sigmoid(x)


def _log_sigmoid(x):
    return jnp.minimum(x, 0.0) - jnp.log1p(jnp.exp(-jnp.abs(x)))


def _split3(x):
    hi = x.astype(BF16)
    r1 = x - hi.astype(F32)
    mid = r1.astype(BF16)
    lo = (r1 - mid.astype(F32)).astype(BF16)
    return hi, mid, lo


class _WeightRounding:
    def __init__(self, weights, n_steps, step_index):
        self.items = []
        for w, layer in weights:
            slab = w.shape[1] // n_steps
            assert w.shape[1] % n_steps == 0 and slab % BF16_SUBLANES == 0, (w.shape, n_steps)
            self.items.append((w, layer, slab))
        self.step_index = step_index

    def __len__(self):
        return len(self.items)

    def in_specs(self):
        return [pl.BlockSpec((None, slab, w.shape[2]),
                             lambda *ids, layer=layer: (layer, self.step_index(*ids), 0))
                for w, layer, slab in self.items]

    def out_specs(self):
        return [pl.BlockSpec((None, slab, w.shape[2]), lambda *ids: (0, self.step_index(*ids), 0))
                for w, _, slab in self.items]

    def out_shapes(self):
        return [jax.ShapeDtypeStruct((1,) + w.shape[1:], BF16) for w, _, _ in self.items]

    def operands(self):
        return [w for w, _, _ in self.items]

    def vmem_bytes(self):
        return sum(2 * _nbytes((slab, w.shape[2]), F32) + 2 * _nbytes((slab, w.shape[2]), BF16)
                   for w, _, slab in self.items)

    @staticmethod
    def run(in_refs, out_refs):
        for i_ref, o_ref in zip(in_refs, out_refs):
            o_ref[...] = i_ref[...].astype(BF16)


def _mod_spec(rows, width, chunk, tiles_per_group, n_chunks=1):
    assert chunk % n_chunks == 0
    return pl.BlockSpec((None, rows, n_chunks * width),
                        lambda m, n: (m // tiles_per_group, 0, chunk // n_chunks))


def _ada_kernel(c_ref, w_ref, b_ref, o_ref):
    o_ref[...] = jnp.dot(_silu(c_ref[...]), w_ref[...], preferred_element_type=F32) + b_ref[...]


def _ada(c_all, ada_w, ada_b):
    depth, d, n6 = ada_w.shape
    rows = c_all.shape[0]
    tn = min(1024, n6)
    vmem = 2 * (_nbytes((d, tn), F32) + _nbytes((rows, tn), F32)) + _nbytes((rows, d), F32) * 2 + (4 << 20)
    return pl.pallas_call(
        _ada_kernel,
        out_shape=jax.ShapeDtypeStruct((depth, rows, n6), F32),
        grid=(depth, n6 // tn),
        in_specs=[pl.BlockSpec((rows, d), lambda i, n: (0, 0)),
                  pl.BlockSpec((None, d, tn), lambda i, n: (i, 0, n)),
                  pl.BlockSpec((None, 1, tn), lambda i, n: (i, 0, n))],
        out_specs=pl.BlockSpec((None, rows, tn), lambda i, n: (i, 0, n)),
        compiler_params=_params(("arbitrary", "arbitrary"), vmem),
        name="ada_modulation",
    )(c_all, ada_w, ada_b.reshape(depth, 1, n6))


def _fox_proj_kernel(x_ref, xs_ref, g_ref, mod_ref, mods_ref, w_ref, wf_ref, bf_ref,
                     q_ref, kf_ref, vf_ref, kb_ref, vb_ref, lf_ref, zs_ref, lfs_ref, h_ref, *,
                     nq, tm, q_scale, q_scale_s):
    n = pl.program_id(1)
    d = x_ref.shape[1]

    @pl.when(n == 0)
    def _():
        w_f = wf_ref[...].astype(BF16)
        for x_r, mod_r, lf_r, rows in ((x_ref, mod_ref, lf_ref, slice(0, tm)),
                                       (xs_ref, mods_ref, lfs_ref, slice(tm, None))):
            h = _norm_mod(x_r[...], g_ref[...], mod_r[:, :d], mod_r[:, d:]).astype(BF16)
            h_ref[rows, :] = h
            fl = lax.dot_general(h, w_f, (((1,), (1,)), ((), ())), preferred_element_type=F32)
            lf_r[...] = _log_sigmoid(fl + bf_ref[...])

    z = lax.dot_general(h_ref[...], w_ref[...].astype(BF16), (((1,), (1,)), ((), ())),
                        preferred_element_type=F32)
    zp = z[:tm]
    zs = z[tm:]
    in_q = n < nq

    @pl.when(in_q)
    def _():
        q_ref[...] = (zp * q_scale).astype(BF16)

    @pl.when(jnp.logical_and(n >= nq, n < 2 * nq))
    def _():
        kf_ref[...] = zp
        kb_ref[...] = zp.astype(BF16)

    @pl.when(n >= 2 * nq)
    def _():
        vf_ref[...] = zp
        vb_ref[...] = zp.astype(BF16)

    @pl.when(pl.program_id(0) == pl.num_programs(0) - 1)
    def _():
        zs_ref[...] = zs * jnp.where(in_q, q_scale_s, 1.0)


def _fox_proj(x, xs, g, mods_p, mods_s, tiles_per_batch, w_in, layer, b_f, tm, q_scale, q_scale_s):
    m_rows, d = x.shape
    srows = xs.shape[0]
    heads = b_f.shape[1]
    assert w_in.shape[2] - 3 * d == heads and (3 * d) % heads == 0
    w_t = jnp.swapaxes(w_in, 1, 2)
    tn = min(COL_TILE, d)
    nq = d // tn
    kernel = functools.partial(_fox_proj_kernel, nq=nq, tm=tm, q_scale=q_scale, q_scale_s=q_scale_s)

    last_tile = m_rows // tm - 1

    def col(lo):
        return lambda m, n: (m, jnp.clip(n - lo, 0, nq - 1))

    vmem = (2 * _nbytes((tm, d), F32) + _nbytes((tm + srows, d), BF16) + 2 * _nbytes((d, tn), F32)
            + _nbytes((d, tn), BF16) + 2 * (3 * _nbytes((tm, tn), BF16) + 2 * _nbytes((tm, tn), F32))
            + 3 * _nbytes((tm, tn), F32) + (4 << 20))
    out_f = jax.ShapeDtypeStruct((m_rows, d), F32)
    out_b = jax.ShapeDtypeStruct((m_rows, d), BF16)
    full = lambda m, n: (0, 0)
    outs = pl.pallas_call(
        kernel,
        out_shape=(out_b, out_f, out_f, out_b, out_b, jax.ShapeDtypeStruct((m_rows, heads), F32),
                   jax.ShapeDtypeStruct((srows, 3 * d), F32), jax.ShapeDtypeStruct((srows, heads), F32)),
        grid=(m_rows // tm, 3 * nq),
        in_specs=[pl.BlockSpec((tm, d), lambda m, n: (m, 0)),
                  pl.BlockSpec((srows, d), full),
                  pl.BlockSpec((1, d), full),
                  _mod_spec(1, d, 0, tiles_per_batch, 2),
                  _mod_spec(srows, d, 0, m_rows, 2),
                  pl.BlockSpec((None, tn, d), lambda m, n: (layer, n, 0)),
                  pl.BlockSpec((None, heads, d), lambda m, n: (layer, 3 * d // heads, 0)),
                  pl.BlockSpec((1, heads), full)],
        out_specs=(pl.BlockSpec((tm, tn), col(0)),
                   pl.BlockSpec((tm, tn), col(nq)),
                   pl.BlockSpec((tm, tn), col(2 * nq)),
                   pl.BlockSpec((tm, tn), col(nq)),
                   pl.BlockSpec((tm, tn), col(2 * nq)),
                   pl.BlockSpec((tm, heads), lambda m, n: (m, 0)),
                   pl.BlockSpec((srows, tn), lambda m, n: (0, jnp.where(m == last_tile, n, 0))),
                   pl.BlockSpec((srows, heads), full)),
        scratch_shapes=[pltpu.VMEM((tm + srows, d), BF16)],
        compiler_params=_params(("arbitrary", "arbitrary"), vmem),
        name="fox_in_proj",
    )(x, xs, g, mods_p, mods_s, w_t, w_t, b_f)
    return outs[:6], outs[6:]


def _cumsum_kernel(lf_ref, aug_ref, carry_ref, *, hd):
    @pl.when(pl.program_id(1) == 0)
    def _():
        carry_ref[...] = jnp.zeros_like(carry_ref)

    tc, heads = lf_ref.shape
    row = lax.broadcasted_iota(jnp.int32, (tc, tc), 0)
    col = lax.broadcasted_iota(jnp.int32, (tc, tc), 1)
    tri = jnp.where(col <= row, 1.0, 0.0).astype(BF16)
    c = carry_ref[...]
    for piece in _split3(lf_ref[...]):
        c = c + jnp.dot(tri, piece, preferred_element_type=F32)
    carry_ref[...] = c[tc - 1:tc, :]

    row = lax.broadcasted_iota(jnp.int32, (3 * heads, heads * hd), 0)
    lane = lax.broadcasted_iota(jnp.int32, (3 * heads, heads * hd), 1)
    place = jnp.where(lane == (row % heads) * hd + row // heads, 1.0, 0.0).astype(BF16)
    pieces = jnp.concatenate(_split3(c * (-LOG2E)), axis=1)
    aug_ref[...] = jnp.dot(pieces, place, preferred_element_type=F32).astype(BF16)


def _cumsum(logf, batch, seq, hd):
    rows, heads = logf.shape
    tc = min(CUMSUM_TILE, seq)
    per = seq // tc
    return pl.pallas_call(
        functools.partial(_cumsum_kernel, hd=hd),
        out_shape=jax.ShapeDtypeStruct((rows, heads * hd), BF16),
        grid=(batch, per),
        in_specs=[pl.BlockSpec((tc, heads), lambda b, t: (b * per + t, 0))],
        out_specs=pl.BlockSpec((tc, heads * hd), lambda b, t: (b * per + t, 0)),
        scratch_shapes=[pltpu.VMEM((1, heads), F32)],
        compiler_params=_params(("arbitrary", "arbitrary"), 24 << 20),
        name="fox_logf_cumsum",
    )(logf)


def _fox_attn_kernel(q_ref, k_ref, v_ref, aug_ref, *rest, tq, tk, ts, hd, heads_per_step, n_round):
    o_ref, vt_ref = rest[n_round], rest[-1]
    _WeightRounding.run(rest[:n_round], rest[n_round + 1:-1])
    i = pl.program_id(2)
    seq = k_ref.shape[0]
    q0 = pl.multiple_of(i * tq, tq)
    lane_tile = 128

    @pl.when(i == 0)
    def _():
        for t in range(seq // tk):
            vt_ref[:, t * tk:(t + 1) * tk] = v_ref[t * tk:(t + 1) * tk, :].astype(F32).T.astype(BF16)

    ones = jnp.where(lax.broadcasted_iota(jnp.int32, (tq, hd), 1) < 3, 1.0, 0.0).astype(BF16)
    q_ext = [jnp.concatenate([q_ref[:, hh * hd:(hh + 1) * hd], ones], axis=1)
             for hh in range(heads_per_step)]

    def step(j, carry, masked):
        carry = list(carry)
        items = [(sub, hh) for sub in range(tk // ts) for hh in range(heads_per_step)]
        n_tiles = tq // lane_tile

        def tile_range(sub):
            if masked and tq == tk:
                return sub * ts // lane_tile, min((sub + 1) * ts // lane_tile, n_tiles)
            return 0, (n_tiles if masked else 0)

        def scores(sub, hh):
            k0 = pl.multiple_of(j * tk + sub * ts, ts)
            lanes = slice(hh * hd, (hh + 1) * hd)
            k_ext = jnp.concatenate([k_ref[pl.ds(k0, ts), lanes], aug_ref[pl.ds(k0, ts), lanes]], axis=1)
            return lax.dot_general(k_ext, q_ext[hh][tile_range(sub)[0] * lane_tile:],
                                   (((1,), (1,)), ((), ())), preferred_element_type=F32)

        s_next = scores(*items[0])
        for n, (sub, hh) in enumerate(items):
            s = s_next
            if n + 1 < len(items):
                s_next = scores(*items[n + 1])
            m, l, acc = carry[3 * hh:3 * hh + 3]
            k0 = pl.multiple_of(j * tk + sub * ts, ts)
            first, full = tile_range(sub)
            lo = first * lane_tile
            m_t, l_t, p_t = ([m[:, :lo]], [l[:, :lo]], []) if lo else ([], [], [])
            for qt in range(first, n_tiles):
                ql = slice(qt * lane_tile, (qt + 1) * lane_tile)
                sq = s[:, (qt - first) * lane_tile:(qt - first + 1) * lane_tile]
                if qt < full:
                    keep = (k0 + lax.broadcasted_iota(jnp.int32, (ts, lane_tile), 0)
                            <= q0 + qt * lane_tile + lax.broadcasted_iota(jnp.int32, (ts, lane_tile), 1))
                    sq = jnp.where(keep, sq, NEG_BIG)
                mq = jnp.maximum(m[:, ql], jnp.max(sq, axis=0, keepdims=True))
                pq = jnp.exp2(sq - mq)
                l_t.append(jnp.exp2(m[:, ql] - mq) * l[:, ql] + jnp.sum(pq, axis=0, keepdims=True))
                m_t.append(mq)
                p_t.append(pq.astype(BF16))
            m_new = jnp.concatenate(m_t, axis=1)
            alpha = jnp.exp2(m[:, lo:] - m_new[:, lo:])
            vt = vt_ref[hh * hd:(hh + 1) * hd, pl.ds(k0, ts)]
            acc_hi = alpha * acc[:, lo:] + jnp.dot(vt, jnp.concatenate(p_t, axis=1),
                                                   preferred_element_type=F32)
            carry[3 * hh:3 * hh + 3] = [m_new, jnp.concatenate(l_t, axis=1),
                                        jnp.concatenate([acc[:, :lo], acc_hi], axis=1) if lo else acc_hi]
        return tuple(carry)

    init = (jnp.full((1, tq), NEG_BIG, F32), jnp.zeros((1, tq), F32),
            jnp.zeros((hd, tq), F32)) * heads_per_step
    n_full = q0 // tk
    carry = lax.fori_loop(0, n_full, lambda j, c: step(j, c, False), init)
    carry = step(n_full, carry, True)
    for hh in range(heads_per_step):
        _, l, acc = carry[3 * hh:3 * hh + 3]
        o_ref[:, hh * hd:(hh + 1) * hd] = (acc / l).T.astype(o_ref.dtype)


def _fox_attn(q, k, v, aug, batch, seq, heads, head_dim, weights_to_round):
    tq = min(ATTN_Q_TILE, seq)
    tk = min(ATTN_K_TILE, seq)
    nq = seq // tq
    hps = ATTN_HEADS_PER_STEP if heads % ATTN_HEADS_PER_STEP == 0 else 1
    groups = heads // hps
    width = hps * head_dim
    rounding = _WeightRounding(weights_to_round, batch * groups * nq,
                               lambda b, h, i: (b * groups + h) * nq + i)
    kernel = functools.partial(_fox_attn_kernel, tq=tq, tk=tk, ts=min(ATTN_K_SUB, tk), hd=head_dim,
                               heads_per_step=hps, n_round=len(rounding))
    vmem = (7 * _nbytes((seq, width), BF16) + 6 * hps * _nbytes((tk, tq), F32) + rounding.vmem_bytes()
            + (8 << 20))
    outs = pl.pallas_call(
        kernel,
        out_shape=[jax.ShapeDtypeStruct(q.shape, BF16)] + rounding.out_shapes(),
        grid=(batch, groups, nq),
        in_specs=[pl.BlockSpec((tq, width), lambda b, h, i: (b * nq + i, h)),
                  pl.BlockSpec((seq, width), lambda b, h, i: (b, h)),
                  pl.BlockSpec((seq, width), lambda b, h, i: (b, h)),
                  pl.BlockSpec((seq, width), lambda b, h, i: (b, h))] + rounding.in_specs(),
        out_specs=[pl.BlockSpec((tq, width), lambda b, h, i: (b * nq + i, h))] + rounding.out_specs(),
        scratch_shapes=[pltpu.VMEM((width, seq), BF16)],
        compiler_params=_params(("arbitrary", "arbitrary", "arbitrary"), vmem),
        name="fox_prompt_attention",
    )(q, k, v, aug, *rounding.operands())
    return outs[0], outs[1:]


def _page_suffix_kernel(lf_ref, out_ref, *, heads):
    x = lf_ref[...]
    cols = x.shape[1]
    col = lax.broadcasted_iota(jnp.int32, x.shape, 1)
    incl = x
    tot = x
    sh = heads
    while sh < cols:
        incl = incl + jnp.where(col + sh < cols, pltpu.roll(incl, cols - sh, 1), 0.0)
        tot = tot + pltpu.roll(tot, sh, 1)
        sh *= 2
    out_ref[:, :cols] = incl - x
    out_ref[:, cols:] = tot


def _page_suffix(cache_lf, heads):
    n_rows, cols = cache_lf.shape
    tr = 256 if n_rows % 256 == 0 else n_rows
    return pl.pallas_call(
        functools.partial(_page_suffix_kernel, heads=heads),
        out_shape=jax.ShapeDtypeStruct((n_rows, 2 * cols), F32),
        grid=(n_rows // tr,),
        in_specs=[pl.BlockSpec((tr, cols), lambda r: (r, 0))],
        out_specs=pl.BlockSpec((tr, 2 * cols), lambda r: (r, 0)),
        compiler_params=_params(("arbitrary",), 12 * _nbytes((tr, cols), F32) + (4 << 20)),
        name="fox_page_logf_suffix",
    )(cache_lf)


def _fox_decode_kernel(pt_ref, q_ref, kn_ref, vn_ref, cn_ref, *refs, pages, page_id):
    k_refs = refs[:pages]
    v_refs = refs[pages:2 * pages]
    bias_refs = refs[2 * pages:3 * pages]
    o_ref, m_ref, l_ref, acc_ref, run_ref = refs[3 * pages:]
    s_idx = pl.program_id(1)
    heads = q_ref.shape[0]
    cols = k_refs[0].shape[0]
    head_mask = (lax.broadcasted_iota(jnp.int32, (heads, cols), 1) % heads
                 == lax.broadcasted_iota(jnp.int32, (heads, cols), 0))

    @pl.when(s_idx == 0)
    def _():
        m_ref[...] = jnp.full_like(m_ref, NEG_BIG)
        l_ref[...] = jnp.zeros_like(l_ref)
        acc_ref[...] = jnp.zeros_like(acc_ref)
        run_ref[...] = jnp.zeros_like(run_ref)

    q = q_ref[...]
    cn = cn_ref[...]
    run = run_ref[...]
    scores = []
    for g in range(pages):
        s = lax.dot_general(q, k_refs[g][...].astype(BF16), (((1,), (1,)), ((), ())),
                            preferred_element_type=F32)
        row = page_id(pt_ref, pl.program_id(0), s_idx, g) % F32_SUBLANES
        page_bias = bias_refs[g][pl.ds(row, 1), :]
        scores.append(jnp.where(head_mask, s + (page_bias[:, :cols] + (run + cn)), NEG_BIG))
        run = run + page_bias[:, cols:]
    run_ref[...] = run
    m = m_ref[...]
    m_new = m
    for s in scores:
        m_new = jnp.maximum(m_new, jnp.max(s, axis=-1, keepdims=True))
    alpha = jnp.exp(m - m_new)
    l = alpha * l_ref[...]
    acc = alpha * acc_ref[...]
    for g in range(pages):
        p = jnp.exp(scores[g] - m_new)
        l = l + jnp.sum(p, axis=-1, keepdims=True)
        acc = acc + jnp.dot(p.astype(BF16), v_refs[g][...].astype(BF16), preferred_element_type=F32)
    m_ref[...] = m_new
    l_ref[...] = l
    acc_ref[...] = acc

    @pl.when(s_idx == pl.num_programs(1) - 1)
    def _():
        s_new = jnp.sum(q.astype(F32) * kn_ref[...], axis=-1, keepdims=True)
        m = m_ref[...]
        m_new = jnp.maximum(m, s_new)
        alpha = jnp.exp(m - m_new)
        p_new = jnp.exp(s_new - m_new)
        l = alpha * l_ref[...] + p_new
        acc = alpha * acc_ref[...] + p_new.astype(BF16).astype(F32) * vn_ref[...]
        o_ref[...] = acc / l


def _fox_decode(page_table, pool_base, q, k_new, v_new, logf_new, cache_k, cache_v, cache_lf):
    bs, n_pages = page_table.shape
    _, heads, hd = q.shape
    n_pool, cols, _ = cache_k.shape
    assert n_pool % F32_SUBLANES == 0 and pool_base % F32_SUBLANES == 0
    page_bias = _page_suffix(cache_lf, heads)
    g_pages = PAGES_PER_STEP if n_pages % PAGES_PER_STEP == 0 else 1
    steps = n_pages // g_pages

    def page_id(pt, b, s, g):
        return pool_base + pt[b * n_pages + (n_pages - 1 - (s * g_pages + g))]

    kernel = functools.partial(_fox_decode_kernel, pages=g_pages, page_id=page_id)

    def page_idx(g):
        return lambda b, s, pt: (page_id(pt, b, s, g), 0, 0)

    def row_block_idx(g):
        return lambda b, s, pt: (page_id(pt, b, s, g) // F32_SUBLANES, 0)

    tok = pl.BlockSpec((None, heads, hd), lambda b, s, pt: (b, 0, 0))
    in_specs = [tok, tok, tok, pl.BlockSpec((None, 1, cols), lambda b, s, pt: (b, 0, 0))]
    in_specs += [pl.BlockSpec((None, cols, hd), page_idx(g)) for g in range(g_pages)]
    in_specs += [pl.BlockSpec((None, cols, hd), page_idx(g)) for g in range(g_pages)]
    in_specs += [pl.BlockSpec((F32_SUBLANES, 2 * cols), row_block_idx(g)) for g in range(g_pages)]
    vmem = (4 * g_pages * _nbytes((cols, hd), F32) + 2 * g_pages * _nbytes((cols, hd), BF16)
            + 4 * g_pages * _nbytes((heads, cols), F32) + (4 << 20))
    grid_spec = pltpu.PrefetchScalarGridSpec(
        num_scalar_prefetch=1,
        grid=(bs, steps),
        in_specs=in_specs,
        out_specs=pl.BlockSpec((None, heads, hd), lambda b, s, pt: (b, 0, 0)),
        scratch_shapes=[pltpu.VMEM((heads, 1), F32), pltpu.VMEM((heads, 1), F32),
                        pltpu.VMEM((heads, hd), F32), pltpu.VMEM((1, cols), F32)])
    return pl.pallas_call(
        kernel,
        out_shape=jax.ShapeDtypeStruct((bs, heads, hd), F32),
        grid_spec=grid_spec,
        compiler_params=_params(("arbitrary", "arbitrary"), vmem),
        name="fox_decode_attention",
    )(page_table.reshape(-1), q, k_new, v_new, logf_new,
      *([cache_k] * g_pages), *([cache_v] * g_pages), *([page_bias] * g_pages))


def _proj_residual_kernel(a_ref, w_ref, x_ref, gate_ref, o_ref):
    z = jnp.dot(a_ref[...], w_ref[...].astype(BF16), preferred_element_type=F32)
    o_ref[...] = x_ref[...] + gate_ref[...] * z


def _proj_residual(a, w, layer, x, mods, mod_rows, tiles_per_group, gate_chunk, tm):
    m_rows, k = a.shape
    d = w.shape[2]
    tn = min(OUT_PROJ_COL_TILE, d)
    vmem = (2 * _nbytes((tm, k), BF16) + 2 * _nbytes((k, tn), w.dtype) + _nbytes((k, tn), BF16)
            + 5 * _nbytes((tm, tn), F32) + (4 << 20))
    return pl.pallas_call(
        _proj_residual_kernel,
        out_shape=jax.ShapeDtypeStruct((m_rows, d), F32),
        grid=(m_rows // tm, d // tn),
        in_specs=[pl.BlockSpec((tm, k), lambda m, n: (m, 0)),
                  pl.BlockSpec((None, k, tn), lambda m, n: (layer, 0, n)),
                  pl.BlockSpec((tm, tn), lambda m, n: (m, n)),
                  pl.BlockSpec((None, mod_rows, tn),
                               lambda m, n: (m // tiles_per_group, 0, gate_chunk * (d // tn) + n))],
        out_specs=pl.BlockSpec((tm, tn), lambda m, n: (m, n)),
        compiler_params=_params(("arbitrary", "arbitrary"), vmem),
        name="mixer_out_proj",
    )(a, w, x, mods)


def _mlp_kernel(x_ref, xs_ref, g_ref, mod_ref, mods_ref, wu_ref, wd_ref, fg_ref, o_ref, os_ref, h_ref, *,
                tm, final_norm):
    f = pl.program_id(1)
    d = x_ref.shape[1]

    @pl.when(f == 0)
    def _():
        h_ref[:tm, :] = _norm_mod(x_ref[...], g_ref[...], mod_ref[:, :d], mod_ref[:, d:2 * d]).astype(BF16)
        h_ref[tm:, :] = _norm_mod(xs_ref[...], g_ref[...], mods_ref[:, :d],
                                  mods_ref[:, d:2 * d]).astype(BF16)
        o_ref[...] = jnp.zeros_like(o_ref)
        os_ref[...] = jnp.zeros_like(os_ref)

    u = jnp.dot(h_ref[...], wu_ref[...].astype(BF16), preferred_element_type=F32)
    u = jnp.square(jnp.maximum(u, 0.0)).astype(BF16)
    z = jnp.dot(u, wd_ref[...].astype(BF16), preferred_element_type=F32)
    o_ref[...] += z[:tm]
    os_ref[...] += z[tm:]

    @pl.when(f == pl.num_programs(1) - 1)
    def _():
        for x_r, mod_r, o_r in ((x_ref, mod_ref, o_ref), (xs_ref, mods_ref, os_ref)):
            y = x_r[...] + mod_r[:, 2 * d:] * o_r[...]
            if final_norm:
                y = y * lax.rsqrt(jnp.mean(y * y, axis=-1, keepdims=True) + EPS) * fg_ref[...]
            o_r[...] = y


def _mlp(x, xs, g, mods_p, mods_s, tiles_per_batch, w_up, w_down, layer, final_g, final_norm, tm):
    m_rows, d = x.shape
    srows = xs.shape[0]
    d_ff = w_up.shape[2]
    tf = min(COL_TILE, d_ff)
    kernel = functools.partial(_mlp_kernel, tm=tm, final_norm=final_norm)
    x_buffers = 1 if w_up.dtype == F32 else 2
    x_mode = {"pipeline_mode": pl.Buffered(1)} if x_buffers == 1 else {}
    vmem = ((2 + x_buffers) * _nbytes((tm, d), F32) + _nbytes((tm + srows, d), BF16)
            + 4 * _nbytes((d, tf), w_up.dtype) + 2 * _nbytes((d, tf), BF16) + 4 * _nbytes((tm, tf), F32)
            + (4 << 20))
    full = lambda m, f: (0, 0)
    vec = pl.BlockSpec((1, d), full)
    return pl.pallas_call(
        kernel,
        out_shape=(jax.ShapeDtypeStruct((m_rows, d), F32), jax.ShapeDtypeStruct((srows, d), F32)),
        grid=(m_rows // tm, d_ff // tf),
        in_specs=[pl.BlockSpec((tm, d), lambda m, f: (m, 0), **x_mode),
                  pl.BlockSpec((srows, d), full),
                  vec,
                  _mod_spec(1, d, 3, tiles_per_batch, 3),
                  _mod_spec(srows, d, 3, m_rows, 3),
                  pl.BlockSpec((None, d, tf), lambda m, f: (layer, 0, f)),
                  pl.BlockSpec((None, tf, d), lambda m, f: (layer, f, 0)),
                  vec],
        out_specs=(pl.BlockSpec((tm, d), lambda m, f: (m, 0)), pl.BlockSpec((srows, d), full)),
        scratch_shapes=[pltpu.VMEM((tm + srows, d), BF16)],
        compiler_params=_params(("arbitrary", "arbitrary"), vmem),
        name="relu2_mlp",
    )(x, xs, g, mods_p, mods_s, w_up, w_down, final_g)


def _ret_proj_kernel(x_ref, xs_ref, g_ref, mod_ref, mods_ref, w_ref, cs_ref, css_ref,
                     q_ref, k_ref, v_ref, gt_ref, zs_ref, h_ref, *, nq, tm, key_dim, k_scale):
    n = pl.program_id(1)
    d = x_ref.shape[1]

    @pl.when(n == 0)
    def _():
        h_ref[:tm, :] = _norm_mod(x_ref[...], g_ref[...], mod_ref[:, :d], mod_ref[:, d:]).astype(BF16)
        h_ref[tm:, :] = _norm_mod(xs_ref[...], g_ref[...], mods_ref[:, :d], mods_ref[:, d:]).astype(BF16)

    z = jnp.dot(h_ref[...], w_ref[...].astype(BF16), preferred_element_type=F32)
    zp = z[:tm]
    zs = z[tm:]
    half = key_dim // 2

    def rotate(zz, table_ref):
        cos = table_ref[:, :half]
        sin = table_ref[:, half:]
        parts = []
        for hh in range(zz.shape[1] // key_dim):
            x1 = zz[:, hh * key_dim:hh * key_dim + half]
            x2 = zz[:, hh * key_dim + half:(hh + 1) * key_dim]
            parts += [x1 * cos - x2 * sin, x1 * sin + x2 * cos]
        return jnp.concatenate(parts, axis=1)

    last_tile = pl.program_id(0) == pl.num_programs(0) - 1
    in_q = n < nq
    in_k = jnp.logical_and(n >= nq, n < 2 * nq)

    @pl.when(in_q)
    def _():
        q_ref[...] = rotate(zp, cs_ref).astype(BF16)

    @pl.when(in_k)
    def _():
        k_ref[...] = rotate(zp, cs_ref) * k_scale

    @pl.when(jnp.logical_and(n >= 2 * nq, n < 4 * nq))
    def _():
        v_ref[...] = zp.astype(BF16)

    @pl.when(n >= 4 * nq)
    def _():
        gt_ref[...] = zp

    @pl.when(jnp.logical_and(last_tile, n < 2 * nq))
    def _():
        zs_ref[...] = rotate(zs, css_ref) * jnp.where(in_k, k_scale, 1.0)

    @pl.when(jnp.logical_and(last_tile, n >= 2 * nq))
    def _():
        zs_ref[...] = zs


def _ret_proj(x, xs, g, mods_p, mods_s, tiles_per_batch, w_in, layer, rot, rot_s, key_dim, tm):
    m_rows, d = x.shape
    srows = xs.shape[0]
    tn = min(COL_TILE, d)
    nq = d // tn
    kernel = functools.partial(_ret_proj_kernel, nq=nq, tm=tm, key_dim=key_dim, k_scale=key_dim ** -0.5)
    last_tile = m_rows // tm - 1

    def col(lo, width):
        return lambda m, n: (m, jnp.clip(n - lo, 0, width - 1))

    vmem = (2 * _nbytes((tm, d), F32) + _nbytes((tm + srows, d), BF16) + 2 * _nbytes((d, tn), F32)
            + _nbytes((d, tn), BF16) + 2 * (2 * _nbytes((tm, tn), BF16) + 2 * _nbytes((tm, tn), F32))
            + 2 * _nbytes((tm, key_dim), F32) + 4 * _nbytes((tm, tn), F32) + (4 << 20))
    full = lambda m, n: (0, 0)
    outs = pl.pallas_call(
        kernel,
        out_shape=(jax.ShapeDtypeStruct((m_rows, d), BF16), jax.ShapeDtypeStruct((m_rows, d), F32),
                   jax.ShapeDtypeStruct((m_rows, 2 * d), BF16), jax.ShapeDtypeStruct((m_rows, 2 * d), F32),
                   jax.ShapeDtypeStruct((srows, 6 * d), F32)),
        grid=(m_rows // tm, 6 * nq),
        in_specs=[pl.BlockSpec((tm, d), lambda m, n: (m, 0)),
                  pl.BlockSpec((srows, d), full),
                  pl.BlockSpec((1, d), full),
                  _mod_spec(1, d, 0, tiles_per_batch, 2),
                  _mod_spec(srows, d, 0, m_rows, 2),
                  pl.BlockSpec((None, d, tn), lambda m, n: (layer, 0, n)),
                  pl.BlockSpec((tm, key_dim), lambda m, n: (m % tiles_per_batch, 0)),
                  pl.BlockSpec((srows, key_dim), full)],
        out_specs=(pl.BlockSpec((tm, tn), col(0, nq)),
                   pl.BlockSpec((tm, tn), col(nq, nq)),
                   pl.BlockSpec((tm, tn), col(2 * nq, 2 * nq)),
                   pl.BlockSpec((tm, tn), col(4 * nq, 2 * nq)),
                   pl.BlockSpec((srows, tn), lambda m, n: (0, jnp.where(m == last_tile, n, 0)))),
        scratch_shapes=[pltpu.VMEM((tm + srows, d), BF16)],
        compiler_params=_params(("arbitrary", "arbitrary"), vmem),
        name="ret_in_proj",
    )(x, xs, g, mods_p, mods_s, w_in, rot, rot_s)
    return outs[:4], outs[4]


def _group_norm_gate(o, gate):
    mu = jnp.mean(o, axis=-1, keepdims=True)
    cen = o - mu
    var = jnp.mean(cen * cen, axis=-1, keepdims=True)
    return _silu(gate) * (cen * lax.rsqrt(var + EPS))


def _ret_chunk_kernel(lg_ref, q_ref, k_ref, v_ref, gt_ref, *rest, n_round, heads_per_step):
    y_ref, s_ref = rest[n_round:n_round + 2]
    dec_ref, xi_ref, zeta_ref = rest[-3:]
    _WeightRounding.run(rest[:n_round], rest[n_round + 2:-3])
    ln = q_ref.shape[0]
    kd = s_ref.shape[1]
    vd = s_ref.shape[2]
    lgs = [jnp.full((1, 1), lg_ref[pl.program_id(1) * heads_per_step + hh], F32)
           for hh in range(heads_per_step)]

    @pl.when(pl.program_id(2) == 0)
    def _():
        s_ref[...] = jnp.zeros_like(s_ref)
        diff = (lax.broadcasted_iota(jnp.int32, (ln, ln), 0)
                - lax.broadcasted_iota(jnp.int32, (ln, ln), 1)).astype(F32)
        n = lax.broadcasted_iota(jnp.int32, (ln, 1), 0).astype(F32)
        for hh, lg in enumerate(lgs):
            dec_ref[hh] = jnp.where(diff >= 0, jnp.exp(lg * jnp.maximum(diff, 0.0)), 0.0)
            xi_ref[hh] = jnp.exp(lg * (n + 1.0))
            zeta_ref[hh] = jnp.exp(lg * (ln - 1.0 - n))

    for hh, lg in enumerate(lgs):
        q = q_ref[:, hh * kd:(hh + 1) * kd]
        k = k_ref[:, hh * kd:(hh + 1) * kd]
        v = v_ref[:, hh * vd:(hh + 1) * vd]
        s0 = s_ref[hh]
        qk = lax.dot_general(q, k.astype(BF16), (((1,), (1,)), ((), ())), preferred_element_type=F32)
        a = (qk * dec_ref[hh]).astype(BF16)
        o = (jnp.dot(a, v, preferred_element_type=F32)
             + jnp.dot(q, s0.astype(BF16), preferred_element_type=F32) * xi_ref[hh])
        kz = (k * zeta_ref[hh]).astype(BF16)
        s_ref[hh] = jnp.exp(lg * ln) * s0 + lax.dot_general(
            kz, v, (((0,), (0,)), ((), ())), preferred_element_type=F32)
        y_ref[:, hh * vd:(hh + 1) * vd] = _group_norm_gate(
            o, gt_ref[:, hh * vd:(hh + 1) * vd]).astype(y_ref.dtype)


def _ret_prompt(lg, q, k, v, gate, batch, seq, heads, key_dim, val_dim, weights_to_round):
    ln = min(RET_CHUNK, seq)
    nc = seq // ln
    hps = RET_HEADS_PER_STEP if heads % RET_HEADS_PER_STEP == 0 else 1
    groups = heads // hps
    rounding = _WeightRounding(weights_to_round, batch * groups * nc,
                               lambda b, h, c, *_: (b * groups + h) * nc + c)
    vmem = (hps * (2 * (_nbytes((ln, key_dim), BF16) + _nbytes((ln, key_dim), F32)
                        + _nbytes((ln, val_dim), BF16) + _nbytes((ln, val_dim), F32)
                        + _nbytes((ln, val_dim), BF16))
                   + 4 * _nbytes((key_dim, val_dim), F32) + 8 * _nbytes((ln, val_dim), F32)
                   + _nbytes((ln, ln), F32))
            + rounding.vmem_bytes() + (8 << 20))
    grid_spec = pltpu.PrefetchScalarGridSpec(
        num_scalar_prefetch=1,
        grid=(batch, groups, nc),
        in_specs=[pl.BlockSpec((ln, hps * key_dim), lambda b, h, c, lg: (b * nc + c, h)),
                  pl.BlockSpec((ln, hps * key_dim), lambda b, h, c, lg: (b * nc + c, h)),
                  pl.BlockSpec((ln, hps * val_dim), lambda b, h, c, lg: (b * nc + c, h)),
                  pl.BlockSpec((ln, hps * val_dim), lambda b, h, c, lg: (b * nc + c, h))]
        + rounding.in_specs(),
        out_specs=[pl.BlockSpec((ln, hps * val_dim), lambda b, h, c, lg: (b * nc + c, h)),
                   pl.BlockSpec((None, hps, key_dim, val_dim), lambda b, h, c, lg: (b, h, 0, 0))]
        + rounding.out_specs(),
        scratch_shapes=[pltpu.VMEM((hps, ln, ln), F32), pltpu.VMEM((hps, ln, 1), F32),
                        pltpu.VMEM((hps, ln, 1), F32)])
    outs = pl.pallas_call(
        functools.partial(_ret_chunk_kernel, n_round=len(rounding), heads_per_step=hps),
        out_shape=[jax.ShapeDtypeStruct((batch * seq, heads * val_dim), BF16),
                   jax.ShapeDtypeStruct((batch, heads, key_dim, val_dim), F32)] + rounding.out_shapes(),
        grid_spec=grid_spec,
        compiler_params=_params(("arbitrary", "arbitrary", "arbitrary"), vmem),
        name="ret_prompt_chunks",
    )(lg, q, k, v, gate, *rounding.operands())
    return outs[0], outs[1], outs[2:]


def _ret_decode_kernel(lg_ref, q_ref, k_ref, v_ref, gt_ref, s0_ref, y_ref, s_ref):
    q = q_ref[...]
    k = k_ref[...]
    q_cols = q.T
    k_cols = k.T
    a_all = jnp.sum(q * k, axis=-1, keepdims=True)
    for h in range(s0_ref.shape[0]):
        gamma = jnp.exp(jnp.full((1, 1), lg_ref[h], F32))
        s0 = s0_ref[h]
        v = v_ref[h:h + 1, :]
        qs = jnp.sum(q_cols[:, h:h + 1] * s0, axis=0, keepdims=True)
        o = a_all[h:h + 1, :] * v + qs * gamma
        s_ref[h] = gamma * s0 + k_cols[:, h:h + 1] * v
        y_ref[h:h + 1, :] = _group_norm_gate(o, gt_ref[h:h + 1, :])


def _ret_decode(lg, q, k, v, gate, state):
    bs, heads, key_dim, val_dim = state.shape

    def vec(width):
        return pl.BlockSpec((None, heads, width), lambda b, lg: (b, 0, 0))

    state_spec = pl.BlockSpec((None, heads, key_dim, val_dim), lambda b, lg: (b, 0, 0, 0))
    grid_spec = pltpu.PrefetchScalarGridSpec(
        num_scalar_prefetch=1,
        grid=(bs,),
        in_specs=[vec(key_dim), vec(key_dim), vec(val_dim), vec(val_dim), state_spec],
        out_specs=(vec(val_dim), state_spec))
    y, s = pl.pallas_call(
        _ret_decode_kernel,
        out_shape=(jax.ShapeDtypeStruct((bs, heads, val_dim), F32),
                   jax.ShapeDtypeStruct(state.shape, F32)),
        grid_spec=grid_spec,
        compiler_params=_params(("arbitrary",), 4 * _nbytes(state.shape[1:], F32) + (16 << 20)),
        name="ret_decode_step",
    )(lg, q.reshape(bs, heads, key_dim), k.reshape(bs, heads, key_dim),
      v.reshape(bs, heads, val_dim), gate.reshape(bs, heads, val_dim), state)
    return y.reshape(bs, heads * val_dim), s


def _rotary_table(pos, key_dim):
    half = key_dim // 2
    inv = (np.float32(1.0) / np.float32(ROPE_BASE) ** np.linspace(0.0, 1.0, half, dtype=np.float32))
    ang = (np.asarray(pos, np.float32)[:, None] * inv.astype(np.float32)[None, :]).astype(np.float64)
    return jnp.asarray(np.concatenate([np.cos(ang), np.sin(ang)], axis=1), F32)


def _pad_rows(a, rows):
    return jnp.pad(a, ((0, rows - a.shape[0]), (0, 0)))


def kernel(x_prompt, x_sample, c_prompt, c_sample, cache_fox_k, cache_fox_v, cache_fox_logf, state_ret, page_table, norm1_g, norm2_g, ada_w, ada_b, fox_w_in, fox_b_f, fox_w_out, ret_w_in, ret_w_out, mlp_w_up, mlp_w_down, final_g):
    bp, tp, d = x_prompt.shape
    bs, ts, _ = x_sample.shape
    assert ts == 1, "the decode kernels handle one new token per sequence"
    depth = ada_w.shape[0]
    _, n_pool, page, fox_heads, fox_hd = cache_fox_k.shape
    _, _, ret_heads, key_dim, val_dim = state_ret.shape
    past_len = page_table.shape[1] * page
    srows = -(-bs // BF16_SUBLANES) * BF16_SUBLANES
    tm = min(ROW_TILE, tp)
    tiles_p = tp // tm

    mod_rows = -(-(bs + bp) // 8) * 8
    c_all = _pad_rows(jnp.concatenate([c_sample, c_prompt], axis=0), max(mod_rows, srows))
    mods = _ada(c_all, ada_w, ada_b)

    lg = jnp.log1p(-jnp.exp2(-5.0 - jnp.arange(ret_heads, dtype=F32)))
    rot_p = _rotary_table(np.arange(tp), key_dim)
    rot_s = _rotary_table(np.full((srows,), past_len), key_dim)

    yp = x_prompt.reshape(bp * tp, d)
    ys = _pad_rows(x_sample.reshape(bs, d), srows)
    fg = final_g.reshape(1, d)
    outs = {name: [] for name in ("kp", "vp", "lfp", "ks", "vs", "lfs", "sp", "ss")}

    ret_rounded = None
    for i in range(depth):
        mods_p = mods[i, bs:bs + bp].reshape(bp, 1, 6 * d)
        mods_s = mods[i, :srows].reshape(1, srows, 6 * d)
        g1 = norm1_g[i].reshape(1, d)
        g2 = norm2_g[i].reshape(1, d)
        j = i // 2
        if i % 2 == 0:
            b_f = fox_b_f[j].reshape(1, fox_heads)
            (q, kf, vf, kb, vb, lf), sample = _fox_proj(
                yp, ys, g1, mods_p, mods_s, tiles_p, fox_w_in, j, b_f, tm,
                fox_hd ** -0.5 * LOG2E, fox_hd ** -0.5)
            aug = _cumsum(lf, bp, tp, fox_hd)
            to_round = [(fox_w_out, j), (mlp_w_up, i), (mlp_w_down, i)]
            if i + 1 < depth:
                jr = (i + 1) // 2
                to_round += [(ret_w_in, jr), (ret_w_out, jr), (mlp_w_up, i + 1), (mlp_w_down, i + 1)]
            o, rounded = _fox_attn(q, kb, vb, aug, bp, tp, fox_heads, fox_hd, to_round)
            w_out, w_up, w_down = rounded[:3]
            l_mlp = 0
            ret_rounded = rounded[3:] if i + 1 < depth else None
            yp = _proj_residual(o, w_out, 0, yp, mods_p, 1, tiles_p, 2, tm)
            outs["kp"].append(kf.reshape(bp, tp, fox_heads, fox_hd))
            outs["vp"].append(vf.reshape(bp, tp, fox_heads, fox_hd))
            outs["lfp"].append(lf.reshape(bp, tp, fox_heads))
            zs, lf = sample
            kf, vf = zs[:, d:2 * d], zs[:, 2 * d:]
            bf16_rounded = lambda a: a.astype(BF16).astype(F32)
            o = _fox_decode(
                page_table, j * n_pool,
                zs[:bs, :d].astype(BF16).reshape(bs, fox_heads, fox_hd),
                bf16_rounded(kf[:bs]).reshape(bs, fox_heads, fox_hd),
                bf16_rounded(vf[:bs]).reshape(bs, fox_heads, fox_hd),
                jnp.tile(lf[:bs], (1, page)).reshape(bs, 1, page * fox_heads),
                cache_fox_k.reshape(-1, page * fox_heads, fox_hd),
                cache_fox_v.reshape(-1, page * fox_heads, fox_hd),
                cache_fox_logf.reshape(-1, page * fox_heads))
            o = _pad_rows(o.reshape(bs, d), srows).astype(BF16)
            ys = _proj_residual(o, w_out, 0, ys, mods_s, srows, 1, 2, srows)
            outs["ks"].append(kf[:bs].reshape(bs, ts, fox_heads, fox_hd))
            outs["vs"].append(vf[:bs].reshape(bs, ts, fox_heads, fox_hd))
            outs["lfs"].append(lf[:bs].reshape(bs, ts, fox_heads))
        else:
            if ret_rounded is None:
                w_in, w_out, w_up, w_down = ret_w_in, ret_w_out, mlp_w_up, mlp_w_down
                l_mix, l_mlp = j, i
            else:
                w_in, w_out, w_up, w_down = ret_rounded
                l_mix = l_mlp = 0
            (q, k, v, gate), zs = _ret_proj(yp, ys, g1, mods_p, mods_s, tiles_p, w_in, l_mix,
                                            rot_p, rot_s, key_dim, tm)
            y, s, _ = _ret_prompt(lg, q, k, v, gate, bp, tp, ret_heads, key_dim, val_dim, [])
            yp = _proj_residual(y, w_out, l_mix, yp, mods_p, 1, tiles_p, 2, tm)
            outs["sp"].append(s)
            zs = zs[:bs]
            bf16_rounded = lambda a: a.astype(BF16).astype(F32)
            y, s = _ret_decode(lg, bf16_rounded(zs[:, :d]), zs[:, d:2 * d], bf16_rounded(zs[:, 2 * d:4 * d]),
                               zs[:, 4 * d:], state_ret[j].astype(F32))
            ys = _proj_residual(_pad_rows(y, srows).astype(BF16), w_out, l_mix, ys, mods_s, srows, 1, 2, srows)
            outs["ss"].append(s)
        yp, ys = _mlp(yp, ys, g2, mods_p, mods_s, tiles_p, w_up, w_down, l_mlp, fg, i == depth - 1, tm)

    return (yp.reshape(bp, tp, d), ys[:bs].reshape(bs, ts, d),
            jnp.stack(outs["kp"]), jnp.stack(outs["vp"]), jnp.stack(outs["lfp"]),
            jnp.stack(outs["ks"]), jnp.stack(outs["vs"]), jnp.stack(outs["lfs"]),
            jnp.stack(outs["sp"]), jnp.stack(outs["ss"]))
```
